```python
import math
import jax, jax.numpy as jnp
from jax import lax
import numpy as np

D_MODEL = 1024
BATCH = 2
SEQ = 8192
DEPTH = 1

A_HEADS = 8
A_QK_DIM = 64
A_V_DIM = 64
A_Q_RANK = 256
A_KV_RANK = 128
IDX_HEADS = 8
IDX_DIM = 64
TOPK_MAX = 256
Q_BLOCK = 128
DN_HEADS = 8
DN_K_DIM = 64
DN_V_DIM = 64
DN_CONV = 4
DN_CHUNK = 64
D_FF = 2816
FFN_CONV = 3

RMS_EPS = 1e-6
LN_EPS = 1e-5
ALPHA = (2 * DEPTH) ** 0.25
BETA_INIT = (8 * DEPTH) ** -0.25

A_Q_W = A_HEADS * A_QK_DIM
A_OUT_W = A_HEADS * A_V_DIM
DN_KEY_W = DN_HEADS * DN_K_DIM
DN_VAL_W = DN_HEADS * DN_V_DIM
DN_QKV_W = 2 * DN_KEY_W + DN_VAL_W
MIX_W = A_OUT_W + DN_VAL_W
IN_SIZES = (A_Q_RANK, A_KV_RANK, IDX_DIM, IDX_HEADS, DN_QKV_W, DN_VAL_W, DN_HEADS, DN_HEADS)
IN_W = sum(IN_SIZES)

kernel_name = "hymba_dsa_gdn_convffn_deepnorm"


def _split(t, sizes):
    idx = [int(i) for i in np.cumsum(sizes)[:-1]]
    return jnp.split(t, idx, axis=-1)


def _rms_norm(x, g):
    xf = x.astype(jnp.float32)
    y = xf * lax.rsqrt(jnp.mean(xf * xf, axis=-1, keepdims=True) + RMS_EPS)
    return (y * g.astype(jnp.float32)).astype(x.dtype)


def _layer_norm(x, g, b):
    xf = x.astype(jnp.float32)
    mu = jnp.mean(xf, axis=-1, keepdims=True)
    var = jnp.mean(jnp.square(xf - mu), axis=-1, keepdims=True)
    y = (xf - mu) * lax.rsqrt(var + LN_EPS)
    return (y * g.astype(jnp.float32) + b.astype(jnp.float32)).astype(x.dtype)


def _l2norm(x):
    return x * lax.rsqrt(jnp.sum(x * x, axis=-1, keepdims=True) + RMS_EPS)


def _causal_dwconv(x, w):
    k_w, c = w.shape
    return lax.conv_general_dilated(
        x, w[:, None, :].astype(x.dtype), window_strides=(1,), padding=[(k_w - 1, 0)],
        dimension_numbers=('NWC', 'WIO', 'NWC'), feature_group_count=c)


def _dsa_attention(c_q, c_kv, k_idx, w_idx, q_norm_g, w_uq, w_qidx, kv_norm_g, w_uk, w_uv,
                   kidx_ln_g, kidx_ln_b):
    B, S, _ = c_q.shape
    n_blk = S // Q_BLOCK
    top_k = min(TOPK_MAX, S // 4)
    c_q = _rms_norm(c_q, q_norm_g)
    q = (c_q @ w_uq).reshape(B, S, A_HEADS, A_QK_DIM)
    q_lat = jnp.einsum('bshd,hdc->bshc', q, w_uk) * (A_QK_DIM ** -0.5)
    q_idx = (c_q @ w_qidx).reshape(B, S, IDX_HEADS, IDX_DIM)
    c_kv = _rms_norm(c_kv, kv_norm_g)
    k_idx = _layer_norm(k_idx, kidx_ln_g, kidx_ln_b)
    w_idx = w_idx * (IDX_HEADS ** -0.5 * IDX_DIM ** -0.5)
    key_pos = jnp.arange(S)

    def to_blocks(t):
        return t.reshape((B, n_blk, Q_BLOCK) + t.shape[2:]).swapaxes(0, 1)

    def block(args):
        ql, qi, wi, blk = args
        q_pos = blk * Q_BLOCK + jnp.arange(Q_BLOCK)
        rel = jax.nn.relu(jnp.einsum('bqhd,bsd->bqhs', qi, k_idx))
        score = jnp.einsum('bqhs,bqh->bqs', rel, wi).astype(jnp.float32)
        causal = key_pos[None, :] <= q_pos[:, None]
        score = jnp.where(causal[None], score, -jnp.inf)
        _, sel = lax.top_k(score, top_k)
        valid = sel <= q_pos[None, :, None]
        c_sel = jax.vmap(lambda c, i: c[i])(c_kv, sel)
        logits = jnp.einsum('bqhc,bqkc->bqhk', ql, c_sel).astype(jnp.float32)
        logits = jnp.where(valid[:, :, None, :], logits, -jnp.inf)
        p = jax.nn.softmax(logits, axis=-1).astype(c_sel.dtype)
        o_lat = jnp.einsum('bqhk,bqkc->bqhc', p, c_sel)
        return jnp.einsum('bqhc,hcd->bqhd', o_lat, w_uv)

    o = lax.map(block, (to_blocks(q_lat), to_blocks(q_idx), to_blocks(w_idx), jnp.arange(n_blk)))
    return o.swapaxes(0, 1).reshape(B, S, A_OUT_W)


def _chunk_gated_delta_rule(q, k, v, beta, g):
    B, S, H, dk = q.shape
    dv = v.shape[-1]
    C = DN_CHUNK
    N = S // C

    def chunks(t):
        return t.reshape((B, N, C, H) + t.shape[3:]).transpose((0, 3, 1, 2) + tuple(range(4, t.ndim + 1)))

    q, k, v = chunks(q), chunks(k), chunks(v)
    beta, g = chunks(beta), chunks(g)
    g = jnp.cumsum(g, axis=-1)
    tri_incl = jnp.tril(jnp.ones((C, C), dtype=bool))
    tri_strict = jnp.tril(jnp.ones((C, C), dtype=bool), -1)
    decay = jnp.exp(jnp.where(tri_incl, g[..., :, None] - g[..., None, :], -jnp.inf))
    k_beta = k * beta[..., None]
    v_beta = v * beta[..., None]
    lower = jnp.where(tri_strict, jnp.einsum('bhnid,bhnjd->bhnij', k_beta, k) * decay, 0.0)
    a_mat = lower + jnp.eye(C, dtype=q.dtype)
    rhs = jnp.concatenate([v_beta, k_beta * jnp.exp(g)[..., None]], axis=-1)
    sol = lax.linalg.triangular_solve(a_mat, rhs, left_side=True, lower=True, unit_diagonal=True)
    u, w = sol[..., :dv], sol[..., dv:]
    attn = jnp.where(tri_incl, jnp.einsum('bhnid,bhnjd->bhnij', q, k) * decay, 0.0)
    q_dec = q * jnp.exp(g)[..., None]
    k_end = k * jnp.exp(g[..., -1:] - g)[..., None]
    chunk_dec = jnp.exp(g[..., -1])

    def step(state, inp):
        u_n, w_n, qd_n, a_n, ke_n, cd_n = inp
        v_new = u_n - jnp.einsum('bhcd,bhde->bhce', w_n, state)
        o_n = jnp.einsum('bhcd,bhde->bhce', qd_n, state) + jnp.einsum('bhij,bhje->bhie', a_n, v_new)
        state = state * cd_n[..., None, None] + jnp.einsum('bhcd,bhce->bhde', ke_n, v_new)
        return state, o_n

    lead = lambda t: jnp.moveaxis(t, 2, 0)
    state0 = jnp.zeros((B, H, dk, dv), dtype=q.dtype)
    _, o = lax.scan(step, state0, (lead(u), lead(w), lead(q_dec), lead(attn), lead(k_end), lead(chunk_dec)))
    return o.transpose(1, 0, 3, 2, 4).reshape(B, S, H, dv)


def _gated_deltanet(qkv, z, b_raw, a_raw, conv_w, a_log, dt_bias, norm_g):
    B, S, _ = qkv.shape
    qkv = jax.nn.silu(_causal_dwconv(qkv, conv_w))
    q, k, v = _split(qkv.astype(jnp.float32), (DN_KEY_W, DN_KEY_W, DN_VAL_W))
    q = _l2norm(q.reshape(B, S, DN_HEADS, DN_K_DIM)) * (DN_K_DIM ** -0.5)
    k = _l2norm(k.reshape(B, S, DN_HEADS, DN_K_DIM))
    v = v.reshape(B, S, DN_HEADS, DN_V_DIM)
    beta = jax.nn.sigmoid(b_raw.astype(jnp.float32))
    g = -jnp.exp(a_log.astype(jnp.float32)) * jax.nn.softplus(a_raw.astype(jnp.float32) + dt_bias.astype(jnp.float32))
    o = _chunk_gated_delta_rule(q, k, v, beta, g)
    o = o * lax.rsqrt(jnp.mean(o * o, axis=-1, keepdims=True) + RMS_EPS) * norm_g.astype(jnp.float32)
    o = o * jax.nn.silu(z.astype(jnp.float32).reshape(B, S, DN_HEADS, DN_V_DIM))
    return o.reshape(B, S, DN_VAL_W).astype(qkv.dtype)


def _conv_glu_ffn(h, w_in, conv_w, conv_b, w_down):
    gate, up = _split(h @ w_in, (D_FF, D_FF))
    gate = _causal_dwconv(gate, conv_w) + conv_b
    return (jax.nn.silu(gate) * up) @ w_down


def setup_inputs(seed: int = 0) -> dict:
    key = jax.random.key(seed)
    ks = jax.random.split(key, 32)
    nrm = lambda k, shape, s: jax.random.normal(k, shape, jnp.float32) * s
    gain = lambda k, n: 1.0 + 0.02 * jax.random.normal(k, (DEPTH, n), jnp.float32)
    dt = jnp.exp(jax.random.uniform(ks[14], (DEPTH, DN_HEADS), jnp.float32, math.log(1e-3), math.log(1e-1)))
    return {
        "x": nrm(ks[0], (BATCH, SEQ, D_MODEL), 1.0),
        "w_in": nrm(ks[1], (DEPTH, D_MODEL, IN_W), D_MODEL ** -0.5),
        "q_norm_g": gain(ks[2], A_Q_RANK),
        "w_uq": nrm(ks[3], (DEPTH, A_Q_RANK, A_Q_W), A_Q_RANK ** -0.5),
        "w_qidx": nrm(ks[4], (DEPTH, A_Q_RANK, IDX_HEADS * IDX_DIM), A_Q_RANK ** -0.5),
        "kv_norm_g": gain(ks[5], A_KV_RANK),
        "w_uk": nrm(ks[6], (DEPTH, A_HEADS, A_QK_DIM, A_KV_RANK), A_QK_DIM ** -0.5),
        "w_uv": nrm(ks[7], (DEPTH, A_HEADS, A_KV_RANK, A_V_DIM), A_KV_RANK ** -0.5),
        "kidx_ln_g": gain(ks[8], IDX_DIM),
        "kidx_ln_b": nrm(ks[9], (DEPTH, IDX_DIM), 0.02),
        "attn_out_g": gain(ks[10], A_OUT_W),
        "dn_conv_w": nrm(ks[11], (DEPTH, DN_CONV, DN_QKV_W), DN_CONV ** -0.5),
        "dn_a_log": jnp.log(jax.random.uniform(ks[12], (DEPTH, DN_HEADS), jnp.float32, 1.0, 16.0)),
        "dn_dt_bias": dt + jnp.log(-jnp.expm1(-dt)),
        "dn_norm_g": gain(ks[13], DN_V_DIM),
        "w_out": nrm(ks[15], (DEPTH, MIX_W, D_MODEL), MIX_W ** -0.5 * BETA_INIT),
        "ln1_g": gain(ks[16], D_MODEL),
        "ln1_b": nrm(ks[17], (DEPTH, D_MODEL), 0.02),
        "ffn_w_in": nrm(ks[18], (DEPTH, D_MODEL, 2 * D_FF), D_MODEL ** -0.5),
        "ffn_conv_w": nrm(ks[19], (DEPTH, FFN_CONV, D_FF), FFN_CONV ** -0.5),
        "ffn_conv_b": nrm(ks[20], (DEPTH, D_FF), 0.01),
        "ffn_w_down": nrm(ks[21], (DEPTH, D_FF, D_MODEL), D_FF ** -0.5 * BETA_INIT),
        "ln2_g": gain(ks[22], D_MODEL),
        "ln2_b": nrm(ks[23], (DEPTH, D_MODEL), 0.02),
    }


def reference(x, w_in, q_norm_g, w_uq, w_qidx, kv_norm_g, w_uk, w_uv, kidx_ln_g, kidx_ln_b,
              attn_out_g, dn_conv_w, dn_a_log, dn_dt_bias, dn_norm_g, w_out, ln1_g, ln1_b,
              ffn_w_in, ffn_conv_w, ffn_conv_b, ffn_w_down, ln2_g, ln2_b):
    for l in range(DEPTH):
        proj = x @ w_in[l]
        c_q, c_kv, k_idx, w_idx, dn_qkv, dn_z, dn_b, dn_a = _split(proj, IN_SIZES)
        a_out = _dsa_attention(c_q, c_kv, k_idx, w_idx, q_norm_g[l], w_uq[l], w_qidx[l],
                               kv_norm_g[l], w_uk[l], w_uv[l], kidx_ln_g[l], kidx_ln_b[l])
        a_out = _rms_norm(a_out, attn_out_g[l])
        d_out = _gated_deltanet(dn_qkv, dn_z, dn_b, dn_a, dn_conv_w[l], dn_a_log[l],
                                dn_dt_bias[l], dn_norm_g[l])
        mix = jnp.concatenate([a_out, d_out], axis=-1) @ w_out[l]
        h = _layer_norm(ALPHA * x + mix, ln1_g[l], ln1_b[l])
        f = _conv_glu_ffn(h, ffn_w_in[l], ffn_conv_w[l], ffn_conv_b[l], ffn_w_down[l])
        x = _layer_norm(ALPHA * h + f, ln2_g[l], ln2_b[l])
    return x
```

```python
import functools

import jax
import jax.numpy as jnp
from jax import lax
from jax.experimental import pallas as pl
from jax.experimental.pallas import tpu as pltpu

F32 = jnp.float32
BF16 = jnp.bfloat16
I32 = jnp.int32

D_MODEL = 1024
A_HEADS = 8
A_QK_DIM = 64
A_V_DIM = 64
A_Q_RANK = 256
A_KV_RANK = 128
IDX_HEADS = 8
IDX_DIM = 64
TOPK_MAX = 256
DN_HEADS = 8
DN_K_DIM = 64
DN_V_DIM = 64
DN_CONV = 4
D_FF = 2816
FFN_CONV = 3
RMS_EPS = 1e-6
LN_EPS = 1e-5

A_OUT_W = A_HEADS * A_V_DIM
DN_KEY_W = DN_HEADS * DN_K_DIM
DN_VAL_W = DN_HEADS * DN_V_DIM
DN_QKV_W = 2 * DN_KEY_W + DN_VAL_W
IN_SIZES = (A_Q_RANK, A_KV_RANK, IDX_DIM, IDX_HEADS, DN_QKV_W, DN_VAL_W, DN_HEADS, DN_HEADS)

MISC_W = 128
MISC_WIDX = IDX_DIM
MISC_B = IDX_DIM + IDX_HEADS
MISC_A = MISC_B + DN_HEADS
PROJ_W = A_Q_RANK + A_KV_RANK + MISC_W + DN_QKV_W + DN_VAL_W

SUBLANES = 8
INT_MIN = -(2 ** 31)
NEG_KEY = INT_MIN + 0x00800000
VMEM_LIMIT = 56 * 1024 * 1024

GDN_CHUNK = 64
GDN_TILE = 256


def _dot(a, b, precision=None):
    return jnp.dot(a, b, preferred_element_type=F32, precision=precision)


def _dot_nt(a, b, precision=None):
    return lax.dot_general(a, b, (((1,), (1,)), ((), ())), preferred_element_type=F32, precision=precision)


def _dot_tn(a, b, precision=None):
    return lax.dot_general(a, b, (((0,), (0,)), ((), ())), preferred_element_type=F32, precision=precision)


def _sigmoid(x):
    return 1.0 / (1.0 + jnp.exp(-x))


def _silu(x):
    return x * _sigmoid(x)


def _layer_norm_rows(v, g, b):
    mu = jnp.mean(v, axis=-1, keepdims=True)
    d = v - mu
    var = jnp.mean(d * d, axis=-1, keepdims=True)
    return d * lax.rsqrt(var + LN_EPS) * g + b


def _rms_norm_rows(v, g):
    return v * lax.rsqrt(jnp.mean(v * v, axis=-1, keepdims=True) + RMS_EPS) * g


def _in_proj_kernel(x_ref, w_ref, qg_ref, wuq_ref, wuk_ref, wqi_ref, kvg_ref, lng_ref, lnb_ref,
                    qlat_ref, qidx_ref, ckv_ref, ckvt_ref, kidx_ref, misc_ref, misct_ref, qkv_ref, z_ref):
    xb = x_ref[...].astype(BF16)
    proj = _dot(xb, w_ref[...])
    o = 0
    c_q = proj[:, o:o + A_Q_RANK]; o += A_Q_RANK
    c_kv = proj[:, o:o + A_KV_RANK]; o += A_KV_RANK
    misc = proj[:, o:o + MISC_W]; o += MISC_W
    qkv_ref[...] = proj[:, o:o + DN_QKV_W]; o += DN_QKV_W
    z_ref[...] = proj[:, o:o + DN_VAL_W]

    cqn = _rms_norm_rows(c_q, qg_ref[...]).astype(BF16)
    for h in range(A_HEADS):
        q_h = _dot(cqn, wuq_ref[h]).astype(BF16)
        qlat_ref[h] = (_dot(q_h, wuk_ref[h]) * (A_QK_DIM ** -0.5)).astype(BF16)
        qidx_ref[h] = _dot(cqn, wqi_ref[h]).astype(BF16)
    ckv = _rms_norm_rows(c_kv, kvg_ref[...])
    ckv_ref[...] = ckv.astype(BF16)
    ckvt_ref[...] = ckv.T.astype(BF16)
    kidx_ref[...] = _layer_norm_rows(misc[:, :IDX_DIM], lng_ref[...], lnb_ref[...]).astype(BF16)
    misc_ref[...] = misc
    misct_ref[...] = misc.T


def _in_proj(x2, wc, qg, wuq, wuk, wqi, kvg, lng, lnb, *, tm):
    T = x2.shape[0]
    const = lambda *s: pl.BlockSpec(s, lambda i: (0,) * len(s))
    return pl.pallas_call(
        _in_proj_kernel,
        grid=(T // tm,),
        in_specs=[
            pl.BlockSpec((tm, D_MODEL), lambda i: (i, 0)),
            const(D_MODEL, PROJ_W), const(1, A_Q_RANK), const(A_HEADS, A_Q_RANK, A_QK_DIM),
            const(A_HEADS, A_QK_DIM, A_KV_RANK), const(IDX_HEADS, A_Q_RANK, IDX_DIM),
            const(1, A_KV_RANK), const(1, IDX_DIM), const(1, IDX_DIM),
        ],
        out_specs=[
            pl.BlockSpec((A_HEADS, tm, A_KV_RANK), lambda i: (0, i, 0)),
            pl.BlockSpec((IDX_HEADS, tm, IDX_DIM), lambda i: (0, i, 0)),
            pl.BlockSpec((tm, A_KV_RANK), lambda i: (i, 0)),
            pl.BlockSpec((A_KV_RANK, tm), lambda i: (0, i)),
            pl.BlockSpec((tm, IDX_DIM), lambda i: (i, 0)),
            pl.BlockSpec((tm, MISC_W), lambda i: (i, 0)),
            pl.BlockSpec((MISC_W, tm), lambda i: (0, i)),
            pl.BlockSpec((tm, DN_QKV_W), lambda i: (i, 0)),
            pl.BlockSpec((tm, DN_VAL_W), lambda i: (i, 0)),
        ],
        out_shape=[
            jax.ShapeDtypeStruct((A_HEADS, T, A_KV_RANK), BF16),
            jax.ShapeDtypeStruct((IDX_HEADS, T, IDX_DIM), BF16),
            jax.ShapeDtypeStruct((T, A_KV_RANK), BF16),
            jax.ShapeDtypeStruct((A_KV_RANK, T), BF16),
            jax.ShapeDtypeStruct((T, IDX_DIM), BF16),
            jax.ShapeDtypeStruct((T, MISC_W), F32),
            jax.ShapeDtypeStruct((MISC_W, T), F32),
            jax.ShapeDtypeStruct((T, DN_QKV_W), F32),
            jax.ShapeDtypeStruct((T, DN_VAL_W), F32),
        ],
        compiler_params=pltpu.CompilerParams(dimension_semantics=("parallel",), vmem_limit_bytes=VMEM_LIMIT),
        name="in_proj",
    )(x2, wc, qg, wuq, wuk, wqi, kvg, lng, lnb)


def _dsa_kernel(qidx_ref, qlat_ref, wt_ref, kidx_ref, ckv_ref, ckvt_ref, wuvt_ref, g_ref, out_ref,
                keys_ref, m_ref, l_ref, acc_ref, *, qb, kb, rb, topk, idx_bits):
    H = A_HEADS
    hq = H * qb
    j = pl.program_id(1)
    q0 = j * qb
    nchunks = ((j + 1) * qb + kb - 1) // kb
    nrblk = nchunks * (kb // rb)

    qi = qidx_ref[...].reshape(hq, IDX_DIM)
    ql = qlat_ref[...].reshape(hq, A_KV_RANK)
    w = wt_ref[...] * (IDX_HEADS ** -0.5 * IDX_DIM ** -0.5)

    def score_body(c, carry):
        r0 = pl.multiple_of(c * kb, kb)
        kc = kidx_ref[pl.ds(r0, kb), :]
        rel = _dot_nt(kc, qi)
        s = jnp.zeros((kb, qb), F32)
        for h in range(H):
            s = s + jnp.maximum(rel[:, h * qb:(h + 1) * qb], 0.0) * w[h:h + 1, :]
        bits = lax.bitcast_convert_type(s, I32)
        key = jnp.where(bits < 0, INT_MIN - bits, bits)
        kpos = r0 + lax.broadcasted_iota(I32, (kb, qb), 0)
        qpos = q0 + lax.broadcasted_iota(I32, (kb, qb), 1)
        keys_ref[pl.ds(r0, kb), :] = jnp.where(kpos <= qpos, key, NEG_KEY)
        return carry

    lax.fori_loop(0, nchunks, score_body, 0)

    def count_rows(pred):
        def body(r, acc):
            r0 = pl.multiple_of(r * rb, rb)
            k = keys_ref[pl.ds(r0, rb), :]
            m = jnp.where(pred(k, r0), 1, 0).astype(I32)
            return acc + m.reshape(rb // SUBLANES, SUBLANES, qb).sum(axis=0)
        acc = lax.fori_loop(0, nrblk, body, jnp.zeros((SUBLANES, qb), I32))
        return acc.sum(axis=0, keepdims=True)

    def bisect_body(p, carry):
        t_u, cnt_t = carry
        cand_u = t_u | jnp.left_shift(jnp.int32(1), 31 - p)
        cand_s = cand_u ^ INT_MIN
        cnt = count_rows(lambda k, r0: k >= cand_s)
        ok = cnt >= topk
        return jnp.where(ok, cand_u, t_u), jnp.where(ok, cnt, cnt_t)

    t_u, cnt_t = lax.fori_loop(
        0, 32, bisect_body, (jnp.zeros((1, qb), I32), jnp.full((1, qb), nrblk * rb, I32)))
    t = t_u ^ INT_MIN

    excess = jnp.where(t > NEG_KEY, cnt_t - topk, 0)

    @pl.when(jnp.max(excess) > 0)
    def _():
        keep = topk - count_rows(lambda k, r0: k >= t + 1)

        def idx_body(p, pos):
            cand = pos + jnp.left_shift(jnp.int32(1), idx_bits - 1 - p)
            c = count_rows(lambda k, r0: (k == t) & (
                r0 + lax.broadcasted_iota(I32, (rb, qb), 0) < cand))
            return jnp.where(c < keep, cand, pos)

        pos = lax.fori_loop(0, idx_bits, idx_body, jnp.zeros((1, qb), I32))

        def demote(r, carry):
            r0 = pl.multiple_of(r * rb, rb)
            k = keys_ref[pl.ds(r0, rb), :]
            row = r0 + lax.broadcasted_iota(I32, (rb, qb), 0)
            keys_ref[pl.ds(r0, rb), :] = jnp.where((k == t) & (row > pos), NEG_KEY, k)
            return carry

        lax.fori_loop(0, nrblk, demote, 0)

    thr = jnp.maximum(t, NEG_KEY + 1)

    m_ref[...] = jnp.full(m_ref.shape, -1e30, F32)
    l_ref[...] = jnp.zeros(l_ref.shape, F32)
    acc_ref[...] = jnp.zeros(acc_ref.shape, F32)

    def attn_body(c, carry):
        r0 = pl.multiple_of(c * kb, kb)
        kv = ckv_ref[pl.ds(r0, kb), :]
        lg = _dot_nt(kv, ql)
        sel = keys_ref[pl.ds(r0, kb), :] >= thr
        lg = jnp.concatenate(
            [jnp.where(sel, lg[:, h * qb:(h + 1) * qb], -jnp.inf) for h in range(H)], axis=1)
        m_old = m_ref[...]
        m_new = jnp.maximum(m_old, jnp.max(lg, axis=0, keepdims=True))
        p = jnp.exp(lg - m_new)
        alpha = jnp.exp(m_old - m_new)
        l_ref[...] = l_ref[...] * alpha + jnp.sum(p, axis=0, keepdims=True)
        acc_ref[...] = acc_ref[...] * alpha + _dot(ckvt_ref[:, pl.ds(r0, kb)], p.astype(BF16))
        m_ref[...] = m_new
        return carry

    lax.fori_loop(0, nchunks, attn_body, 0)

    o_lat = (acc_ref[...] / l_ref[...]).astype(BF16)
    out_t = jnp.concatenate(
        [_dot(wuvt_ref[h], o_lat[:, h * qb:(h + 1) * qb]) for h in range(H)], axis=0)
    out_ref[...] = _rms_norm_rows(out_t.T, g_ref[...])


def _dsa(qidx, qlat, misct, kidx, ckv, ckvt, wuvt, g, *, B, S, qb, kb, rb):
    T = B * S
    nq = S // qb
    topk = min(TOPK_MAX, S // 4)
    idx_bits = max(1, (S - 1).bit_length())
    kern = functools.partial(_dsa_kernel, qb=qb, kb=kb, rb=rb, topk=topk, idx_bits=idx_bits)
    hq = A_HEADS * qb
    return pl.pallas_call(
        kern,
        grid=(B, nq),
        in_specs=[
            pl.BlockSpec((IDX_HEADS, qb, IDX_DIM), lambda b, j: (0, b * nq + j, 0)),
            pl.BlockSpec((A_HEADS, qb, A_KV_RANK), lambda b, j: (0, b * nq + j, 0)),
            pl.BlockSpec((IDX_HEADS, qb), lambda b, j: (MISC_WIDX // IDX_HEADS, b * nq + j)),
            pl.BlockSpec((S, IDX_DIM), lambda b, j: (b, 0)),
            pl.BlockSpec((S, A_KV_RANK), lambda b, j: (b, 0)),
            pl.BlockSpec((A_KV_RANK, S), lambda b, j: (0, b)),
            pl.BlockSpec((A_HEADS, A_V_DIM, A_KV_RANK), lambda b, j: (0, 0, 0)),
            pl.BlockSpec((1, A_OUT_W), lambda b, j: (0, 0)),
        ],
        out_specs=pl.BlockSpec((qb, A_OUT_W), lambda b, j: (b * nq + j, 0)),
        out_shape=jax.ShapeDtypeStruct((T, A_OUT_W), F32),
        scratch_shapes=[
            pltpu.VMEM((S, qb), I32),
            pltpu.VMEM((1, hq), F32),
            pltpu.VMEM((1, hq), F32),
            pltpu.VMEM((A_KV_RANK, hq), F32),
        ],
        compiler_params=pltpu.CompilerParams(
            dimension_semantics=("parallel", "arbitrary"), vmem_limit_bytes=VMEM_LIMIT),
        name="dsa",
    )(qidx, qlat, misct, kidx, ckv, ckvt, wuvt, g)


def _gdn_kernel(qkv_ref, z_ref, misc_ref, at_ref, cw_ref, alog_c_ref, dtb_c_ref, alog_r_ref, dtb_r_ref,
                ng_ref, out_ref, xbuf_ref, state_ref, *, tb):
    C = GDN_CHUNK
    nck = tb // C
    H = DN_HEADS
    step = pl.program_id(1)

    @pl.when(step == 0)
    def _():
        xbuf_ref[0:SUBLANES, :] = jnp.zeros((SUBLANES, DN_QKV_W), F32)
        state_ref[...] = jnp.zeros(state_ref.shape, F32)

    xbuf_ref[SUBLANES:SUBLANES + tb, :] = qkv_ref[...]
    y = jnp.zeros((tb, DN_QKV_W), F32)
    for k in range(DN_CONV):
        off = SUBLANES - (DN_CONV - 1) + k
        y = y + xbuf_ref[off:off + tb, :] * cw_ref[k:k + 1, :]
    xbuf_ref[0:SUBLANES, :] = xbuf_ref[tb:tb + SUBLANES, :]
    y = _silu(y)

    def softplus(v):
        return jnp.maximum(v, 0.0) + jnp.log(1.0 + jnp.exp(-jnp.abs(v)))

    misc = misc_ref[...]
    beta_c = _sigmoid(misc[:, MISC_B:MISC_B + H])
    g_c = -jnp.exp(alog_r_ref[...]) * softplus(misc[:, MISC_A:MISC_A + H] + dtb_r_ref[...])
    g_r = -jnp.exp(alog_c_ref[...]) * softplus(at_ref[...] + dtb_c_ref[...])

    ri = lax.broadcasted_iota(I32, (tb, tb), 0)
    ci = lax.broadcasted_iota(I32, (tb, tb), 1)
    same = (ri // C) == (ci // C)
    tri_incl = same & (ci <= ri)
    tri_strict = same & (ci < ri)
    hi = lax.Precision.HIGHEST
    low_incl = jnp.where(tri_incl, 1.0, 0.0).astype(F32)
    up_incl = jnp.where(same & (ri <= ci), 1.0, 0.0).astype(F32)
    blk = jnp.where(same, 1.0, 0.0).astype(F32)
    gcum_c = _dot(low_incl, g_c, hi)
    gcum_r = _dot(g_r, up_incl, hi)
    glast_c = _dot(blk, g_c, hi)
    glast_r = _dot(g_r, blk, hi)
    eye = jnp.where(ri == ci, 1.0, 0.0).astype(F32)

    head_outs = []
    for h in range(H):
        q = y[:, h * DN_K_DIM:(h + 1) * DN_K_DIM]
        k = y[:, DN_KEY_W + h * DN_K_DIM:DN_KEY_W + (h + 1) * DN_K_DIM]
        v = y[:, 2 * DN_KEY_W + h * DN_V_DIM:2 * DN_KEY_W + (h + 1) * DN_V_DIM]
        q = q * lax.rsqrt(jnp.sum(q * q, axis=-1, keepdims=True) + RMS_EPS) * (DN_K_DIM ** -0.5)
        k = k * lax.rsqrt(jnp.sum(k * k, axis=-1, keepdims=True) + RMS_EPS)
        gc = gcum_c[:, h:h + 1]
        gr = gcum_r[h:h + 1, :]
        beta = beta_c[:, h:h + 1]
        decay = jnp.exp(jnp.where(tri_incl, gc - gr, -jnp.inf))
        kb_ = k * beta
        vb = v * beta
        kbf = k.astype(BF16)
        lower = jnp.where(tri_strict, _dot_nt(kb_.astype(BF16), kbf) * decay, 0.0)
        x_inv = eye - lower
        pw = lower
        for _ in range(5):
            pw = _dot(pw, pw, hi)
            x_inv = x_inv + _dot(x_inv, pw, hi)
        eg = jnp.exp(gc)
        u = _dot(x_inv, vb, hi)
        wk = _dot(x_inv, kb_ * eg, hi)
        attn = jnp.where(tri_incl, _dot_nt(q.astype(BF16), kbf) * decay, 0.0).astype(BF16)
        qd = (q * eg).astype(BF16)
        ke = (k * jnp.exp(glast_c[:, h:h + 1] - gc)).astype(BF16)
        wkb = wk.astype(BF16)

        st = state_ref[h]
        outs = []
        for n in range(nck):
            r = slice(n * C, (n + 1) * C)
            stb = st.astype(BF16)
            v_new = u[r] - _dot(wkb[r], stb)
            v_nb = v_new.astype(BF16)
            outs.append(_dot(qd[r], stb) + _dot(attn[r, r], v_nb))
            cd = jnp.exp(glast_r[h:h + 1, n * C:n * C + 1])
            st = st * cd + _dot_tn(ke[r], v_nb)
        state_ref[h] = st
        o = jnp.concatenate(outs, axis=0)
        o = o * lax.rsqrt(jnp.mean(o * o, axis=-1, keepdims=True) + RMS_EPS) * ng_ref[...]
        head_outs.append(o * _silu(z_ref[:, h * DN_V_DIM:(h + 1) * DN_V_DIM]))
    out_ref[...] = jnp.concatenate(head_outs, axis=1).astype(out_ref.dtype)


def _gdn(qkv, z, misc, misct, conv_w, a_log, dt_bias, norm_g, *, B, S, tb):
    T = B * S
    ns = S // tb
    H = DN_HEADS
    kern = functools.partial(_gdn_kernel, tb=tb)
    const = lambda *s: pl.BlockSpec(s, lambda b, i: (0,) * len(s))
    return pl.pallas_call(
        kern,
        grid=(B, ns),
        in_specs=[
            pl.BlockSpec((tb, DN_QKV_W), lambda b, i: (b * ns + i, 0)),
            pl.BlockSpec((tb, DN_VAL_W), lambda b, i: (b * ns + i, 0)),
            pl.BlockSpec((tb, MISC_W), lambda b, i: (b * ns + i, 0)),
            pl.BlockSpec((H, tb), lambda b, i: (MISC_A // H, b * ns + i)),
            const(DN_CONV, DN_QKV_W), const(H, 1), const(H, 1), const(1, H), const(1, H), const(1, DN_V_DIM),
        ],
        out_specs=pl.BlockSpec((tb, DN_VAL_W), lambda b, i: (b * ns + i, 0)),
        out_shape=jax.ShapeDtypeStruct((T, DN_VAL_W), BF16),
        scratch_shapes=[
            pltpu.VMEM((tb + SUBLANES, DN_QKV_W), F32),
            pltpu.VMEM((H, DN_K_DIM, DN_V_DIM), F32),
        ],
        compiler_params=pltpu.CompilerParams(
            dimension_semantics=("parallel", "arbitrary"), vmem_limit_bytes=VMEM_LIMIT),
        name="gdn",
    )(qkv, z, misc, misct, conv_w, a_log.reshape(H, 1), dt_bias.reshape(H, 1),
      a_log.reshape(1, H), dt_bias.reshape(1, H), norm_g.reshape(1, DN_V_DIM))


def _out_proj_kernel(a_ref, d_ref, x_ref, wa_ref, wd_ref, g_ref, b_ref, h_ref, *, alpha):
    mix = _dot(a_ref[...].astype(BF16), wa_ref[...]) + _dot(d_ref[...], wd_ref[...])
    h_ref[...] = _layer_norm_rows(alpha * x_ref[...] + mix, g_ref[...], b_ref[...])


def _out_proj(a, d, x2, wa, wd, g, b, *, alpha, tm):
    T = x2.shape[0]
    const = lambda *s: pl.BlockSpec(s, lambda i: (0,) * len(s))
    return pl.pallas_call(
        functools.partial(_out_proj_kernel, alpha=alpha),
        grid=(T // tm,),
        in_specs=[
            pl.BlockSpec((tm, A_OUT_W), lambda i: (i, 0)),
            pl.BlockSpec((tm, DN_VAL_W), lambda i: (i, 0)),
            pl.BlockSpec((tm, D_MODEL), lambda i: (i, 0)),
            const(A_OUT_W, D_MODEL), const(DN_VAL_W, D_MODEL), const(1, D_MODEL), const(1, D_MODEL),
        ],
        out_specs=pl.BlockSpec((tm, D_MODEL), lambda i: (i, 0)),
        out_shape=jax.ShapeDtypeStruct((T, D_MODEL), F32),
        compiler_params=pltpu.CompilerParams(dimension_semantics=("parallel",), vmem_limit_bytes=VMEM_LIMIT),
        name="out_proj",
    )(a, d, x2, wa, wd, g, b)


def _ffn_kernel(h_ref, halo_ref, wg_ref, wu_ref, cw_ref, cb_ref, wd_ref, g_ref, b_ref, out_ref,
                gbuf_ref, acc_ref, *, alpha, tm, tiles_per_seq):
    i = pl.program_id(0)
    f = pl.program_id(1)
    nf = pl.num_programs(1)
    hb = h_ref[...].astype(BF16)
    wg = wg_ref[...]
    gate = _dot(hb, wg)
    up = _dot(hb, wu_ref[...])
    halo = _dot(halo_ref[...].astype(BF16), wg)
    halo = jnp.where(i % tiles_per_seq == 0, 0.0, halo)
    gbuf_ref[0:SUBLANES, :] = halo
    gbuf_ref[SUBLANES:SUBLANES + tm, :] = gate
    conv = jnp.zeros(gate.shape, F32)
    for k in range(FFN_CONV):
        off = SUBLANES - (FFN_CONV - 1) + k
        conv = conv + gbuf_ref[off:off + tm, :] * cw_ref[k:k + 1, :]
    act = (_silu(conv + cb_ref[...]) * up).astype(BF16)
    part = _dot(act, wd_ref[...])

    @pl.when(f == 0)
    def _():
        acc_ref[...] = part

    @pl.when(f > 0)
    def _():
        acc_ref[...] += part

    @pl.when(f == nf - 1)
    def _():
        out_ref[...] = _layer_norm_rows(alpha * h_ref[...] + acc_ref[...], g_ref[...], b_ref[...])


def _ffn(h, wg, wu, cw, cb, wd, g, b, *, alpha, S, tm, tf):
    T = h.shape[0]
    nf = D_FF // tf
    hs = tm // SUBLANES
    kern = functools.partial(_ffn_kernel, alpha=alpha, tm=tm, tiles_per_seq=S // tm)
    return pl.pallas_call(
        kern,
        grid=(T // tm, nf),
        in_specs=[
            pl.BlockSpec((tm, D_MODEL), lambda i, f: (i, 0)),
            pl.BlockSpec((SUBLANES, D_MODEL), lambda i, f: (jnp.maximum(i * hs - 1, 0), 0)),
            pl.BlockSpec((D_MODEL, tf), lambda i, f: (0, f)),
            pl.BlockSpec((D_MODEL, tf), lambda i, f: (0, f)),
            pl.BlockSpec((FFN_CONV, tf), lambda i, f: (0, f)),
            pl.BlockSpec((1, tf), lambda i, f: (0, f)),
            pl.BlockSpec((tf, D_MODEL), lambda i, f: (f, 0)),
            pl.BlockSpec((1, D_MODEL), lambda i, f: (0, 0)),
            pl.BlockSpec((1, D_MODEL), lambda i, f: (0, 0)),
        ],
        out_specs=pl.BlockSpec((tm, D_MODEL), lambda i, f: (i, 0)),
        out_shape=jax.ShapeDtypeStruct((T, D_MODEL), F32),
        scratch_shapes=[pltpu.VMEM((tm + SUBLANES, tf), F32), pltpu.VMEM((tm, D_MODEL), F32)],
        compiler_params=pltpu.CompilerParams(
            dimension_semantics=("parallel", "arbitrary"), vmem_limit_bytes=VMEM_LIMIT),
        name="ffn",
    )(h, h, wg, wu, cw, cb, wd, g, b)


def _regroup_w_in(w):
    offs = [0]
    for s in IN_SIZES:
        offs.append(offs[-1] + s)
    cq, ckv, kidx, widx, qkv, z, b, a = (w[:, offs[i]:offs[i + 1]] for i in range(8))
    pad = jnp.zeros((w.shape[0], MISC_W - (IDX_DIM + IDX_HEADS + 2 * DN_HEADS)), w.dtype)
    return jnp.concatenate([cq, ckv, kidx, widx, b, a, pad, qkv, z], axis=1).astype(BF16)


def _layer(x2, p, *, B, S, alpha):
    row = lambda v: v.reshape(1, -1)
    wc = _regroup_w_in(p["w_in"])
    wuq = p["w_uq"].reshape(A_Q_RANK, A_HEADS, A_QK_DIM).transpose(1, 0, 2).astype(BF16)
    wqi = p["w_qidx"].reshape(A_Q_RANK, IDX_HEADS, IDX_DIM).transpose(1, 0, 2).astype(BF16)
    qlat, qidx, ckv, ckvt, kidx, misc, misct, qkv, z = _in_proj(
        x2, wc, row(p["q_norm_g"]), wuq, p["w_uk"].astype(BF16), wqi, row(p["kv_norm_g"]),
        row(p["kidx_ln_g"]), row(p["kidx_ln_b"]), tm=min(512, S))
    qb = min(256, S)
    a_out = _dsa(qidx, qlat, misct, kidx, ckv, ckvt, p["w_uv"].transpose(0, 2, 1).astype(BF16),
                 row(p["attn_out_g"]), B=B, S=S, qb=qb, kb=qb, rb=qb)
    d_out = _gdn(qkv, z, misc, misct, p["dn_conv_w"], p["dn_a_log"], p["dn_dt_bias"], p["dn_norm_g"],
                 B=B, S=S, tb=GDN_TILE)
    w_out = p["w_out"].astype(BF16)
    h = _out_proj(a_out, d_out, x2, w_out[:A_OUT_W], w_out[A_OUT_W:], row(p["ln1_g"]), row(p["ln1_b"]),
                  alpha=alpha, tm=min(512, S))
    w_ffn = p["ffn_w_in"].astype(BF16)
    return _ffn(h, w_ffn[:, :D_FF], w_ffn[:, D_FF:], p["ffn_conv_w"], row(p["ffn_conv_b"]),
                p["ffn_w_down"].astype(BF16), row(p["ln2_g"]), row(p["ln2_b"]),
                alpha=alpha, S=S, tm=min(512, S), tf=D_FF // 2)


_PARAM_NAMES = ("w_in", "q_norm_g", "w_uq", "w_qidx", "kv_norm_g", "w_uk", "w_uv", "kidx_ln_g", "kidx_ln_b",
                "attn_out_g", "dn_conv_w", "dn_a_log", "dn_dt_bias", "dn_norm_g", "w_out", "ln1_g", "ln1_b",
                "ffn_w_in", "ffn_conv_w", "ffn_conv_b", "ffn_w_down", "ln2_g", "ln2_b")


def kernel(x, w_in, q_norm_g, w_uq, w_qidx, kv_norm_g, w_uk, w_uv, kidx_ln_g, kidx_ln_b, attn_out_g, dn_conv_w, dn_a_log, dn_dt_bias, dn_norm_g, w_out, ln1_g, ln1_b, ffn_w_in, ffn_conv_w, ffn_conv_b, ffn_w_down, ln2_g, ln2_b):
    params = (w_in, q_norm_g, w_uq, w_qidx, kv_norm_g, w_uk, w_uv, kidx_ln_g, kidx_ln_b, attn_out_g, dn_conv_w,
              dn_a_log, dn_dt_bias, dn_norm_g, w_out, ln1_g, ln1_b, ffn_w_in, ffn_conv_w, ffn_conv_b, ffn_w_down,
              ln2_g, ln2_b)
    B, S, D = x.shape
    depth = w_in.shape[0]
    alpha = (2 * depth) ** 0.25
    x2 = x.reshape(B * S, D)
    for l in range(depth):
        x2 = _layer(x2, {n: v[l] for n, v in zip(_PARAM_NAMES, params)}, B=B, S=S, alpha=alpha)
    return x2.reshape(B, S, D)
```

```python
import functools

import jax
import jax.numpy as jnp
from jax import lax
from jax.experimental import pallas as pl
from jax.experimental.pallas import tpu as pltpu

F32 = jnp.float32
BF16 = jnp.bfloat16
I32 = jnp.int32
I16 = jnp.int16

D_MODEL = 1024
A_HEADS = 8
A_QK_DIM = 64
A_V_DIM = 64
A_Q_RANK = 256
A_KV_RANK = 128
IDX_HEADS = 8
IDX_DIM = 64
TOPK_MAX = 256
DN_HEADS = 8
DN_K_DIM = 64
DN_V_DIM = 64
DN_CONV = 4
D_FF = 2816
FFN_CONV = 3
RMS_EPS = 1e-6
LN_EPS = 1e-5

A_OUT_W = A_HEADS * A_V_DIM
DN_KEY_W = DN_HEADS * DN_K_DIM
DN_VAL_W = DN_HEADS * DN_V_DIM
DN_QKV_W = 2 * DN_KEY_W + DN_VAL_W
IN_SIZES = (A_Q_RANK, A_KV_RANK, IDX_DIM, IDX_HEADS, DN_QKV_W, DN_VAL_W, DN_HEADS, DN_HEADS)

MISC_W = 128
MISC_WIDX = IDX_DIM
MISC_B = IDX_DIM + IDX_HEADS
MISC_A = MISC_B + DN_HEADS
PROJ_W = A_Q_RANK + A_KV_RANK + MISC_W + DN_QKV_W + DN_VAL_W

LOG2E = 1.4426950408889634
SUBLANES = 8
PACK16 = 16
HALF16 = 2 ** 15
INT_MIN = -(2 ** 31)
NEG_KEY = INT_MIN + 0x00800000
VMEM_LIMIT = 56 * 1024 * 1024

GDN_CHUNK = 64
GDN_TILE = 256


def _dot(a, b, precision=None):
    return jnp.dot(a, b, preferred_element_type=F32, precision=precision)


def _dot_nt(a, b, precision=None):
    return lax.dot_general(a, b, (((1,), (1,)), ((), ())), preferred_element_type=F32, precision=precision)


def _dot_tn(a, b, precision=None):
    return lax.dot_general(a, b, (((0,), (0,)), ((), ())), preferred_element_type=F32, precision=precision)


def _sigmoid(x):
    return 1.0 / (1.0 + jnp.exp(-x))


def _silu(x):
    return x * _sigmoid(x)


def _layer_norm_rows(v, g, b):
    mu = jnp.mean(v, axis=-1, keepdims=True)
    d = v - mu
    var = jnp.mean(d * d, axis=-1, keepdims=True)
    return d * lax.rsqrt(var + LN_EPS) * g + b


def _rms_norm_rows(v, g):
    return v * lax.rsqrt(jnp.mean(v * v, axis=-1, keepdims=True) + RMS_EPS) * g


def _in_proj_kernel(x_ref, w_ref, qg_ref, wuq_ref, wuk_ref, wqi_ref, kvg_ref, lng_ref, lnb_ref,
                    qlat_ref, qidx_ref, ckv_ref, ckvt_ref, kidx_ref, misc_ref, misct_ref, qkv_ref, z_ref):
    xb = x_ref[...].astype(BF16)
    proj = _dot(xb, w_ref[...])
    o = 0
    c_q = proj[:, o:o + A_Q_RANK]; o += A_Q_RANK
    c_kv = proj[:, o:o + A_KV_RANK]; o += A_KV_RANK
    misc = proj[:, o:o + MISC_W]; o += MISC_W
    qkv_ref[...] = proj[:, o:o + DN_QKV_W]; o += DN_QKV_W
    z_ref[...] = proj[:, o:o + DN_VAL_W]

    cqn = _rms_norm_rows(c_q, qg_ref[...]).astype(BF16)
    for h in range(A_HEADS):
        q_h = _dot(cqn, wuq_ref[h]).astype(BF16)
        qlat_ref[h] = (_dot(q_h, wuk_ref[h]) * (A_QK_DIM ** -0.5 * LOG2E)).astype(BF16)
        qidx_ref[h] = _dot(cqn, wqi_ref[h]).astype(BF16)
    ckv = _rms_norm_rows(c_kv, kvg_ref[...])
    ckv_ref[...] = ckv.astype(BF16)
    ckvt_ref[...] = ckv.T.astype(BF16)
    kidx_ref[...] = _layer_norm_rows(misc[:, :IDX_DIM], lng_ref[...], lnb_ref[...]).astype(BF16)
    misc_ref[...] = misc
    misct_ref[...] = misc.T


def _in_proj(x2, wc, qg, wuq, wuk, wqi, kvg, lng, lnb, *, tm):
    T = x2.shape[0]
    const = lambda *s: pl.BlockSpec(s, lambda i: (0,) * len(s))
    return pl.pallas_call(
        _in_proj_kernel,
        grid=(T // tm,),
        in_specs=[
            pl.BlockSpec((tm, D_MODEL), lambda i: (i, 0)),
            const(D_MODEL, PROJ_W), const(1, A_Q_RANK), const(A_HEADS, A_Q_RANK, A_QK_DIM),
            const(A_HEADS, A_QK_DIM, A_KV_RANK), const(IDX_HEADS, A_Q_RANK, IDX_DIM),
            const(1, A_KV_RANK), const(1, IDX_DIM), const(1, IDX_DIM),
        ],
        out_specs=[
            pl.BlockSpec((A_HEADS, tm, A_KV_RANK), lambda i: (0, i, 0)),
            pl.BlockSpec((IDX_HEADS, tm, IDX_DIM), lambda i: (0, i, 0)),
            pl.BlockSpec((tm, A_KV_RANK), lambda i: (i, 0)),
            pl.BlockSpec((A_KV_RANK, tm), lambda i: (0, i)),
            pl.BlockSpec((tm, IDX_DIM), lambda i: (i, 0)),
            pl.BlockSpec((tm, MISC_W), lambda i: (i, 0)),
            pl.BlockSpec((MISC_W, tm), lambda i: (0, i)),
            pl.BlockSpec((tm, DN_QKV_W), lambda i: (i, 0)),
            pl.BlockSpec((tm, DN_VAL_W), lambda i: (i, 0)),
        ],
        out_shape=[
            jax.ShapeDtypeStruct((A_HEADS, T, A_KV_RANK), BF16),
            jax.ShapeDtypeStruct((IDX_HEADS, T, IDX_DIM), BF16),
            jax.ShapeDtypeStruct((T, A_KV_RANK), BF16),
            jax.ShapeDtypeStruct((A_KV_RANK, T), BF16),
            jax.ShapeDtypeStruct((T, IDX_DIM), BF16),
            jax.ShapeDtypeStruct((T, MISC_W), F32),
            jax.ShapeDtypeStruct((MISC_W, T), F32),
            jax.ShapeDtypeStruct((T, DN_QKV_W), F32),
            jax.ShapeDtypeStruct((T, DN_VAL_W), F32),
        ],
        compiler_params=pltpu.CompilerParams(dimension_semantics=("parallel",), vmem_limit_bytes=VMEM_LIMIT),
        name="in_proj",
    )(x2, wc, qg, wuq, wuk, wqi, kvg, lng, lnb)


def _dsa_kernel(qidx_ref, qlat_ref, wt_ref, kidx_ref, ckv_ref, ckvt_ref, wuvt_ref, g_ref, out_ref,
                keys_ref, hi_ref, lo_ref, acc_ref, stage_ref, bias_ref, *, qb, sb, kb, rb, slab, topk, idx_bits):
    H = A_HEADS
    j = pl.program_id(1)
    q0 = j * qb
    nkeys = (j + 1) * qb
    nrblk = (nkeys + rb - 1) // rb
    w = wt_ref[...] * (IDX_HEADS ** -0.5 * IDX_DIM ** -0.5)

    def score_body(c, carry):
        r0 = pl.multiple_of(c * sb, sb)
        kc = kidx_ref[pl.ds(r0, sb), :]
        for h in range(H):
            stage_ref[h] = _dot_nt(kc, qidx_ref[h])
        for r in range(0, sb, slab):
            s = jnp.zeros((slab, qb), F32)
            for h in range(H):
                s = s + jnp.maximum(stage_ref[h, r:r + slab, :], 0.0) * w[h:h + 1, :]
            bits = lax.bitcast_convert_type(s, I32)
            key = jnp.where(bits < 0, INT_MIN - bits, bits)
            kpos = r0 + r + lax.broadcasted_iota(I32, (slab, qb), 0)
            qpos = q0 + lax.broadcasted_iota(I32, (slab, qb), 1)
            _store_keys(r0 + r, slab, jnp.where(kpos <= qpos, key, NEG_KEY))
        return carry

    def _store_keys(r0, n, key):
        keys_ref[pl.ds(r0, n), :] = key
        hi_ref[pl.ds(r0, n), :] = jnp.right_shift(key, 16).astype(I16)
        lo_ref[pl.ds(r0, n), :] = ((key & 0xFFFF) - HALF16).astype(I16)

    lax.fori_loop(0, nkeys // sb, score_body, 0)

    @pl.when(nrblk * rb > nkeys)
    def _():
        _store_keys(pl.multiple_of(nkeys, qb), qb, jnp.full((qb, qb), NEG_KEY, I32))

    nacc = 4

    def count_rows(pred):
        def body(r, acc):
            r0 = pl.multiple_of(r * rb, rb)
            k = keys_ref[pl.ds(r0, rb), :]
            m = jnp.where(pred(k, r0), 1, 0).astype(I32)
            return acc + m.reshape(nacc, rb // (nacc * SUBLANES), SUBLANES, qb).sum(axis=1)
        acc = lax.fori_loop(0, nrblk, body, jnp.zeros((nacc, SUBLANES, qb), I32))
        return acc.sum(axis=0).sum(axis=0, keepdims=True)

    def count16(ref, cand):
        def body(r, accs):
            r0 = pl.multiple_of(r * rb, rb)
            m = jnp.where(ref[pl.ds(r0, rb), :] >= cand, jnp.int16(1), jnp.int16(0))
            accs = list(accs)
            for i in range(rb // PACK16):
                accs[i % nacc] = accs[i % nacc] + m[i * PACK16:(i + 1) * PACK16, :]
            return tuple(accs)
        accs = lax.fori_loop(0, nrblk, body, tuple(jnp.zeros((PACK16, qb), I16) for _ in range(nacc)))
        tot = accs[0].astype(I32)
        for a in accs[1:]:
            tot = tot + a.astype(I32)
        return tot.sum(axis=0, keepdims=True)

    def bisect16(ref, target, cnt_all):
        def body(p, carry):
            t_u, cnt_ok, cnt_rej = carry
            cand_u = t_u | jnp.left_shift(jnp.int32(1), 15 - p)
            cnt = count16(ref, (cand_u - HALF16).astype(I16))
            ok = cnt >= target
            return jnp.where(ok, cand_u, t_u), jnp.where(ok, cnt, cnt_ok), jnp.where(ok, cnt_rej, cnt)
        return lax.fori_loop(0, 16, body, (jnp.zeros((1, qb), I32), cnt_all, jnp.zeros((1, qb), I32)))

    hi_u, cnt_ge_hi, cnt_gt_hi = bisect16(hi_ref, topk, jnp.full((1, qb), nrblk * rb, I32))
    hi_t = (hi_u - HALF16).astype(I16)

    def mask_lo(r, carry):
        r0 = pl.multiple_of(r * rb, rb)
        lo_ref[pl.ds(r0, rb), :] = jnp.where(hi_ref[pl.ds(r0, rb), :] == hi_t, lo_ref[pl.ds(r0, rb), :],
                                             jnp.int16(-HALF16))
        return carry

    lax.fori_loop(0, nrblk, mask_lo, 0)
    lo_u, cnt_ge_lo, _ = bisect16(lo_ref, topk - cnt_gt_hi, cnt_ge_hi - cnt_gt_hi)
    t = jnp.left_shift(hi_u - HALF16, 16) + lo_u
    cnt_t = cnt_gt_hi + cnt_ge_lo

    excess = jnp.where(t > NEG_KEY, cnt_t - topk, 0)

    @pl.when(jnp.max(excess) > 0)
    def _():
        keep = topk - count_rows(lambda k, r0: k >= t + 1)

        def idx_body(p, pos):
            cand = pos + jnp.left_shift(jnp.int32(1), idx_bits - 1 - p)
            c = count_rows(lambda k, r0: (k == t) & (
                r0 + lax.broadcasted_iota(I32, (rb, qb), 0) < cand))
            return jnp.where(c < keep, cand, pos)

        pos = lax.fori_loop(0, idx_bits, idx_body, jnp.zeros((1, qb), I32))

        def demote(r, carry):
            r0 = pl.multiple_of(r * rb, rb)
            k = keys_ref[pl.ds(r0, rb), :]
            row = r0 + lax.broadcasted_iota(I32, (rb, qb), 0)
            keys_ref[pl.ds(r0, rb), :] = jnp.where((k == t) & (row > pos), NEG_KEY, k)
            return carry

        lax.fori_loop(0, nrblk, demote, 0)

    thr = jnp.maximum(t, NEG_KEY + 1)

    acc_ref[...] = jnp.zeros(acc_ref.shape, F32)

    def attn_body(c, carry):
        m_all, l_all = carry
        r0 = pl.multiple_of(c * kb, kb)
        kv = ckv_ref[pl.ds(r0, kb), :]
        kvt = ckvt_ref[:, pl.ds(r0, kb)]
        bias_ref[...] = jnp.where(keys_ref[pl.ds(r0, kb), :] >= thr, 0.0, -jnp.inf)
        m_rows, l_rows = [], []
        for h in range(H):
            lg = _dot_nt(kv, qlat_ref[h]) + bias_ref[...]
            stage_ref[h] = lg
            m_rows.append(jnp.maximum(m_all[h:h + 1, :], jnp.max(lg, axis=0, keepdims=True)))
        for h in range(H):
            p = jnp.exp2(stage_ref[h] - m_rows[h])
            alpha = jnp.exp2(m_all[h:h + 1, :] - m_rows[h])
            l_rows.append(l_all[h:h + 1, :] * alpha + jnp.sum(p, axis=0, keepdims=True))
            acc_ref[h] = acc_ref[h] * alpha + _dot(kvt, p.astype(BF16))
        return jnp.concatenate(m_rows, axis=0), jnp.concatenate(l_rows, axis=0)

    _, l_all = lax.fori_loop(0, nkeys // kb, attn_body,
                             (jnp.full((H, qb), -1e30, F32), jnp.zeros((H, qb), F32)))

    out_t = jnp.concatenate(
        [_dot(wuvt_ref[h], (acc_ref[h] / l_all[h:h + 1, :]).astype(BF16)) for h in range(H)], axis=0)
    out_ref[...] = _rms_norm_rows(out_t.T, g_ref[...])


def _dsa(qidx, qlat, misct, kidx, ckv, ckvt, wuvt, g, *, B, S):
    T = B * S
    qb = min(256, S)
    nq = S // qb
    sb = qb
    kb = qb
    rb = 2 * qb if nq % 2 == 0 else qb
    topk = min(TOPK_MAX, S // 4)
    idx_bits = max(1, (S - 1).bit_length())
    slab = min(64, sb)
    kern = functools.partial(_dsa_kernel, qb=qb, sb=sb, kb=kb, rb=rb, slab=slab, topk=topk, idx_bits=idx_bits)
    return pl.pallas_call(
        kern,
        grid=(B, nq),
        in_specs=[
            pl.BlockSpec((IDX_HEADS, qb, IDX_DIM), lambda b, j: (0, b * nq + j, 0)),
            pl.BlockSpec((A_HEADS, qb, A_KV_RANK), lambda b, j: (0, b * nq + j, 0)),
            pl.BlockSpec((IDX_HEADS, qb), lambda b, j: (MISC_WIDX // IDX_HEADS, b * nq + j)),
            pl.BlockSpec((S, IDX_DIM), lambda b, j: (b, 0)),
            pl.BlockSpec((S, A_KV_RANK), lambda b, j: (b, 0)),
            pl.BlockSpec((A_KV_RANK, S), lambda b, j: (0, b)),
            pl.BlockSpec((A_HEADS, A_V_DIM, A_KV_RANK), lambda b, j: (0, 0, 0)),
            pl.BlockSpec((1, A_OUT_W), lambda b, j: (0, 0)),
        ],
        out_specs=pl.BlockSpec((qb, A_OUT_W), lambda b, j: (b * nq + j, 0)),
        out_shape=jax.ShapeDtypeStruct((T, A_OUT_W), F32),
        scratch_shapes=[
            pltpu.VMEM((S, qb), I32),
            pltpu.VMEM((S, qb), I16),
            pltpu.VMEM((S, qb), I16),
            pltpu.VMEM((A_HEADS, A_KV_RANK, qb), F32),
            pltpu.VMEM((A_HEADS, kb, qb), F32),
            pltpu.VMEM((kb, qb), F32),
        ],
        compiler_params=pltpu.CompilerParams(
            dimension_semantics=("parallel", "arbitrary"), vmem_limit_bytes=VMEM_LIMIT),
        name="dsa",
    )(qidx, qlat, misct, kidx, ckv, ckvt, wuvt, g)


def _gdn_kernel(qkv_ref, z_ref, misc_ref, at_ref, cw_ref, alog_c_ref, dtb_c_ref, alog_r_ref, dtb_r_ref,
                ng_ref, out_ref, xbuf_ref, state_ref, *, tb):
    C = GDN_CHUNK
    nck = tb // C
    H = DN_HEADS
    step = pl.program_id(1)

    @pl.when(step == 0)
    def _():
        xbuf_ref[0:SUBLANES, :] = jnp.zeros((SUBLANES, DN_QKV_W), F32)
        state_ref[...] = jnp.zeros(state_ref.shape, F32)

    xbuf_ref[SUBLANES:SUBLANES + tb, :] = qkv_ref[...]
    y = jnp.zeros((tb, DN_QKV_W), F32)
    for k in range(DN_CONV):
        off = SUBLANES - (DN_CONV - 1) + k
        y = y + xbuf_ref[off:off + tb, :] * cw_ref[k:k + 1, :]
    xbuf_ref[0:SUBLANES, :] = xbuf_ref[tb:tb + SUBLANES, :]
    y = _silu(y)

    def softplus(v):
        return jnp.maximum(v, 0.0) + jnp.log(1.0 + jnp.exp(-jnp.abs(v)))

    misc = misc_ref[...]
    beta_c = _sigmoid(misc[:, MISC_B:MISC_B + H])
    g_c = -jnp.exp(alog_r_ref[...]) * softplus(misc[:, MISC_A:MISC_A + H] + dtb_r_ref[...])
    g_r = -jnp.exp(alog_c_ref[...]) * softplus(at_ref[...] + dtb_c_ref[...])

    ri = lax.broadcasted_iota(I32, (tb, tb), 0)
    ci = lax.broadcasted_iota(I32, (tb, tb), 1)
    same = (ri // C) == (ci // C)
    tri_incl = same & (ci <= ri)
    tri_strict = same & (ci < ri)
    hi = lax.Precision.HIGHEST
    low_incl = jnp.where(tri_incl, 1.0, 0.0).astype(F32)
    up_incl = jnp.where(same & (ri <= ci), 1.0, 0.0).astype(F32)
    blk = jnp.where(same, 1.0, 0.0).astype(F32)
    gcum_c = _dot(low_incl, g_c, hi)
    gcum_r = _dot(g_r, up_incl, hi)
    glast_c = _dot(blk, g_c, hi)
    glast_r = _dot(g_r, blk, hi)
    eye = jnp.where(ri == ci, 1.0, 0.0).astype(F32)
    inv_masks = [(ri // b) == (ci // b) for b in (8, 16, 32)] + [same]

    hs = range(H)
    rs = lambda v: lax.rsqrt(jnp.sum(v * v, axis=-1, keepdims=True) + RMS_EPS)
    qs = [y[:, h * DN_K_DIM:(h + 1) * DN_K_DIM] for h in hs]
    ks = [y[:, DN_KEY_W + h * DN_K_DIM:DN_KEY_W + (h + 1) * DN_K_DIM] for h in hs]
    vs = [y[:, 2 * DN_KEY_W + h * DN_V_DIM:2 * DN_KEY_W + (h + 1) * DN_V_DIM] for h in hs]
    qs = [q * rs(q) * (DN_K_DIM ** -0.5) for q in qs]
    ks = [k * rs(k) for k in ks]
    gcs = [gcum_c[:, h:h + 1] for h in hs]
    betas = [beta_c[:, h:h + 1] for h in hs]
    kbs = [ks[h] * betas[h] for h in hs]
    kbf = [k.astype(BF16) for k in ks]
    kk = [_dot_nt(kbs[h].astype(BF16), kbf[h]) for h in hs]
    qk = [_dot_nt(qs[h].astype(BF16), kbf[h]) for h in hs]
    decay = [jnp.exp(jnp.where(tri_incl, gcs[h] - gcum_r[h:h + 1, :], -jnp.inf)) for h in hs]
    lower = [jnp.where(tri_strict, kk[h] * decay[h], 0.0) for h in hs]
    attn = [jnp.where(tri_incl, qk[h] * decay[h], 0.0).astype(BF16) for h in hs]
    pw = [jnp.where(inv_masks[0], lower[h], 0.0) for h in hs]
    x_inv = [eye - pw[h] for h in hs]
    for _ in range(2):
        pwb = [p.astype(BF16) for p in pw]
        pw = [_dot(p, p) for p in pwb]
        x_inv = [x_inv[h] + _dot(x_inv[h].astype(BF16), pw[h].astype(BF16)) for h in hs]
    for lvl in range(1, len(inv_masks)):
        lvl_mask = inv_masks[lvl] & ~inv_masks[lvl - 1]
        xb = [x.astype(BF16) for x in x_inv]
        t1 = [_dot(xb[h], jnp.where(lvl_mask, lower[h], 0.0).astype(BF16)).astype(BF16) for h in hs]
        x_inv = [x_inv[h] - _dot(t1[h], xb[h]) for h in hs]
    egs = [jnp.exp(g) for g in gcs]
    uw = [_dot(x_inv[h].astype(BF16),
               jnp.concatenate([vs[h] * betas[h], kbs[h] * egs[h]], axis=1).astype(BF16)) for h in hs]
    us = [m[:, :DN_V_DIM] for m in uw]
    wkb = [m[:, DN_V_DIM:].astype(BF16) for m in uw]
    qd = [(qs[h] * egs[h]).astype(BF16) for h in hs]
    ke = [(ks[h] * jnp.exp(glast_c[:, h:h + 1] - gcs[h])).astype(BF16) for h in hs]

    st = [state_ref[h] for h in hs]
    outs = [[] for _ in hs]
    for n in range(nck):
        r = slice(n * C, (n + 1) * C)
        stb = [s.astype(BF16) for s in st]
        v_nb = [(us[h][r] - _dot(wkb[h][r], stb[h])).astype(BF16) for h in hs]
        for h in hs:
            outs[h].append(_dot(qd[h][r], stb[h]) + _dot(attn[h][r, r], v_nb[h]))
        cd = [jnp.exp(glast_r[h:h + 1, n * C:n * C + 1]) for h in hs]
        st = [st[h] * cd[h] + _dot_tn(ke[h][r], v_nb[h]) for h in hs]
    head_outs = []
    for h in hs:
        state_ref[h] = st[h]
        o = jnp.concatenate(outs[h], axis=0)
        o = o * lax.rsqrt(jnp.mean(o * o, axis=-1, keepdims=True) + RMS_EPS) * ng_ref[...]
        head_outs.append(o * _silu(z_ref[:, h * DN_V_DIM:(h + 1) * DN_V_DIM]))
    out_ref[...] = jnp.concatenate(head_outs, axis=1).astype(out_ref.dtype)


def _gdn(qkv, z, misc, misct, conv_w, a_log, dt_bias, norm_g, *, B, S):
    T = B * S
    tb = GDN_TILE
    ns = S // tb
    H = DN_HEADS
    kern = functools.partial(_gdn_kernel, tb=tb)
    const = lambda *s: pl.BlockSpec(s, lambda b, i: (0,) * len(s))
    return pl.pallas_call(
        kern,
        grid=(B, ns),
        in_specs=[
            pl.BlockSpec((tb, DN_QKV_W), lambda b, i: (b * ns + i, 0)),
            pl.BlockSpec((tb, DN_VAL_W), lambda b, i: (b * ns + i, 0)),
            pl.BlockSpec((tb, MISC_W), lambda b, i: (b * ns + i, 0)),
            pl.BlockSpec((H, tb), lambda b, i: (MISC_A // H, b * ns + i)),
            const(DN_CONV, DN_QKV_W), const(H, 1), const(H, 1), const(1, H), const(1, H), const(1, DN_V_DIM),
        ],
        out_specs=pl.BlockSpec((tb, DN_VAL_W), lambda b, i: (b * ns + i, 0)),
        out_shape=jax.ShapeDtypeStruct((T, DN_VAL_W), BF16),
        scratch_shapes=[
            pltpu.VMEM((tb + SUBLANES, DN_QKV_W), F32),
            pltpu.VMEM((H, DN_K_DIM, DN_V_DIM), F32),
        ],
        compiler_params=pltpu.CompilerParams(
            dimension_semantics=("parallel", "arbitrary"), vmem_limit_bytes=VMEM_LIMIT),
        name="gdn",
    )(qkv, z, misc, misct, conv_w, a_log.reshape(H, 1), dt_bias.reshape(H, 1),
      a_log.reshape(1, H), dt_bias.reshape(1, H), norm_g.reshape(1, DN_V_DIM))


def _out_proj_kernel(a_ref, d_ref, x_ref, wa_ref, wd_ref, g_ref, b_ref, h_ref, *, alpha):
    mix = _dot(a_ref[...].astype(BF16), wa_ref[...]) + _dot(d_ref[...], wd_ref[...])
    h_ref[...] = _layer_norm_rows(alpha * x_ref[...] + mix, g_ref[...], b_ref[...])


def _out_proj(a, d, x2, wa, wd, g, b, *, alpha, tm):
    T = x2.shape[0]
    const = lambda *s: pl.BlockSpec(s, lambda i: (0,) * len(s))
    return pl.pallas_call(
        functools.partial(_out_proj_kernel, alpha=alpha),
        grid=(T // tm,),
        in_specs=[
            pl.BlockSpec((tm, A_OUT_W), lambda i: (i, 0)),
            pl.BlockSpec((tm, DN_VAL_W), lambda i: (i, 0)),
            pl.BlockSpec((tm, D_MODEL), lambda i: (i, 0)),
            const(A_OUT_W, D_MODEL), const(DN_VAL_W, D_MODEL), const(1, D_MODEL), const(1, D_MODEL),
        ],
        out_specs=pl.BlockSpec((tm, D_MODEL), lambda i: (i, 0)),
        out_shape=jax.ShapeDtypeStruct((T, D_MODEL), F32),
        compiler_params=pltpu.CompilerParams(dimension_semantics=("parallel",), vmem_limit_bytes=VMEM_LIMIT),
        name="out_proj",
    )(a, d, x2, wa, wd, g, b)


def _ffn_kernel(h_ref, halo_ref, wg_ref, wu_ref, cw_ref, cb_ref, wd_ref, g_ref, b_ref, out_ref,
                gbuf_ref, acc_ref, *, alpha, tm, tiles_per_seq):
    i = pl.program_id(0)
    f = pl.program_id(1)
    nf = pl.num_programs(1)
    hb = h_ref[...].astype(BF16)
    wg = wg_ref[...]
    gate = _dot(hb, wg)
    up = _dot(hb, wu_ref[...])
    halo = _dot(halo_ref[...].astype(BF16), wg)
    halo = jnp.where(i % tiles_per_seq == 0, 0.0, halo)
    gbuf_ref[0:SUBLANES, :] = halo
    gbuf_ref[SUBLANES:SUBLANES + tm, :] = gate
    conv = jnp.zeros(gate.shape, F32)
    for k in range(FFN_CONV):
        off = SUBLANES - (FFN_CONV - 1) + k
        conv = conv + gbuf_ref[off:off + tm, :] * cw_ref[k:k + 1, :]
    act = (_silu(conv + cb_ref[...]) * up).astype(BF16)
    part = _dot(act, wd_ref[...])

    @pl.when(f == 0)
    def _():
        acc_ref[...] = part

    @pl.when(f > 0)
    def _():
        acc_ref[...] += part

    @pl.when(f == nf - 1)
    def _():
        out_ref[...] = _layer_norm_rows(alpha * h_ref[...] + acc_ref[...], g_ref[...], b_ref[...])


def _ffn(h, wg, wu, cw, cb, wd, g, b, *, alpha, S, tm, tf):
    T = h.shape[0]
    nf = D_FF // tf
    hs = tm // SUBLANES
    kern = functools.partial(_ffn_kernel, alpha=alpha, tm=tm, tiles_per_seq=S // tm)
    return pl.pallas_call(
        kern,
        grid=(T // tm, nf),
        in_specs=[
            pl.BlockSpec((tm, D_MODEL), lambda i, f: (i, 0)),
            pl.BlockSpec((SUBLANES, D_MODEL), lambda i, f: (jnp.maximum(i * hs - 1, 0), 0)),
            pl.BlockSpec((D_MODEL, tf), lambda i, f: (0, f)),
            pl.BlockSpec((D_MODEL, tf), lambda i, f: (0, f)),
            pl.BlockSpec((FFN_CONV, tf), lambda i, f: (0, f)),
            pl.BlockSpec((1, tf), lambda i, f: (0, f)),
            pl.BlockSpec((tf, D_MODEL), lambda i, f: (f, 0)),
            pl.BlockSpec((1, D_MODEL), lambda i, f: (0, 0)),
            pl.BlockSpec((1, D_MODEL), lambda i, f: (0, 0)),
        ],
        out_specs=pl.BlockSpec((tm, D_MODEL), lambda i, f: (i, 0)),
        out_shape=jax.ShapeDtypeStruct((T, D_MODEL), F32),
        scratch_shapes=[pltpu.VMEM((tm + SUBLANES, tf), F32), pltpu.VMEM((tm, D_MODEL), F32)],
        compiler_params=pltpu.CompilerParams(
            dimension_semantics=("parallel", "arbitrary"), vmem_limit_bytes=VMEM_LIMIT),
        name="ffn",
    )(h, h, wg, wu, cw, cb, wd, g, b)


def _regroup_w_in(w):
    offs = [0]
    for s in IN_SIZES:
        offs.append(offs[-1] + s)
    cq, ckv, kidx, widx, qkv, z, b, a = (w[:, offs[i]:offs[i + 1]] for i in range(8))
    pad = jnp.zeros((w.shape[0], MISC_W - (IDX_DIM + IDX_HEADS + 2 * DN_HEADS)), w.dtype)
    return jnp.concatenate([cq, ckv, kidx, widx, b, a, pad, qkv, z], axis=1).astype(BF16)


def _layer(x2, p, *, B, S, alpha):
    row = lambda v: v.reshape(1, -1)
    wc = _regroup_w_in(p["w_in"])
    wuq = p["w_uq"].reshape(A_Q_RANK, A_HEADS, A_QK_DIM).transpose(1, 0, 2).astype(BF16)
    wqi = p["w_qidx"].reshape(A_Q_RANK, IDX_HEADS, IDX_DIM).transpose(1, 0, 2).astype(BF16)
    qlat, qidx, ckv, ckvt, kidx, misc, misct, qkv, z = _in_proj(
        x2, wc, row(p["q_norm_g"]), wuq, p["w_uk"].astype(BF16), wqi, row(p["kv_norm_g"]),
        row(p["kidx_ln_g"]), row(p["kidx_ln_b"]), tm=min(512, S))
    a_out = _dsa(qidx, qlat, misct, kidx, ckv, ckvt, p["w_uv"].transpose(0, 2, 1).astype(BF16),
                 row(p["attn_out_g"]), B=B, S=S)
    d_out = _gdn(qkv, z, misc, misct, p["dn_conv_w"], p["dn_a_log"], p["dn_dt_bias"], p["dn_norm_g"],
                 B=B, S=S)
    w_out = p["w_out"].astype(BF16)
    h = _out_proj(a_out, d_out, x2, w_out[:A_OUT_W], w_out[A_OUT_W:], row(p["ln1_g"]), row(p["ln1_b"]),
                  alpha=alpha, tm=min(512, S))
    w_ffn = p["ffn_w_in"].astype(BF16)
    return _ffn(h, w_ffn[:, :D_FF], w_ffn[:, D_FF:], p["ffn_conv_w"], row(p["ffn_conv_b"]),
                p["ffn_w_down"].astype(BF16), row(p["ln2_g"]), row(p["ln2_b"]),
                alpha=alpha, S=S, tm=min(512, S), tf=D_FF // 2)


_PARAM_NAMES = ("w_in", "q_norm_g", "w_uq", "w_qidx", "kv_norm_g", "w_uk", "w_uv", "kidx_ln_g", "kidx_ln_b",
                "attn_out_g", "dn_conv_w", "dn_a_log", "dn_dt_bias", "dn_norm_g", "w_out", "ln1_g", "ln1_b",
                "ffn_w_in", "ffn_conv_w", "ffn_conv_b", "ffn_w_down", "ln2_g", "ln2_b")


def kernel(x, w_in, q_norm_g, w_uq, w_qidx, kv_norm_g, w_uk, w_uv, kidx_ln_g, kidx_ln_b, attn_out_g, dn_conv_w, dn_a_log, dn_dt_bias, dn_norm_g, w_out, ln1_g, ln1_b, ffn_w_in, ffn_conv_w, ffn_conv_b, ffn_w_down, ln2_g, ln2_b):
    params = (w_in, q_norm_g, w_uq, w_qidx, kv_norm_g, w_uk, w_uv, kidx_ln_g, kidx_ln_b, attn_out_g, dn_conv_w,
              dn_a_log, dn_dt_bias, dn_norm_g, w_out, ln1_g, ln1_b, ffn_w_in, ffn_conv_w, ffn_conv_b, ffn_w_down,
              ln2_g, ln2_b)
    B, S, D = x.shape
    depth = w_in.shape[0]
    alpha = (2 * depth) ** 0.25
    x2 = x.reshape(B * S, D)
    for l in range(depth):
        x2 = _layer(x2, {n: v[l] for n, v in zip(_PARAM_NAMES, params)}, B=B, S=S, alpha=alpha)
    return x2.reshape(B, S, D)
```

```python
import functools

import jax
import jax.numpy as jnp
from jax import lax
from jax.experimental import pallas as pl
from jax.experimental.pallas import tpu as pltpu

F32 = jnp.float32
BF16 = jnp.bfloat16
I32 = jnp.int32
I16 = jnp.int16

D_MODEL = 1024
A_HEADS = 8
A_QK_DIM = 64
A_V_DIM = 64
A_Q_RANK = 256
A_KV_RANK = 128
IDX_HEADS = 8
IDX_DIM = 64
TOPK_MAX = 256
DN_HEADS = 8
DN_K_DIM = 64
DN_V_DIM = 64
DN_CONV = 4
D_FF = 2816
FFN_CONV = 3
RMS_EPS = 1e-6
LN_EPS = 1e-5

A_OUT_W = A_HEADS * A_V_DIM
DN_KEY_W = DN_HEADS * DN_K_DIM
DN_VAL_W = DN_HEADS * DN_V_DIM
DN_QKV_W = 2 * DN_KEY_W + DN_VAL_W
IN_SIZES = (A_Q_RANK, A_KV_RANK, IDX_DIM, IDX_HEADS, DN_QKV_W, DN_VAL_W, DN_HEADS, DN_HEADS)

MISC_W = 128
MISC_WIDX = IDX_DIM
MISC_B = IDX_DIM + IDX_HEADS
MISC_A = MISC_B + DN_HEADS
PROJ_W = A_Q_RANK + A_KV_RANK + MISC_W + DN_QKV_W + DN_VAL_W

LOG2E = 1.4426950408889634
SUBLANES = 8
PACK16 = 16
KVT_ROWS = A_KV_RANK + PACK16
HALF16 = 2 ** 15
INT_MIN = -(2 ** 31)
NEG_KEY = INT_MIN + 0x00800000
VMEM_LIMIT = 56 * 1024 * 1024

GDN_CHUNK = 64
GDN_TILE = 256


def _dot(a, b, precision=None):
    return jnp.dot(a, b, preferred_element_type=F32, precision=precision)


def _dot_nt(a, b, precision=None):
    return lax.dot_general(a, b, (((1,), (1,)), ((), ())), preferred_element_type=F32, precision=precision)


def _dot_tn(a, b, precision=None):
    return lax.dot_general(a, b, (((0,), (0,)), ((), ())), preferred_element_type=F32, precision=precision)


def _sigmoid(x):
    return 1.0 / (1.0 + jnp.exp(-x))


def _silu(x):
    return x * _sigmoid(x)


def _layer_norm_rows(v, g, b):
    mu = jnp.mean(v, axis=-1, keepdims=True)
    d = v - mu
    var = jnp.mean(d * d, axis=-1, keepdims=True)
    return d * lax.rsqrt(var + LN_EPS) * g + b


def _rms_norm_rows(v, g):
    return v * lax.rsqrt(jnp.mean(v * v, axis=-1, keepdims=True) + RMS_EPS) * g


def _in_proj_kernel(x_ref, w_ref, qg_ref, wuq_ref, wuk_ref, wqi_ref, kvg_ref, lng_ref, lnb_ref,
                    qlat_ref, qidx_ref, ckv_ref, ckvt_ref, kidx_ref, misc_ref, misct_ref, qkv_ref, z_ref):
    xb = x_ref[...].astype(BF16)
    proj = _dot(xb, w_ref[...])
    o = 0
    c_q = proj[:, o:o + A_Q_RANK]; o += A_Q_RANK
    c_kv = proj[:, o:o + A_KV_RANK]; o += A_KV_RANK
    misc = proj[:, o:o + MISC_W]; o += MISC_W
    qkv_ref[...] = proj[:, o:o + DN_QKV_W]; o += DN_QKV_W
    z_ref[...] = proj[:, o:o + DN_VAL_W]

    cqn = _rms_norm_rows(c_q, qg_ref[...]).astype(BF16)
    for h in range(A_HEADS):
        q_h = _dot(cqn, wuq_ref[h]).astype(BF16)
        qlat_ref[h] = (_dot(q_h, wuk_ref[h]) * (A_QK_DIM ** -0.5 * LOG2E)).astype(BF16)
        qidx_ref[h] = _dot(cqn, wqi_ref[h]).astype(BF16)
    ckv = _rms_norm_rows(c_kv, kvg_ref[...])
    ckv_ref[...] = ckv.astype(BF16)
    ckvt_ref[0:A_KV_RANK, :] = ckv.T.astype(BF16)
    ckvt_ref[A_KV_RANK:KVT_ROWS, :] = jnp.ones((KVT_ROWS - A_KV_RANK, ckv.shape[0]), BF16)
    kidx_ref[...] = _layer_norm_rows(misc[:, :IDX_DIM], lng_ref[...], lnb_ref[...]).astype(BF16)
    misc_ref[...] = misc
    misct_ref[...] = misc.T


def _in_proj(x2, wc, qg, wuq, wuk, wqi, kvg, lng, lnb, *, tm):
    T = x2.shape[0]
    const = lambda *s: pl.BlockSpec(s, lambda i: (0,) * len(s))
    return pl.pallas_call(
        _in_proj_kernel,
        grid=(T // tm,),
        in_specs=[
            pl.BlockSpec((tm, D_MODEL), lambda i: (i, 0)),
            const(D_MODEL, PROJ_W), const(1, A_Q_RANK), const(A_HEADS, A_Q_RANK, A_QK_DIM),
            const(A_HEADS, A_QK_DIM, A_KV_RANK), const(IDX_HEADS, A_Q_RANK, IDX_DIM),
            const(1, A_KV_RANK), const(1, IDX_DIM), const(1, IDX_DIM),
        ],
        out_specs=[
            pl.BlockSpec((A_HEADS, tm, A_KV_RANK), lambda i: (0, i, 0)),
            pl.BlockSpec((IDX_HEADS, tm, IDX_DIM), lambda i: (0, i, 0)),
            pl.BlockSpec((tm, A_KV_RANK), lambda i: (i, 0)),
            pl.BlockSpec((KVT_ROWS, tm), lambda i: (0, i)),
            pl.BlockSpec((tm, IDX_DIM), lambda i: (i, 0)),
            pl.BlockSpec((tm, MISC_W), lambda i: (i, 0)),
            pl.BlockSpec((MISC_W, tm), lambda i: (0, i)),
            pl.BlockSpec((tm, DN_QKV_W), lambda i: (i, 0)),
            pl.BlockSpec((tm, DN_VAL_W), lambda i: (i, 0)),
        ],
        out_shape=[
            jax.ShapeDtypeStruct((A_HEADS, T, A_KV_RANK), BF16),
            jax.ShapeDtypeStruct((IDX_HEADS, T, IDX_DIM), BF16),
            jax.ShapeDtypeStruct((T, A_KV_RANK), BF16),
            jax.ShapeDtypeStruct((KVT_ROWS, T), BF16),
            jax.ShapeDtypeStruct((T, IDX_DIM), BF16),
            jax.ShapeDtypeStruct((T, MISC_W), F32),
            jax.ShapeDtypeStruct((MISC_W, T), F32),
            jax.ShapeDtypeStruct((T, DN_QKV_W), F32),
            jax.ShapeDtypeStruct((T, DN_VAL_W), F32),
        ],
        compiler_params=pltpu.CompilerParams(dimension_semantics=("parallel",), vmem_limit_bytes=VMEM_LIMIT),
        name="in_proj",
    )(x2, wc, qg, wuq, wuk, wqi, kvg, lng, lnb)


def _dsa_kernel(qidx_ref, qlat_ref, wt_ref, kidx_ref, ckv_ref, ckvt_ref, wuvt_ref, g_ref, out_ref,
                keys_ref, hi_ref, lo_ref, acc_ref, stage_ref, bias_ref, *, qb, kb, rb, slab, topk, idx_bits):
    H = A_HEADS
    j = pl.program_id(1)
    q0 = j * qb
    nkeys = (j + 1) * qb
    nrblk = (nkeys + rb - 1) // rb
    ups = rb // kb
    nchunk = nrblk * ups
    w = wt_ref[...] * (IDX_HEADS ** -0.5 * IDX_DIM ** -0.5)
    row_minus_lane = (lax.broadcasted_iota(I32, (slab, qb), 0) - lax.broadcasted_iota(I32, (slab, qb), 1))

    def skewed(nparts, consume, produce, carry):
        def chunk(c, k, carry):
            nxt = jnp.minimum(c + 1, nchunk - 1)
            if ups > 1:
                for i in range(nparts):
                    carry = produce(nxt, (k + 1) % ups, i, consume(c, k, i, carry))
                return carry
            for i in range(nparts):
                carry = consume(c, k, i, carry)
            for i in range(nparts):
                carry = produce(nxt, k, i, carry)
            return carry

        def step(i, carry):
            for k in range(ups):
                carry = chunk(i * ups + k, k, carry)
            return carry

        for i in range(nparts):
            carry = produce(0, 0, i, carry)
        return lax.fori_loop(0, nrblk, step, carry)

    nslab = kb // slab
    hps = H // nslab

    def score_matmuls(c, slot, i, carry):
        kc = kidx_ref[pl.ds(pl.multiple_of(c * kb, kb), kb), :]
        for h in range(i * hps, (i + 1) * hps):
            stage_ref[slot, h] = _dot_nt(kc, qidx_ref[h])
        return carry

    def score_keys(c, slot, i, carry):
        r0 = pl.multiple_of(c * kb, kb) + i * slab
        s = jnp.zeros((slab, qb), F32)
        for h in range(H):
            s = s + jnp.maximum(stage_ref[slot, h, i * slab:(i + 1) * slab, :], 0.0) * w[h:h + 1, :]
        bits = lax.bitcast_convert_type(s, I32)
        key = jnp.where(bits < 0, INT_MIN - bits, bits)
        key = jnp.where(row_minus_lane <= q0 - r0, key, NEG_KEY)
        keys_ref[pl.ds(r0, slab), :] = key
        hi_ref[pl.ds(r0, slab), :] = jnp.right_shift(key, 16).astype(I16)
        lo_ref[pl.ds(r0, slab), :] = ((key & 0xFFFF) - HALF16).astype(I16)
        return carry

    skewed(nslab, score_keys, score_matmuls, 0)

    nacc = 4

    def count_rows(pred):
        def body(r, acc):
            r0 = pl.multiple_of(r * rb, rb)
            k = keys_ref[pl.ds(r0, rb), :]
            m = jnp.where(pred(k, r0), 1, 0).astype(I32)
            return acc + m.reshape(nacc, rb // (nacc * SUBLANES), SUBLANES, qb).sum(axis=1)
        acc = lax.fori_loop(0, nrblk, body, jnp.zeros((nacc, SUBLANES, qb), I32))
        return acc.sum(axis=0).sum(axis=0, keepdims=True)

    def count16(ref, cand):
        def body(r, accs):
            r0 = pl.multiple_of(r * rb, rb)
            m = jnp.where(ref[pl.ds(r0, rb), :] >= cand, jnp.int16(1), jnp.int16(0))
            accs = list(accs)
            for i in range(rb // PACK16):
                accs[i % nacc] = accs[i % nacc] + m[i * PACK16:(i + 1) * PACK16, :]
            return tuple(accs)
        accs = lax.fori_loop(0, nrblk, body, tuple(jnp.zeros((PACK16, qb), I16) for _ in range(nacc)))
        tot = accs[0].astype(I32)
        for a in accs[1:]:
            tot = tot + a.astype(I32)
        return tot.sum(axis=0, keepdims=True)

    def bisect16(ref, target, cnt_all):
        def body(p, carry):
            t_u, cnt_ok, cnt_rej = carry
            cand_u = t_u | jnp.left_shift(jnp.int32(1), 15 - p)
            cnt = count16(ref, (cand_u - HALF16).astype(I16))
            ok = cnt >= target
            return jnp.where(ok, cand_u, t_u), jnp.where(ok, cnt, cnt_ok), jnp.where(ok, cnt_rej, cnt)
        return lax.fori_loop(0, 16, body, (jnp.zeros((1, qb), I32), cnt_all, jnp.zeros((1, qb), I32)))

    hi_u, cnt_ge_hi, cnt_gt_hi = bisect16(hi_ref, topk, jnp.full((1, qb), nrblk * rb, I32))
    hi_t = (hi_u - HALF16).astype(I16)

    def mask_lo(r, carry):
        r0 = pl.multiple_of(r * rb, rb)
        lo_ref[pl.ds(r0, rb), :] = jnp.where(hi_ref[pl.ds(r0, rb), :] == hi_t, lo_ref[pl.ds(r0, rb), :],
                                             jnp.int16(-HALF16))
        return carry

    lax.fori_loop(0, nrblk, mask_lo, 0)
    lo_u, cnt_ge_lo, _ = bisect16(lo_ref, topk - cnt_gt_hi, cnt_ge_hi - cnt_gt_hi)
    t = jnp.left_shift(hi_u - HALF16, 16) + lo_u
    cnt_t = cnt_gt_hi + cnt_ge_lo

    excess = jnp.where(t > NEG_KEY, cnt_t - topk, 0)

    @pl.when(jnp.max(excess) > 0)
    def _():
        keep = topk - count_rows(lambda k, r0: k >= t + 1)

        def idx_body(p, pos):
            cand = pos + jnp.left_shift(jnp.int32(1), idx_bits - 1 - p)
            c = count_rows(lambda k, r0: (k == t) & (
                r0 + lax.broadcasted_iota(I32, (rb, qb), 0) < cand))
            return jnp.where(c < keep, cand, pos)

        pos = lax.fori_loop(0, idx_bits, idx_body, jnp.zeros((1, qb), I32))

        def demote(r, carry):
            r0 = pl.multiple_of(r * rb, rb)
            k = keys_ref[pl.ds(r0, rb), :]
            row = r0 + lax.broadcasted_iota(I32, (rb, qb), 0)
            keys_ref[pl.ds(r0, rb), :] = jnp.where((k == t) & (row > pos), NEG_KEY, k)
            return carry

        lax.fori_loop(0, nrblk, demote, 0)

    thr = jnp.maximum(t, NEG_KEY + 1)

    acc_ref[...] = jnp.zeros(acc_ref.shape, F32)

    def put(carry, h, item):
        return carry[:h] + (item,) + carry[h + 1:]

    def attn_logits(c, slot, h, carry):
        r0 = pl.multiple_of(c * kb, kb)
        if h == 0:
            bias_ref[slot] = jnp.where(keys_ref[pl.ds(r0, kb), :] >= thr, 0.0, -jnp.inf)
        m_prev = carry[h][1]
        lg = _dot_nt(ckv_ref[pl.ds(r0, kb), :], qlat_ref[h]) + bias_ref[slot]
        stage_ref[slot, h] = lg
        return put(carry, h, (m_prev, jnp.maximum(m_prev, jnp.max(lg, axis=0, keepdims=True))))

    def attn_values(c, slot, h, carry):
        m_old, m_new = carry[h]
        kvt = ckvt_ref[:, pl.ds(pl.multiple_of(c * kb, kb), kb)]
        p = jnp.exp2(stage_ref[slot, h] - m_new).astype(BF16)
        acc_ref[h] = acc_ref[h] * jnp.exp2(m_old - m_new) + _dot(kvt, p)
        return carry

    m_init = jnp.full((1, qb), -1e30, F32)
    skewed(H, attn_values, attn_logits, ((m_init, m_init),) * H)

    out_t = jnp.concatenate(
        [_dot(wuvt_ref[h], (acc_ref[h, 0:A_KV_RANK, :] / acc_ref[h, A_KV_RANK:A_KV_RANK + 1, :]).astype(BF16))
         for h in range(H)], axis=0)
    out_ref[...] = _rms_norm_rows(out_t.T, g_ref[...])


def _dsa(qidx, qlat, misct, kidx, ckv, ckvt, wuvt, g, *, B, S):
    T = B * S
    qb = min(256, S)
    nq = S // qb
    kb = qb
    rb = 2 * qb if nq % 2 == 0 else qb
    topk = min(TOPK_MAX, S // 4)
    idx_bits = max(1, (S - 1).bit_length())
    slab = min(64, kb)
    kern = functools.partial(_dsa_kernel, qb=qb, kb=kb, rb=rb, slab=slab, topk=topk, idx_bits=idx_bits)
    return pl.pallas_call(
        kern,
        grid=(B, nq),
        in_specs=[
            pl.BlockSpec((IDX_HEADS, qb, IDX_DIM), lambda b, j: (0, b * nq + j, 0)),
            pl.BlockSpec((A_HEADS, qb, A_KV_RANK), lambda b, j: (0, b * nq + j, 0)),
            pl.BlockSpec((IDX_HEADS, qb), lambda b, j: (MISC_WIDX // IDX_HEADS, b * nq + j)),
            pl.BlockSpec((S, IDX_DIM), lambda b, j: (b, 0)),
            pl.BlockSpec((S, A_KV_RANK), lambda b, j: (b, 0)),
            pl.BlockSpec((KVT_ROWS, S), lambda b, j: (0, b)),
            pl.BlockSpec((A_HEADS, A_V_DIM, A_KV_RANK), lambda b, j: (0, 0, 0)),
            pl.BlockSpec((1, A_OUT_W), lambda b, j: (0, 0)),
        ],
        out_specs=pl.BlockSpec((qb, A_OUT_W), lambda b, j: (b * nq + j, 0)),
        out_shape=jax.ShapeDtypeStruct((T, A_OUT_W), F32),
        scratch_shapes=[
            pltpu.VMEM((S, qb), I32),
            pltpu.VMEM((S, qb), I16),
            pltpu.VMEM((S, qb), I16),
            pltpu.VMEM((A_HEADS, KVT_ROWS, qb), F32),
            pltpu.VMEM((rb // kb, A_HEADS, kb, qb), F32),
            pltpu.VMEM((rb // kb, kb, qb), F32),
        ],
        compiler_params=pltpu.CompilerParams(
            dimension_semantics=("parallel", "arbitrary"), vmem_limit_bytes=VMEM_LIMIT),
        name="dsa",
    )(qidx, qlat, misct, kidx, ckv, ckvt, wuvt, g)


def _gdn_kernel(qkv_ref, z_ref, misc_ref, at_ref, cw_ref, alog_c_ref, dtb_c_ref, alog_r_ref, dtb_r_ref,
                ng_ref, out_ref, xbuf_ref, state_ref, *, tb):
    C = GDN_CHUNK
    nck = tb // C
    H = DN_HEADS
    step = pl.program_id(1)

    @pl.when(step == 0)
    def _():
        xbuf_ref[0:SUBLANES, :] = jnp.zeros((SUBLANES, DN_QKV_W), F32)
        state_ref[...] = jnp.zeros(state_ref.shape, F32)

    xbuf_ref[SUBLANES:SUBLANES + tb, :] = qkv_ref[...]
    y = jnp.zeros((tb, DN_QKV_W), F32)
    for k in range(DN_CONV):
        off = SUBLANES - (DN_CONV - 1) + k
        y = y + xbuf_ref[off:off + tb, :] * cw_ref[k:k + 1, :]
    xbuf_ref[0:SUBLANES, :] = xbuf_ref[tb:tb + SUBLANES, :]
    y = _silu(y)

    def softplus(v):
        return jnp.maximum(v, 0.0) + jnp.log(1.0 + jnp.exp(-jnp.abs(v)))

    misc = misc_ref[...]
    beta_c = _sigmoid(misc[:, MISC_B:MISC_B + H])
    g_c = -jnp.exp(alog_r_ref[...]) * softplus(misc[:, MISC_A:MISC_A + H] + dtb_r_ref[...])
    g_r = -jnp.exp(alog_c_ref[...]) * softplus(at_ref[...] + dtb_c_ref[...])

    ri = lax.broadcasted_iota(I32, (tb, tb), 0)
    ci = lax.broadcasted_iota(I32, (tb, tb), 1)
    same = (ri // C) == (ci // C)
    tri_incl = same & (ci <= ri)
    hi = lax.Precision.HIGHEST
    low_incl = jnp.where(tri_incl, 1.0, 0.0).astype(F32)
    up_incl = jnp.where(same & (ri <= ci), 1.0, 0.0).astype(F32)
    blk = jnp.where(same, 1.0, 0.0).astype(F32)
    gcum_c = _dot(low_incl, g_c, hi)
    gcum_r = _dot(g_r, up_incl, hi)
    glast_c = _dot(blk, g_c, hi)
    glast_r = _dot(g_r, blk, hi)
    eye = jnp.where(ri == ci, 1.0, 0.0).astype(F32)
    blk_f = [jnp.where((ri // b) == (ci // b), 1.0, 0.0).astype(F32) for b in (8, 16, 32)] + [blk]
    ring_f = [blk_f[i] - blk_f[i - 1] for i in range(1, len(blk_f))]
    off_diag = 1.0 - eye

    hs = range(H)
    rs = lambda v: lax.rsqrt(jnp.sum(v * v, axis=-1, keepdims=True) + RMS_EPS)
    qs = [y[:, h * DN_K_DIM:(h + 1) * DN_K_DIM] for h in hs]
    ks = [y[:, DN_KEY_W + h * DN_K_DIM:DN_KEY_W + (h + 1) * DN_K_DIM] for h in hs]
    vs = [y[:, 2 * DN_KEY_W + h * DN_V_DIM:2 * DN_KEY_W + (h + 1) * DN_V_DIM] for h in hs]
    qs = [q * rs(q) * (DN_K_DIM ** -0.5) for q in qs]
    ks = [k * rs(k) for k in ks]
    gcs = [gcum_c[:, h:h + 1] for h in hs]
    betas = [beta_c[:, h:h + 1] for h in hs]
    kbs = [ks[h] * betas[h] for h in hs]
    kbf = [k.astype(BF16) for k in ks]
    kk = [_dot_nt(kbs[h].astype(BF16), kbf[h]) for h in hs]
    qk = [_dot_nt(qs[h].astype(BF16), kbf[h]) for h in hs]
    decay = [jnp.exp(jnp.where(tri_incl, gcs[h] - gcum_r[h:h + 1, :], -jnp.inf)) for h in hs]
    lower = [kk[h] * (decay[h] * off_diag) for h in hs]
    attn = [(qk[h] * decay[h]).astype(BF16) for h in hs]
    pw = [lower[h] * blk_f[0] for h in hs]
    x_inv = [eye - pw[h] for h in hs]
    for _ in range(2):
        pwb = [p.astype(BF16) for p in pw]
        pw = [_dot(p, p) for p in pwb]
        x_inv = [x_inv[h] + _dot(x_inv[h].astype(BF16), pw[h].astype(BF16)) for h in hs]
    for ring in ring_f:
        xb = [x.astype(BF16) for x in x_inv]
        t1 = [_dot(xb[h], (lower[h] * ring).astype(BF16)).astype(BF16) for h in hs]
        x_inv = [x_inv[h] - _dot(t1[h], xb[h]) for h in hs]
    egs = [jnp.exp(g) for g in gcs]
    uw = [_dot(x_inv[h].astype(BF16),
               jnp.concatenate([vs[h] * betas[h], kbs[h] * egs[h]], axis=1).astype(BF16)) for h in hs]
    us = [m[:, :DN_V_DIM] for m in uw]
    wkb = [m[:, DN_V_DIM:].astype(BF16) for m in uw]
    qd = [(qs[h] * egs[h]).astype(BF16) for h in hs]
    ke = [(ks[h] * jnp.exp(glast_c[:, h:h + 1] - gcs[h])).astype(BF16) for h in hs]

    st = [state_ref[h] for h in hs]
    outs = [[] for _ in hs]
    for n in range(nck):
        r = slice(n * C, (n + 1) * C)
        stb = [s.astype(BF16) for s in st]
        v_nb = [(us[h][r] - _dot(wkb[h][r], stb[h])).astype(BF16) for h in hs]
        for h in hs:
            outs[h].append(_dot(qd[h][r], stb[h]) + _dot(attn[h][r, r], v_nb[h]))
        cd = [jnp.exp(glast_r[h:h + 1, n * C:n * C + 1]) for h in hs]
        st = [st[h] * cd[h] + _dot_tn(ke[h][r], v_nb[h]) for h in hs]
    head_outs = []
    for h in hs:
        state_ref[h] = st[h]
        o = jnp.concatenate(outs[h], axis=0)
        o = o * lax.rsqrt(jnp.mean(o * o, axis=-1, keepdims=True) + RMS_EPS) * ng_ref[...]
        head_outs.append(o * _silu(z_ref[:, h * DN_V_DIM:(h + 1) * DN_V_DIM]))
    out_ref[...] = jnp.concatenate(head_outs, axis=1).astype(out_ref.dtype)


def _gdn(qkv, z, misc, misct, conv_w, a_log, dt_bias, norm_g, *, B, S):
    T = B * S
    tb = GDN_TILE
    ns = S // tb
    H = DN_HEADS
    kern = functools.partial(_gdn_kernel, tb=tb)
    const = lambda *s: pl.BlockSpec(s, lambda b, i: (0,) * len(s))
    return pl.pallas_call(
        kern,
        grid=(B, ns),
        in_specs=[
            pl.BlockSpec((tb, DN_QKV_W), lambda b, i: (b * ns + i, 0)),
            pl.BlockSpec((tb, DN_VAL_W), lambda b, i: (b * ns + i, 0)),
            pl.BlockSpec((tb, MISC_W), lambda b, i: (b * ns + i, 0)),
            pl.BlockSpec((H, tb), lambda b, i: (MISC_A // H, b * ns + i)),
            const(DN_CONV, DN_QKV_W), const(H, 1), const(H, 1), const(1, H), const(1, H), const(1, DN_V_DIM),
        ],
        out_specs=pl.BlockSpec((tb, DN_VAL_W), lambda b, i: (b * ns + i, 0)),
        out_shape=jax.ShapeDtypeStruct((T, DN_VAL_W), BF16),
        scratch_shapes=[
            pltpu.VMEM((tb + SUBLANES, DN_QKV_W), F32),
            pltpu.VMEM((H, DN_K_DIM, DN_V_DIM), F32),
        ],
        compiler_params=pltpu.CompilerParams(
            dimension_semantics=("parallel", "arbitrary"), vmem_limit_bytes=VMEM_LIMIT),
        name="gdn",
    )(qkv, z, misc, misct, conv_w, a_log.reshape(H, 1), dt_bias.reshape(H, 1),
      a_log.reshape(1, H), dt_bias.reshape(1, H), norm_g.reshape(1, DN_V_DIM))


def _out_proj_kernel(a_ref, d_ref, x_ref, wa_ref, wd_ref, g_ref, b_ref, h_ref, *, alpha):
    mix = _dot(a_ref[...].astype(BF16), wa_ref[...]) + _dot(d_ref[...], wd_ref[...])
    h_ref[...] = _layer_norm_rows(alpha * x_ref[...] + mix, g_ref[...], b_ref[...])


def _out_proj(a, d, x2, wa, wd, g, b, *, alpha, tm):
    T = x2.shape[0]
    const = lambda *s: pl.BlockSpec(s, lambda i: (0,) * len(s))
    return pl.pallas_call(
        functools.partial(_out_proj_kernel, alpha=alpha),
        grid=(T // tm,),
        in_specs=[
            pl.BlockSpec((tm, A_OUT_W), lambda i: (i, 0)),
            pl.BlockSpec((tm, DN_VAL_W), lambda i: (i, 0)),
            pl.BlockSpec((tm, D_MODEL), lambda i: (i, 0)),
            const(A_OUT_W, D_MODEL), const(DN_VAL_W, D_MODEL), const(1, D_MODEL), const(1, D_MODEL),
        ],
        out_specs=pl.BlockSpec((tm, D_MODEL), lambda i: (i, 0)),
        out_shape=jax.ShapeDtypeStruct((T, D_MODEL), F32),
        compiler_params=pltpu.CompilerParams(dimension_semantics=("parallel",), vmem_limit_bytes=VMEM_LIMIT),
        name="out_proj",
    )(a, d, x2, wa, wd, g, b)


def _ffn_kernel(h_ref, halo_ref, wg_ref, wu_ref, cw_ref, cb_ref, wd_ref, g_ref, b_ref, out_ref,
                gbuf_ref, acc_ref, *, alpha, tm, tiles_per_seq):
    i = pl.program_id(0)
    f = pl.program_id(1)
    nf = pl.num_programs(1)
    hb = h_ref[...].astype(BF16)
    wg = wg_ref[...]
    gate = _dot(hb, wg)
    up = _dot(hb, wu_ref[...])
    halo = _dot(halo_ref[...].astype(BF16), wg)
    halo = jnp.where(i % tiles_per_seq == 0, 0.0, halo)
    gbuf_ref[0:SUBLANES, :] = halo
    gbuf_ref[SUBLANES:SUBLANES + tm, :] = gate
    conv = jnp.zeros(gate.shape, F32)
    for k in range(FFN_CONV):
        off = SUBLANES - (FFN_CONV - 1) + k
        conv = conv + gbuf_ref[off:off + tm, :] * cw_ref[k:k + 1, :]
    act = (_silu(conv + cb_ref[...]) * up).astype(BF16)
    part = _dot(act, wd_ref[...])

    @pl.when(f == 0)
    def _():
        acc_ref[...] = part

    @pl.when(f > 0)
    def _():
        acc_ref[...] += part

    @pl.when(f == nf - 1)
    def _():
        out_ref[...] = _layer_norm_rows(alpha * h_ref[...] + acc_ref[...], g_ref[...], b_ref[...])


def _ffn(h, wg, wu, cw, cb, wd, g, b, *, alpha, S, tm, tf):
    T = h.shape[0]
    nf = D_FF // tf
    hs = tm // SUBLANES
    kern = functools.partial(_ffn_kernel, alpha=alpha, tm=tm, tiles_per_seq=S // tm)
    return pl.pallas_call(
        kern,
        grid=(T // tm, nf),
        in_specs=[
            pl.BlockSpec((tm, D_MODEL), lambda i, f: (i, 0)),
            pl.BlockSpec((SUBLANES, D_MODEL), lambda i, f: (jnp.maximum(i * hs - 1, 0), 0)),
            pl.BlockSpec((D_MODEL, tf), lambda i, f: (0, f)),
            pl.BlockSpec((D_MODEL, tf), lambda i, f: (0, f)),
            pl.BlockSpec((FFN_CONV, tf), lambda i, f: (0, f)),
            pl.BlockSpec((1, tf), lambda i, f: (0, f)),
            pl.BlockSpec((tf, D_MODEL), lambda i, f: (f, 0)),
            pl.BlockSpec((1, D_MODEL), lambda i, f: (0, 0)),
            pl.BlockSpec((1, D_MODEL), lambda i, f: (0, 0)),
        ],
        out_specs=pl.BlockSpec((tm, D_MODEL), lambda i, f: (i, 0)),
        out_shape=jax.ShapeDtypeStruct((T, D_MODEL), F32),
        scratch_shapes=[pltpu.VMEM((tm + SUBLANES, tf), F32), pltpu.VMEM((tm, D_MODEL), F32)],
        compiler_params=pltpu.CompilerParams(
            dimension_semantics=("parallel", "arbitrary"), vmem_limit_bytes=VMEM_LIMIT),
        name="ffn",
    )(h, h, wg, wu, cw, cb, wd, g, b)


def _regroup_w_in(w):
    offs = [0]
    for s in IN_SIZES:
        offs.append(offs[-1] + s)
    cq, ckv, kidx, widx, qkv, z, b, a = (w[:, offs[i]:offs[i + 1]] for i in range(8))
    pad = jnp.zeros((w.shape[0], MISC_W - (IDX_DIM + IDX_HEADS + 2 * DN_HEADS)), w.dtype)
    return jnp.concatenate([cq, ckv, kidx, widx, b, a, pad, qkv, z], axis=1).astype(BF16)


def _layer(x2, p, *, B, S, alpha):
    row = lambda v: v.reshape(1, -1)
    wc = _regroup_w_in(p["w_in"])
    wuq = p["w_uq"].reshape(A_Q_RANK, A_HEADS, A_QK_DIM).transpose(1, 0, 2).astype(BF16)
    wqi = p["w_qidx"].reshape(A_Q_RANK, IDX_HEADS, IDX_DIM).transpose(1, 0, 2).astype(BF16)
    qlat, qidx, ckv, ckvt, kidx, misc, misct, qkv, z = _in_proj(
        x2, wc, row(p["q_norm_g"]), wuq, p["w_uk"].astype(BF16), wqi, row(p["kv_norm_g"]),
        row(p["kidx_ln_g"]), row(p["kidx_ln_b"]), tm=min(512, S))
    a_out = _dsa(qidx, qlat, misct, kidx, ckv, ckvt, p["w_uv"].transpose(0, 2, 1).astype(BF16),
                 row(p["attn_out_g"]), B=B, S=S)
    d_out = _gdn(qkv, z, misc, misct, p["dn_conv_w"], p["dn_a_log"], p["dn_dt_bias"], p["dn_norm_g"],
                 B=B, S=S)
    w_out = p["w_out"].astype(BF16)
    h = _out_proj(a_out, d_out, x2, w_out[:A_OUT_W], w_out[A_OUT_W:], row(p["ln1_g"]), row(p["ln1_b"]),
                  alpha=alpha, tm=min(512, S))
    w_ffn = p["ffn_w_in"].astype(BF16)
    return _ffn(h, w_ffn[:, :D_FF], w_ffn[:, D_FF:], p["ffn_conv_w"], row(p["ffn_conv_b"]),
                p["ffn_w_down"].astype(BF16), row(p["ln2_g"]), row(p["ln2_b"]),
                alpha=alpha, S=S, tm=min(512, S), tf=D_FF // 2)


_PARAM_NAMES = ("w_in", "q_norm_g", "w_uq", "w_qidx", "kv_norm_g", "w_uk", "w_uv", "kidx_ln_g", "kidx_ln_b",
                "attn_out_g", "dn_conv_w", "dn_a_log", "dn_dt_bias", "dn_norm_g", "w_out", "ln1_g", "ln1_b",
                "ffn_w_in", "ffn_conv_w", "ffn_conv_b", "ffn_w_down", "ln2_g", "ln2_b")


def kernel(x, w_in, q_norm_g, w_uq, w_qidx, kv_norm_g, w_uk, w_uv, kidx_ln_g, kidx_ln_b, attn_out_g, dn_conv_w, dn_a_log, dn_dt_bias, dn_norm_g, w_out, ln1_g, ln1_b, ffn_w_in, ffn_conv_w, ffn_conv_b, ffn_w_down, ln2_g, ln2_b):
    params = (w_in, q_norm_g, w_uq, w_qidx, kv_norm_g, w_uk, w_uv, kidx_ln_g, kidx_ln_b, attn_out_g, dn_conv_w,
              dn_a_log, dn_dt_bias, dn_norm_g, w_out, ln1_g, ln1_b, ffn_w_in, ffn_conv_w, ffn_conv_b, ffn_w_down,
              ln2_g, ln2_b)
    B, S, D = x.shape
    depth = w_in.shape[0]
    alpha = (2 * depth) ** 0.25
    x2 = x.reshape(B * S, D)
    for l in range(depth):
        x2 = _layer(x2, {n: v[l] for n, v in zip(_PARAM_NAMES, params)}, B=B, S=S, alpha=alpha)
    return x2.reshape(B, S, D)
```

```python
import functools

import jax
import jax.numpy as jnp
import numpy as np
from jax import lax
from jax.experimental import pallas as pl
from jax.experimental.pallas import tpu as pltpu

F32 = jnp.float32
BF16 = jnp.bfloat16
I32 = jnp.int32
I16 = jnp.int16

D_MODEL = 1024
A_HEADS = 8
A_QK_DIM = 64
A_V_DIM = 64
A_Q_RANK = 256
A_KV_RANK = 128
IDX_HEADS = 8
IDX_DIM = 64
TOPK_MAX = 256
DN_HEADS = 8
DN_K_DIM = 64
DN_V_DIM = 64
DN_CONV = 4
D_FF = 2816
FFN_CONV = 3
RMS_EPS = 1e-6
LN_EPS = 1e-5

A_OUT_W = A_HEADS * A_V_DIM
DN_KEY_W = DN_HEADS * DN_K_DIM
DN_VAL_W = DN_HEADS * DN_V_DIM
DN_QKV_W = 2 * DN_KEY_W + DN_VAL_W
IN_SIZES = (A_Q_RANK, A_KV_RANK, IDX_DIM, IDX_HEADS, DN_QKV_W, DN_VAL_W, DN_HEADS, DN_HEADS)

MISC_W = 128
MISC_WIDX = IDX_DIM
MISC_B = IDX_DIM + IDX_HEADS
MISC_A = MISC_B + DN_HEADS
PROJ_W = A_Q_RANK + A_KV_RANK + MISC_W + DN_QKV_W + DN_VAL_W

LOG2E = 1.4426950408889634
SUBLANES = 8
PACK16 = 16
KVT_ROWS = A_KV_RANK + PACK16
HALF16 = 2 ** 15
INT_MIN = -(2 ** 31)
NEG_KEY = INT_MIN + 0x00800000
VMEM_LIMIT = 56 * 1024 * 1024

GDN_CHUNK = 64
GDN_SOLVE = 256
GDN_SOLVES_PER_STEP = 1


def _dot(a, b, precision=None):
    return jnp.dot(a, b, preferred_element_type=F32, precision=precision)


def _dot_nt(a, b, precision=None):
    return lax.dot_general(a, b, (((1,), (1,)), ((), ())), preferred_element_type=F32, precision=precision)


def _dot_tn(a, b, precision=None):
    return lax.dot_general(a, b, (((0,), (0,)), ((), ())), preferred_element_type=F32, precision=precision)


def _sigmoid(x):
    return 1.0 / (1.0 + jnp.exp(-x))


def _silu(x):
    return x * _sigmoid(x)


def _layer_norm_rows(v, g, b):
    mu = jnp.mean(v, axis=-1, keepdims=True)
    d = v - mu
    var = jnp.mean(d * d, axis=-1, keepdims=True)
    return d * lax.rsqrt(var + LN_EPS) * g + b


def _rms_norm_rows(v, g):
    return v * lax.rsqrt(jnp.mean(v * v, axis=-1, keepdims=True) + RMS_EPS) * g


def _in_proj_kernel(x_ref, w_ref, qg_ref, wuq_ref, wuk_ref, wqi_ref, kvg_ref, lng_ref, lnb_ref,
                    qlat_ref, qidx_ref, ckv_ref, ckvt_ref, kidx_ref, misc_ref, misct_ref, qkv_ref, z_ref):
    xb = x_ref[...].astype(BF16)
    proj = _dot(xb, w_ref[...])
    o = 0
    c_q = proj[:, o:o + A_Q_RANK]; o += A_Q_RANK
    c_kv = proj[:, o:o + A_KV_RANK]; o += A_KV_RANK
    misc = proj[:, o:o + MISC_W]; o += MISC_W
    qkv_ref[...] = proj[:, o:o + DN_QKV_W]; o += DN_QKV_W
    z_ref[...] = proj[:, o:o + DN_VAL_W]

    cqn = _rms_norm_rows(c_q, qg_ref[...]).astype(BF16)
    for h in range(A_HEADS):
        q_h = _dot(cqn, wuq_ref[h]).astype(BF16)
        qlat_ref[h] = (_dot(q_h, wuk_ref[h]) * (A_QK_DIM ** -0.5 * LOG2E)).astype(BF16)
        qidx_ref[h] = _dot(cqn, wqi_ref[h]).astype(BF16)
    ckv = _rms_norm_rows(c_kv, kvg_ref[...])
    ckv_ref[...] = ckv.astype(BF16)
    ckvt_ref[0:A_KV_RANK, :] = ckv.T.astype(BF16)
    ckvt_ref[A_KV_RANK:KVT_ROWS, :] = jnp.ones((KVT_ROWS - A_KV_RANK, ckv.shape[0]), BF16)
    kidx_ref[...] = _layer_norm_rows(misc[:, :IDX_DIM], lng_ref[...], lnb_ref[...]).astype(BF16)
    misc_ref[...] = misc
    misct_ref[...] = misc.T


def _in_proj(x2, wc, qg, wuq, wuk, wqi, kvg, lng, lnb, *, tm):
    T = x2.shape[0]
    const = lambda *s: pl.BlockSpec(s, lambda i: (0,) * len(s))
    return pl.pallas_call(
        _in_proj_kernel,
        grid=(T // tm,),
        in_specs=[
            pl.BlockSpec((tm, D_MODEL), lambda i: (i, 0)),
            const(D_MODEL, PROJ_W), const(1, A_Q_RANK), const(A_HEADS, A_Q_RANK, A_QK_DIM),
            const(A_HEADS, A_QK_DIM, A_KV_RANK), const(IDX_HEADS, A_Q_RANK, IDX_DIM),
            const(1, A_KV_RANK), const(1, IDX_DIM), const(1, IDX_DIM),
        ],
        out_specs=[
            pl.BlockSpec((A_HEADS, tm, A_KV_RANK), lambda i: (0, i, 0)),
            pl.BlockSpec((IDX_HEADS, tm, IDX_DIM), lambda i: (0, i, 0)),
            pl.BlockSpec((tm, A_KV_RANK), lambda i: (i, 0)),
            pl.BlockSpec((KVT_ROWS, tm), lambda i: (0, i)),
            pl.BlockSpec((tm, IDX_DIM), lambda i: (i, 0)),
            pl.BlockSpec((tm, MISC_W), lambda i: (i, 0)),
            pl.BlockSpec((MISC_W, tm), lambda i: (0, i)),
            pl.BlockSpec((tm, DN_QKV_W), lambda i: (i, 0)),
            pl.BlockSpec((tm, DN_VAL_W), lambda i: (i, 0)),
        ],
        out_shape=[
            jax.ShapeDtypeStruct((A_HEADS, T, A_KV_RANK), BF16),
            jax.ShapeDtypeStruct((IDX_HEADS, T, IDX_DIM), BF16),
            jax.ShapeDtypeStruct((T, A_KV_RANK), BF16),
            jax.ShapeDtypeStruct((KVT_ROWS, T), BF16),
            jax.ShapeDtypeStruct((T, IDX_DIM), BF16),
            jax.ShapeDtypeStruct((T, MISC_W), F32),
            jax.ShapeDtypeStruct((MISC_W, T), F32),
            jax.ShapeDtypeStruct((T, DN_QKV_W), F32),
            jax.ShapeDtypeStruct((T, DN_VAL_W), F32),
        ],
        compiler_params=pltpu.CompilerParams(dimension_semantics=("parallel",), vmem_limit_bytes=VMEM_LIMIT),
        name="in_proj",
    )(x2, wc, qg, wuq, wuk, wqi, kvg, lng, lnb)


def _dsa_kernel(qidx_ref, qlat_ref, wt_ref, kidx_ref, ckv_ref, ckvt_ref, wuvt_ref, g_ref, out_ref,
                keys_ref, hi_ref, lo_ref, acc_ref, stage_ref, bias_ref, *, qb, kb, rb, slab, topk, idx_bits):
    H = A_HEADS
    j = pl.program_id(1)
    q0 = j * qb
    nkeys = (j + 1) * qb
    nrblk = (nkeys + rb - 1) // rb
    ups = rb // kb
    nchunk = nrblk * ups
    w = wt_ref[...] * (IDX_HEADS ** -0.5 * IDX_DIM ** -0.5)
    row_minus_lane = (lax.broadcasted_iota(I32, (slab, qb), 0) - lax.broadcasted_iota(I32, (slab, qb), 1))

    def skewed(nparts, consume, produce, carry):
        def chunk(c, k, carry):
            nxt = jnp.minimum(c + 1, nchunk - 1)
            if ups > 1:
                for i in range(nparts):
                    carry = produce(nxt, (k + 1) % ups, i, consume(c, k, i, carry))
                return carry
            for i in range(nparts):
                carry = consume(c, k, i, carry)
            for i in range(nparts):
                carry = produce(nxt, k, i, carry)
            return carry

        def step(i, carry):
            for k in range(ups):
                carry = chunk(i * ups + k, k, carry)
            return carry

        for i in range(nparts):
            carry = produce(0, 0, i, carry)
        return lax.fori_loop(0, nrblk, step, carry)

    nslab = kb // slab
    hps = H // nslab

    def score_matmuls(c, slot, i, carry):
        kc = kidx_ref[pl.ds(pl.multiple_of(c * kb, kb), kb), :]
        for h in range(i * hps, (i + 1) * hps):
            stage_ref[slot, h] = _dot_nt(kc, qidx_ref[h])
        return carry

    def score_keys(c, slot, i, carry):
        r0 = pl.multiple_of(c * kb, kb) + i * slab
        s = jnp.zeros((slab, qb), F32)
        for h in range(H):
            s = s + jnp.maximum(stage_ref[slot, h, i * slab:(i + 1) * slab, :], 0.0) * w[h:h + 1, :]
        bits = lax.bitcast_convert_type(s, I32)
        key = jnp.where(bits < 0, INT_MIN - bits, bits)
        key = jnp.where(row_minus_lane <= q0 - r0, key, NEG_KEY)
        keys_ref[pl.ds(r0, slab), :] = key
        hi_ref[pl.ds(r0, slab), :] = jnp.right_shift(key, 16).astype(I16)
        lo_ref[pl.ds(r0, slab), :] = ((key & 0xFFFF) - HALF16).astype(I16)
        return carry

    skewed(nslab, score_keys, score_matmuls, 0)

    nacc = 4

    def count_rows(pred):
        def body(r, acc):
            r0 = pl.multiple_of(r * rb, rb)
            k = keys_ref[pl.ds(r0, rb), :]
            m = jnp.where(pred(k, r0), 1, 0).astype(I32)
            return acc + m.reshape(nacc, rb // (nacc * SUBLANES), SUBLANES, qb).sum(axis=1)
        acc = lax.fori_loop(0, nrblk, body, jnp.zeros((nacc, SUBLANES, qb), I32))
        return acc.sum(axis=0).sum(axis=0, keepdims=True)

    def count16(ref, cand):
        def body(r, accs):
            r0 = pl.multiple_of(r * rb, rb)
            m = jnp.where(ref[pl.ds(r0, rb), :] >= cand, jnp.int16(1), jnp.int16(0))
            accs = list(accs)
            for i in range(rb // PACK16):
                accs[i % nacc] = accs[i % nacc] + m[i * PACK16:(i + 1) * PACK16, :]
            return tuple(accs)
        accs = lax.fori_loop(0, nrblk, body, tuple(jnp.zeros((PACK16, qb), I16) for _ in range(nacc)))
        tot = accs[0].astype(I32)
        for a in accs[1:]:
            tot = tot + a.astype(I32)
        return tot.sum(axis=0, keepdims=True)

    def bisect16(ref, target, cnt_all):
        def body(p, carry):
            t_u, cnt_ok, cnt_rej = carry
            cand_u = t_u | jnp.left_shift(jnp.int32(1), 15 - p)
            cnt = count16(ref, (cand_u - HALF16).astype(I16))
            ok = cnt >= target
            return jnp.where(ok, cand_u, t_u), jnp.where(ok, cnt, cnt_ok), jnp.where(ok, cnt_rej, cnt)
        return lax.fori_loop(0, 16, body, (jnp.zeros((1, qb), I32), cnt_all, jnp.zeros((1, qb), I32)))

    hi_u, cnt_ge_hi, cnt_gt_hi = bisect16(hi_ref, topk, jnp.full((1, qb), nrblk * rb, I32))
    hi_t = (hi_u - HALF16).astype(I16)

    def mask_lo(r, carry):
        r0 = pl.multiple_of(r * rb, rb)
        lo_ref[pl.ds(r0, rb), :] = jnp.where(hi_ref[pl.ds(r0, rb), :] == hi_t, lo_ref[pl.ds(r0, rb), :],
                                             jnp.int16(-HALF16))
        return carry

    lax.fori_loop(0, nrblk, mask_lo, 0)
    lo_u, cnt_ge_lo, _ = bisect16(lo_ref, topk - cnt_gt_hi, cnt_ge_hi - cnt_gt_hi)
    t = jnp.left_shift(hi_u - HALF16, 16) + lo_u
    cnt_t = cnt_gt_hi + cnt_ge_lo

    excess = jnp.where(t > NEG_KEY, cnt_t - topk, 0)

    @pl.when(jnp.max(excess) > 0)
    def _():
        keep = topk - count_rows(lambda k, r0: k >= t + 1)

        def idx_body(p, pos):
            cand = pos + jnp.left_shift(jnp.int32(1), idx_bits - 1 - p)
            c = count_rows(lambda k, r0: (k == t) & (
                r0 + lax.broadcasted_iota(I32, (rb, qb), 0) < cand))
            return jnp.where(c < keep, cand, pos)

        pos = lax.fori_loop(0, idx_bits, idx_body, jnp.zeros((1, qb), I32))

        def demote(r, carry):
            r0 = pl.multiple_of(r * rb, rb)
            k = keys_ref[pl.ds(r0, rb), :]
            row = r0 + lax.broadcasted_iota(I32, (rb, qb), 0)
            keys_ref[pl.ds(r0, rb), :] = jnp.where((k == t) & (row > pos), NEG_KEY, k)
            return carry

        lax.fori_loop(0, nrblk, demote, 0)

    thr = jnp.maximum(t, NEG_KEY + 1)

    acc_ref[...] = jnp.zeros(acc_ref.shape, F32)

    def put(carry, h, item):
        return carry[:h] + (item,) + carry[h + 1:]

    def attn_logits(c, slot, h, carry):
        r0 = pl.multiple_of(c * kb, kb)
        if h == 0:
            bias_ref[slot] = jnp.where(keys_ref[pl.ds(r0, kb), :] >= thr, 0.0, -jnp.inf)
        m_prev = carry[h][1]
        lg = _dot_nt(ckv_ref[pl.ds(r0, kb), :], qlat_ref[h]) + bias_ref[slot]
        stage_ref[slot, h] = lg
        return put(carry, h, (m_prev, jnp.maximum(m_prev, jnp.max(lg, axis=0, keepdims=True))))

    def attn_values(c, slot, h, carry):
        m_old, m_new = carry[h]
        kvt = ckvt_ref[:, pl.ds(pl.multiple_of(c * kb, kb), kb)]
        p = jnp.exp2(stage_ref[slot, h] - m_new).astype(BF16)
        acc_ref[h] = acc_ref[h] * jnp.exp2(m_old - m_new) + _dot(kvt, p)
        return carry

    m_init = jnp.full((1, qb), -1e30, F32)
    skewed(H, attn_values, attn_logits, ((m_init, m_init),) * H)

    out_t = jnp.concatenate(
        [_dot(wuvt_ref[h], (acc_ref[h, 0:A_KV_RANK, :] / acc_ref[h, A_KV_RANK:A_KV_RANK + 1, :]).astype(BF16))
         for h in range(H)], axis=0)
    out_ref[...] = _rms_norm_rows(out_t.T, g_ref[...])


def _dsa(qidx, qlat, misct, kidx, ckv, ckvt, wuvt, g, *, B, S):
    T = B * S
    qb = min(256, S)
    nq = S // qb
    kb = qb
    rb = 2 * qb if nq % 2 == 0 else qb
    topk = min(TOPK_MAX, S // 4)
    idx_bits = max(1, (S - 1).bit_length())
    slab = min(64, kb)
    kern = functools.partial(_dsa_kernel, qb=qb, kb=kb, rb=rb, slab=slab, topk=topk, idx_bits=idx_bits)
    return pl.pallas_call(
        kern,
        grid=(B, nq),
        in_specs=[
            pl.BlockSpec((IDX_HEADS, qb, IDX_DIM), lambda b, j: (0, b * nq + j, 0)),
            pl.BlockSpec((A_HEADS, qb, A_KV_RANK), lambda b, j: (0, b * nq + j, 0)),
            pl.BlockSpec((IDX_HEADS, qb), lambda b, j: (MISC_WIDX // IDX_HEADS, b * nq + j)),
            pl.BlockSpec((S, IDX_DIM), lambda b, j: (b, 0)),
            pl.BlockSpec((S, A_KV_RANK), lambda b, j: (b, 0)),
            pl.BlockSpec((KVT_ROWS, S), lambda b, j: (0, b)),
            pl.BlockSpec((A_HEADS, A_V_DIM, A_KV_RANK), lambda b, j: (0, 0, 0)),
            pl.BlockSpec((1, A_OUT_W), lambda b, j: (0, 0)),
        ],
        out_specs=pl.BlockSpec((qb, A_OUT_W), lambda b, j: (b * nq + j, 0)),
        out_shape=jax.ShapeDtypeStruct((T, A_OUT_W), F32),
        scratch_shapes=[
            pltpu.VMEM((S, qb), I32),
            pltpu.VMEM((S, qb), I16),
            pltpu.VMEM((S, qb), I16),
            pltpu.VMEM((A_HEADS, KVT_ROWS, qb), F32),
            pltpu.VMEM((rb // kb, A_HEADS, kb, qb), F32),
            pltpu.VMEM((rb // kb, kb, qb), F32),
        ],
        compiler_params=pltpu.CompilerParams(
            dimension_semantics=("parallel", "arbitrary"), vmem_limit_bytes=VMEM_LIMIT),
        name="dsa",
    )(qidx, qlat, misct, kidx, ckv, ckvt, wuvt, g)


def _gdn_constants(sb):
    r, c = np.arange(sb)[:, None], np.arange(sb)[None, :]
    same = (r // GDN_CHUNK) == (c // GDN_CHUNK)
    tri = same & (c <= r)
    diag = [((r // b) == (c // b)).astype(np.float32) for b in (8, 16, 32)] + [same.astype(np.float32)]
    eye = (r == c).astype(np.float32)
    mats = [diag[0]] + [diag[i] - diag[i - 1] for i in (1, 2, 3)] + [eye, 1.0 - eye, np.where(tri, 0.0, -np.inf)]
    mask_c = np.concatenate([tri, same], axis=0).astype(np.float32)
    mask_r = np.concatenate([same & (r <= c), same], axis=1).astype(np.float32)
    spread = (np.arange(DN_KEY_W)[None, :] // DN_K_DIM == np.arange(DN_HEADS)[:, None]).astype(np.float32)
    spread3 = np.zeros((3, 3, 3, DN_HEADS, DN_KEY_W), np.float32)
    for j in range(3):
        spread3[j, :, j] = spread
    return (jnp.asarray(np.stack(mats), F32), jnp.asarray(mask_c, BF16), jnp.asarray(mask_r, BF16),
            jnp.asarray(spread3.reshape(3, 9 * DN_HEADS, DN_KEY_W), BF16), jnp.asarray(np.tile(spread, (1, 2)), BF16))


def _gdn_kernel(qkv_ref, z_ref, misc_ref, at_ref, cw_ref, alog_c_ref, dtb_c_ref, alog_r_ref, dtb_r_ref,
                ng_ref, fm_ref, mc_ref, mr_ref, sr_ref, sc_ref, out_ref, xbuf_ref, state_ref, *, tb, sb):
    C = GDN_CHUNK
    nck = tb // C
    H = DN_HEADS
    step = pl.program_id(1)

    @pl.when(step == 0)
    def _():
        xbuf_ref[0:SUBLANES, :] = jnp.zeros((SUBLANES, DN_QKV_W), F32)
        state_ref[...] = jnp.zeros(state_ref.shape, F32)

    xbuf_ref[SUBLANES:SUBLANES + tb, :] = qkv_ref[...]
    xe = xbuf_ref[...]
    y = xe[SUBLANES:] * cw_ref[DN_CONV - 1:DN_CONV, :]
    for d in range(1, DN_CONV):
        y = y + pltpu.roll(xe, d, 0)[SUBLANES:] * cw_ref[DN_CONV - 1 - d:DN_CONV - d, :]
    xbuf_ref[0:SUBLANES, :] = xe[tb:tb + SUBLANES]
    y = _silu(y)

    def softplus(v):
        return jnp.maximum(v, 0.0) + jnp.log(1.0 + jnp.exp(-jnp.abs(v)))

    misc = misc_ref[...]
    beta_c = _sigmoid(misc[:, MISC_B:MISC_B + H])
    g_c = -jnp.exp(alog_r_ref[...]) * softplus(misc[:, MISC_A:MISC_A + H] + dtb_r_ref[...])
    g_r = -jnp.exp(alog_c_ref[...]) * softplus(at_ref[...] + dtb_c_ref[...])

    rows = [slice(b * sb, (b + 1) * sb) for b in range(tb // sb)]
    mask_c = mc_ref[...]
    mask_r = mr_ref[...]

    def pieces(v, axis, n=3):
        out, rest = [], v
        for _ in range(n - 1):
            out.append(rest.astype(BF16))
            rest = rest - out[-1].astype(F32)
        return jnp.concatenate(out + [rest.astype(BF16)], axis=axis)

    sums_c = [_dot(mask_c, pieces(g_c[r], 1)) for r in rows]
    sums_c = [s[:, 0:H] + s[:, H:2 * H] + s[:, 2 * H:3 * H] for s in sums_c]
    sums_r = [_dot(pieces(g_r[:, r], 0), mask_r) for r in rows]
    sums_r = [s[0:H] + s[H:2 * H] + s[2 * H:3 * H] for s in sums_r]
    gcum_c = [s[0:sb] for s in sums_c]
    glast_c = [s[sb:2 * sb] for s in sums_c]
    gcum_r = [s[:, 0:sb] for s in sums_r]
    glast_r = [s[:, sb:2 * sb] for s in sums_r]
    blk8, ring_f, eye, off_diag, tri_bias = fm_ref[0], [fm_ref[1], fm_ref[2], fm_ref[3]], fm_ref[4], fm_ref[5], fm_ref[6]

    hs = range(H)
    bh = [(b, h) for b in range(len(rows)) for h in hs]
    ch = range(len(bh))
    def expand3(a, b, c):
        p = pieces(jnp.concatenate([a, b, c], axis=1), 1)
        return [_dot(p, sr_ref[j]) for j in range(3)]

    head_sums = lambda v: _dot_nt(pieces(v, 1, 2), sc_ref[...])
    yq, yk, yv = y[:, :DN_KEY_W], y[:, DN_KEY_W:2 * DN_KEY_W], y[:, 2 * DN_KEY_W:]
    gc_all = jnp.concatenate(gcum_c, axis=0)
    eg_c = jnp.exp(gc_all)
    rq_x, rk_x, beta_x = expand3(lax.rsqrt(head_sums(yq * yq) + RMS_EPS) * (DN_K_DIM ** -0.5),
                                 lax.rsqrt(head_sums(yk * yk) + RMS_EPS), beta_c)
    eg_x, egl_x, _ = expand3(eg_c, jnp.exp(jnp.concatenate(glast_c, axis=0) - gc_all), eg_c)
    q_n, k_n = yq * rq_x, yk * rk_x
    kb_all = k_n * beta_x
    k_bf, kb_bf, q_bf = k_n.astype(BF16), kb_all.astype(BF16), q_n.astype(BF16)
    rhs_v, rhs_k = (yv * beta_x).astype(BF16), (kb_all * eg_x).astype(BF16)
    qd_all = (q_n * eg_x).astype(BF16)
    ke_all = (k_n * egl_x).astype(BF16)
    head = lambda a, i: a[rows[bh[i][0]], bh[i][1] * DN_K_DIM:(bh[i][1] + 1) * DN_K_DIM]
    gcs = [gcum_c[b][:, h:h + 1] for b, h in bh]
    kk = [_dot_nt(head(kb_bf, i), head(k_bf, i)) for i in ch]
    qk = [_dot_nt(head(q_bf, i), head(k_bf, i)) for i in ch]
    decay = [jnp.exp((gcs[i] - gcum_r[b][h:h + 1, :]) + tri_bias) for i, (b, h) in enumerate(bh)]
    lower = [kk[i] * (decay[i] * off_diag) for i in ch]
    attn = [(qk[i] * decay[i]).astype(BF16) for i in ch]
    pw = [lower[i] * blk8 for i in ch]
    x_inv = [eye - pw[i] for i in ch]
    for _ in range(2):
        pwb = [p.astype(BF16) for p in pw]
        pw = [_dot(p, p) for p in pwb]
        x_inv = [x_inv[i] + _dot(x_inv[i].astype(BF16), pw[i].astype(BF16)) for i in ch]
    for ring in ring_f:
        xb = [x.astype(BF16) for x in x_inv]
        t1 = [_dot(xb[i], (lower[i] * ring).astype(BF16)).astype(BF16) for i in ch]
        x_inv = [x_inv[i] - _dot(t1[i], xb[i]) for i in ch]
    uw = [_dot(x_inv[i].astype(BF16), jnp.concatenate([head(rhs_v, i), head(rhs_k, i)], axis=1)) for i in ch]
    us = [m[:, :DN_V_DIM] for m in uw]
    wkb = [m[:, DN_V_DIM:].astype(BF16) for m in uw]
    qd = [head(qd_all, i) for i in ch]
    ke = [head(ke_all, i) for i in ch]

    st = [state_ref[h] for h in hs]
    outs = [[] for _ in hs]
    for n in range(nck):
        b, r0 = divmod(n * C, sb)
        r = slice(r0, r0 + C)
        ws = [_dot(jnp.concatenate([wkb[b * H + h][r], qd[b * H + h][r]], axis=0), st[h].astype(BF16)) for h in hs]
        v_nb = [(us[b * H + h][r] - ws[h][0:C]).astype(BF16) for h in hs]
        for h in hs:
            outs[h].append(ws[h][C:2 * C] + _dot(attn[b * H + h][r, r], v_nb[h]))
        cd = [jnp.exp(glast_r[b][h:h + 1, r0:r0 + 1]) for h in hs]
        st = [st[h] * cd[h] + _dot_tn(ke[b * H + h][r], v_nb[h]) for h in hs]
    for h in hs:
        state_ref[h] = st[h]
    o = jnp.concatenate([jnp.concatenate(outs[h], axis=0) for h in hs], axis=1)
    r_o = lax.rsqrt(head_sums(o * o) * (1.0 / DN_V_DIM) + RMS_EPS)
    o = o * expand3(r_o, r_o, r_o)[0] * ng_ref[...]
    out_ref[...] = (o * _silu(z_ref[...])).astype(out_ref.dtype)


def _gdn(qkv, z, misc, misct, conv_w, a_log, dt_bias, norm_g, *, B, S):
    T = B * S
    sb = min(GDN_SOLVE, S)
    tb = min(GDN_SOLVES_PER_STEP * sb, S)
    ns = S // tb
    H = DN_HEADS
    kern = functools.partial(_gdn_kernel, tb=tb, sb=sb)
    consts = _gdn_constants(sb)
    const = lambda *s: pl.BlockSpec(s, lambda b, i: (0,) * len(s))
    return pl.pallas_call(
        kern,
        grid=(B, ns),
        in_specs=[
            pl.BlockSpec((tb, DN_QKV_W), lambda b, i: (b * ns + i, 0)),
            pl.BlockSpec((tb, DN_VAL_W), lambda b, i: (b * ns + i, 0)),
            pl.BlockSpec((tb, MISC_W), lambda b, i: (b * ns + i, 0)),
            pl.BlockSpec((H, tb), lambda b, i: (MISC_A // H, b * ns + i)),
            const(DN_CONV, DN_QKV_W), const(H, 1), const(H, 1), const(1, H), const(1, H), const(1, DN_VAL_W),
        ] + [const(*c.shape) for c in consts],
        out_specs=pl.BlockSpec((tb, DN_VAL_W), lambda b, i: (b * ns + i, 0)),
        out_shape=jax.ShapeDtypeStruct((T, DN_VAL_W), BF16),
        scratch_shapes=[
            pltpu.VMEM((tb + SUBLANES, DN_QKV_W), F32),
            pltpu.VMEM((H, DN_K_DIM, DN_V_DIM), F32),
        ],
        compiler_params=pltpu.CompilerParams(
            dimension_semantics=("parallel", "arbitrary"), vmem_limit_bytes=VMEM_LIMIT),
        name="gdn",
    )(qkv, z, misc, misct, conv_w, a_log.reshape(H, 1), dt_bias.reshape(H, 1),
      a_log.reshape(1, H), dt_bias.reshape(1, H), jnp.tile(norm_g.reshape(1, DN_V_DIM), (1, H)), *consts)


def _out_proj_kernel(a_ref, d_ref, x_ref, wa_ref, wd_ref, g_ref, b_ref, h_ref, *, alpha):
    mix = _dot(a_ref[...].astype(BF16), wa_ref[...]) + _dot(d_ref[...], wd_ref[...])
    h_ref[...] = _layer_norm_rows(alpha * x_ref[...] + mix, g_ref[...], b_ref[...])


def _out_proj(a, d, x2, wa, wd, g, b, *, alpha, tm):
    T = x2.shape[0]
    const = lambda *s: pl.BlockSpec(s, lambda i: (0,) * len(s))
    return pl.pallas_call(
        functools.partial(_out_proj_kernel, alpha=alpha),
        grid=(T // tm,),
        in_specs=[
            pl.BlockSpec((tm, A_OUT_W), lambda i: (i, 0)),
            pl.BlockSpec((tm, DN_VAL_W), lambda i: (i, 0)),
            pl.BlockSpec((tm, D_MODEL), lambda i: (i, 0)),
            const(A_OUT_W, D_MODEL), const(DN_VAL_W, D_MODEL), const(1, D_MODEL), const(1, D_MODEL),
        ],
        out_specs=pl.BlockSpec((tm, D_MODEL), lambda i: (i, 0)),
        out_shape=jax.ShapeDtypeStruct((T, D_MODEL), F32),
        compiler_params=pltpu.CompilerParams(dimension_semantics=("parallel",), vmem_limit_bytes=VMEM_LIMIT),
        name="out_proj",
    )(a, d, x2, wa, wd, g, b)


def _ffn_kernel(h_ref, halo_ref, wg_ref, wu_ref, cw_ref, cb_ref, wd_ref, g_ref, b_ref, out_ref,
                gbuf_ref, acc_ref, *, alpha, tm, tiles_per_seq):
    i = pl.program_id(0)
    f = pl.program_id(1)
    nf = pl.num_programs(1)
    hb = h_ref[...].astype(BF16)
    wg = wg_ref[...]
    gate = _dot(hb, wg)
    up = _dot(hb, wu_ref[...])
    halo = _dot(halo_ref[...].astype(BF16), wg)
    halo = jnp.where(i % tiles_per_seq == 0, 0.0, halo)
    gbuf_ref[0:SUBLANES, :] = halo
    gbuf_ref[SUBLANES:SUBLANES + tm, :] = gate
    conv = jnp.zeros(gate.shape, F32)
    for k in range(FFN_CONV):
        off = SUBLANES - (FFN_CONV - 1) + k
        conv = conv + gbuf_ref[off:off + tm, :] * cw_ref[k:k + 1, :]
    act = (_silu(conv + cb_ref[...]) * up).astype(BF16)
    part = _dot(act, wd_ref[...])

    @pl.when(f == 0)
    def _():
        acc_ref[...] = part

    @pl.when(f > 0)
    def _():
        acc_ref[...] += part

    @pl.when(f == nf - 1)
    def _():
        out_ref[...] = _layer_norm_rows(alpha * h_ref[...] + acc_ref[...], g_ref[...], b_ref[...])


def _ffn(h, wg, wu, cw, cb, wd, g, b, *, alpha, S, tm, tf):
    T = h.shape[0]
    nf = D_FF // tf
    hs = tm // SUBLANES
    kern = functools.partial(_ffn_kernel, alpha=alpha, tm=tm, tiles_per_seq=S // tm)
    return pl.pallas_call(
        kern,
        grid=(T // tm, nf),
        in_specs=[
            pl.BlockSpec((tm, D_MODEL), lambda i, f: (i, 0)),
            pl.BlockSpec((SUBLANES, D_MODEL), lambda i, f: (jnp.maximum(i * hs - 1, 0), 0)),
            pl.BlockSpec((D_MODEL, tf), lambda i, f: (0, f)),
            pl.BlockSpec((D_MODEL, tf), lambda i, f: (0, f)),
            pl.BlockSpec((FFN_CONV, tf), lambda i, f: (0, f)),
            pl.BlockSpec((1, tf), lambda i, f: (0, f)),
            pl.BlockSpec((tf, D_MODEL), lambda i, f: (f, 0)),
            pl.BlockSpec((1, D_MODEL), lambda i, f: (0, 0)),
            pl.BlockSpec((1, D_MODEL), lambda i, f: (0, 0)),
        ],
        out_specs=pl.BlockSpec((tm, D_MODEL), lambda i, f: (i, 0)),
        out_shape=jax.ShapeDtypeStruct((T, D_MODEL), F32),
        scratch_shapes=[pltpu.VMEM((tm + SUBLANES, tf), F32), pltpu.VMEM((tm, D_MODEL), F32)],
        compiler_params=pltpu.CompilerParams(
            dimension_semantics=("parallel", "arbitrary"), vmem_limit_bytes=VMEM_LIMIT),
        name="ffn",
    )(h, h, wg, wu, cw, cb, wd, g, b)


def _regroup_w_in(w):
    offs = [0]
    for s in IN_SIZES:
        offs.append(offs[-1] + s)
    cq, ckv, kidx, widx, qkv, z, b, a = (w[:, offs[i]:offs[i + 1]] for i in range(8))
    pad = jnp.zeros((w.shape[0], MISC_W - (IDX_DIM + IDX_HEADS + 2 * DN_HEADS)), w.dtype)
    return jnp.concatenate([cq, ckv, kidx, widx, b, a, pad, qkv, z], axis=1).astype(BF16)


def _layer(x2, p, *, B, S, alpha):
    row = lambda v: v.reshape(1, -1)
    wc = _regroup_w_in(p["w_in"])
    wuq = p["w_uq"].reshape(A_Q_RANK, A_HEADS, A_QK_DIM).transpose(1, 0, 2).astype(BF16)
    wqi = p["w_qidx"].reshape(A_Q_RANK, IDX_HEADS, IDX_DIM).transpose(1, 0, 2).astype(BF16)
    qlat, qidx, ckv, ckvt, kidx, misc, misct, qkv, z = _in_proj(
        x2, wc, row(p["q_norm_g"]), wuq, p["w_uk"].astype(BF16), wqi, row(p["kv_norm_g"]),
        row(p["kidx_ln_g"]), row(p["kidx_ln_b"]), tm=min(512, S))
    a_out = _dsa(qidx, qlat, misct, kidx, ckv, ckvt, p["w_uv"].transpose(0, 2, 1).astype(BF16),
                 row(p["attn_out_g"]), B=B, S=S)
    d_out = _gdn(qkv, z, misc, misct, p["dn_conv_w"], p["dn_a_log"], p["dn_dt_bias"], p["dn_norm_g"],
                 B=B, S=S)
    w_out = p["w_out"].astype(BF16)
    h = _out_proj(a_out, d_out, x2, w_out[:A_OUT_W], w_out[A_OUT_W:], row(p["ln1_g"]), row(p["ln1_b"]),
                  alpha=alpha, tm=min(512, S))
    w_ffn = p["ffn_w_in"].astype(BF16)
    return _ffn(h, w_ffn[:, :D_FF], w_ffn[:, D_FF:], p["ffn_conv_w"], row(p["ffn_conv_b"]),
                p["ffn_w_down"].astype(BF16), row(p["ln2_g"]), row(p["ln2_b"]),
                alpha=alpha, S=S, tm=min(512, S), tf=D_FF // 2)


_PARAM_NAMES = ("w_in", "q_norm_g", "w_uq", "w_qidx", "kv_norm_g", "w_uk", "w_uv", "kidx_ln_g", "kidx_ln_b",
                "attn_out_g", "dn_conv_w", "dn_a_log", "dn_dt_bias", "dn_norm_g", "w_out", "ln1_g", "ln1_b",
                "ffn_w_in", "ffn_conv_w", "ffn_conv_b", "ffn_w_down", "ln2_g", "ln2_b")


def kernel(x, w_in, q_norm_g, w_uq, w_qidx, kv_norm_g, w_uk, w_uv, kidx_ln_g, kidx_ln_b, attn_out_g, dn_conv_w, dn_a_log, dn_dt_bias, dn_norm_g, w_out, ln1_g, ln1_b, ffn_w_in, ffn_conv_w, ffn_conv_b, ffn_w_down, ln2_g, ln2_b):
    params = (w_in, q_norm_g, w_uq, w_qidx, kv_norm_g, w_uk, w_uv, kidx_ln_g, kidx_ln_b, attn_out_g, dn_conv_w,
              dn_a_log, dn_dt_bias, dn_norm_g, w_out, ln1_g, ln1_b, ffn_w_in, ffn_conv_w, ffn_conv_b, ffn_w_down,
              ln2_g, ln2_b)
    B, S, D = x.shape
    depth = w_in.shape[0]
    alpha = (2 * depth) ** 0.25
    x2 = x.reshape(B * S, D)
    for l in range(depth):
        x2 = _layer(x2, {n: v[l] for n, v in zip(_PARAM_NAMES, params)}, B=B, S=S, alpha=alpha)
    return x2.reshape(B, S, D)
```

```python
import functools

import jax
import jax.numpy as jnp
import numpy as np
from jax import lax
from jax.experimental import pallas as pl
from jax.experimental.pallas import tpu as pltpu

F32 = jnp.float32
BF16 = jnp.bfloat16
I32 = jnp.int32
I16 = jnp.int16

D_MODEL = 1024
A_HEADS = 8
A_QK_DIM = 64
A_V_DIM = 64
A_Q_RANK = 256
A_KV_RANK = 128
IDX_HEADS = 8
IDX_DIM = 64
TOPK_MAX = 256
DN_HEADS = 8
DN_K_DIM = 64
DN_V_DIM = 64
DN_CONV = 4
D_FF = 2816
FFN_CONV = 3
RMS_EPS = 1e-6
LN_EPS = 1e-5

A_OUT_W = A_HEADS * A_V_DIM
DN_KEY_W = DN_HEADS * DN_K_DIM
DN_VAL_W = DN_HEADS * DN_V_DIM
DN_QKV_W = 2 * DN_KEY_W + DN_VAL_W
IN_SIZES = (A_Q_RANK, A_KV_RANK, IDX_DIM, IDX_HEADS, DN_QKV_W, DN_VAL_W, DN_HEADS, DN_HEADS)

MISC_W = 128
MISC_WIDX = IDX_DIM
MISC_B = IDX_DIM + IDX_HEADS
MISC_A = MISC_B + DN_HEADS
PROJ_W = A_Q_RANK + A_KV_RANK + MISC_W + DN_QKV_W + DN_VAL_W

LOG2E = 1.4426950408889634
SUBLANES = 8
PACK16 = 16
KVT_ROWS = A_KV_RANK + PACK16
HALF16 = 2 ** 15
INT_MIN = -(2 ** 31)
NEG_KEY = INT_MIN + 0x00800000
VMEM_LIMIT = 56 * 1024 * 1024

GDN_CHUNK = 64
GDN_SOLVE = 256
GDN_SOLVES_PER_STEP = 1


def _dot(a, b, precision=None):
    return jnp.dot(a, b, preferred_element_type=F32, precision=precision)


def _dot_nt(a, b, precision=None):
    return lax.dot_general(a, b, (((1,), (1,)), ((), ())), preferred_element_type=F32, precision=precision)


def _dot_tn(a, b, precision=None):
    return lax.dot_general(a, b, (((0,), (0,)), ((), ())), preferred_element_type=F32, precision=precision)


def _sigmoid(x):
    return 1.0 / (1.0 + jnp.exp(-x))


def _silu(x):
    return x * _sigmoid(x)


def _layer_norm_rows(v, g, b):
    mu = jnp.mean(v, axis=-1, keepdims=True)
    d = v - mu
    var = jnp.mean(d * d, axis=-1, keepdims=True)
    return d * lax.rsqrt(var + LN_EPS) * g + b


def _rms_norm_rows(v, g):
    return v * lax.rsqrt(jnp.mean(v * v, axis=-1, keepdims=True) + RMS_EPS) * g


def _in_proj_kernel(x_ref, w_ref, qg_ref, wuq_ref, wuk_ref, wqi_ref, kvg_ref, lng_ref, lnb_ref,
                    qlat_ref, qidx_ref, ckv_ref, ckvt_ref, kidx_ref, misc_ref, misct_ref, qkv_ref, z_ref):
    xb = x_ref[...].astype(BF16)
    proj = _dot(xb, w_ref[...])
    o = 0
    c_q = proj[:, o:o + A_Q_RANK]; o += A_Q_RANK
    c_kv = proj[:, o:o + A_KV_RANK]; o += A_KV_RANK
    misc = proj[:, o:o + MISC_W]; o += MISC_W
    qkv_ref[...] = proj[:, o:o + DN_QKV_W]; o += DN_QKV_W
    z_ref[...] = proj[:, o:o + DN_VAL_W]

    cqn = _rms_norm_rows(c_q, qg_ref[...]).astype(BF16)
    q = _dot(cqn, wuq_ref[...]).astype(BF16)
    qlat = _dot(q, wuk_ref[...]) * (A_QK_DIM ** -0.5 * LOG2E)
    qidx = _dot(cqn, wqi_ref[...])
    for h in range(A_HEADS):
        qlat_ref[h] = qlat[:, h * A_KV_RANK:(h + 1) * A_KV_RANK].astype(BF16)
        qidx_ref[h] = qidx[:, h * IDX_DIM:(h + 1) * IDX_DIM].astype(BF16)
    ckv = _rms_norm_rows(c_kv, kvg_ref[...])
    ckv_ref[...] = ckv.astype(BF16)
    ckvt_ref[0:A_KV_RANK, :] = ckv.T.astype(BF16)
    ckvt_ref[A_KV_RANK:KVT_ROWS, :] = jnp.ones((KVT_ROWS - A_KV_RANK, ckv.shape[0]), BF16)
    kidx_ref[...] = _layer_norm_rows(misc[:, :IDX_DIM], lng_ref[...], lnb_ref[...]).astype(BF16)
    misc_ref[...] = misc
    misct_ref[...] = misc.T


def _in_proj(x2, wc, qg, wuq, wuk, wqi, kvg, lng, lnb, *, tm):
    T = x2.shape[0]
    const = lambda *s: pl.BlockSpec(s, lambda i: (0,) * len(s))
    return pl.pallas_call(
        _in_proj_kernel,
        grid=(T // tm,),
        in_specs=[
            pl.BlockSpec((tm, D_MODEL), lambda i: (i, 0)),
            const(D_MODEL, PROJ_W), const(1, A_Q_RANK), const(A_Q_RANK, A_HEADS * A_QK_DIM),
            const(A_HEADS * A_QK_DIM, A_HEADS * A_KV_RANK), const(A_Q_RANK, IDX_HEADS * IDX_DIM),
            const(1, A_KV_RANK), const(1, IDX_DIM), const(1, IDX_DIM),
        ],
        out_specs=[
            pl.BlockSpec((A_HEADS, tm, A_KV_RANK), lambda i: (0, i, 0)),
            pl.BlockSpec((IDX_HEADS, tm, IDX_DIM), lambda i: (0, i, 0)),
            pl.BlockSpec((tm, A_KV_RANK), lambda i: (i, 0)),
            pl.BlockSpec((KVT_ROWS, tm), lambda i: (0, i)),
            pl.BlockSpec((tm, IDX_DIM), lambda i: (i, 0)),
            pl.BlockSpec((tm, MISC_W), lambda i: (i, 0)),
            pl.BlockSpec((MISC_W, tm), lambda i: (0, i)),
            pl.BlockSpec((tm, DN_QKV_W), lambda i: (i, 0)),
            pl.BlockSpec((tm, DN_VAL_W), lambda i: (i, 0)),
        ],
        out_shape=[
            jax.ShapeDtypeStruct((A_HEADS, T, A_KV_RANK), BF16),
            jax.ShapeDtypeStruct((IDX_HEADS, T, IDX_DIM), BF16),
            jax.ShapeDtypeStruct((T, A_KV_RANK), BF16),
            jax.ShapeDtypeStruct((KVT_ROWS, T), BF16),
            jax.ShapeDtypeStruct((T, IDX_DIM), BF16),
            jax.ShapeDtypeStruct((T, MISC_W), F32),
            jax.ShapeDtypeStruct((MISC_W, T), F32),
            jax.ShapeDtypeStruct((T, DN_QKV_W), F32),
            jax.ShapeDtypeStruct((T, DN_VAL_W), F32),
        ],
        compiler_params=pltpu.CompilerParams(dimension_semantics=("parallel",), vmem_limit_bytes=VMEM_LIMIT),
        name="in_proj",
    )(x2, wc, qg, wuq, wuk, wqi, kvg, lng, lnb)


def _dsa_kernel(qidx_ref, qlat_ref, wt_ref, kidx_ref, ckv_ref, ckvt_ref, wuvt_ref, g_ref, out_ref,
                keys_ref, hi_ref, lo_ref, acc_ref, stage_ref, bias_ref, *, qb, kb, rb, slab, topk, idx_bits):
    H = A_HEADS
    j = pl.program_id(1)
    q0 = j * qb
    nkeys = (j + 1) * qb
    nrblk = (nkeys + rb - 1) // rb
    ups = rb // kb
    nchunk = nrblk * ups
    w = wt_ref[...] * (IDX_HEADS ** -0.5 * IDX_DIM ** -0.5)
    row_minus_lane = (lax.broadcasted_iota(I32, (slab, qb), 0) - lax.broadcasted_iota(I32, (slab, qb), 1))

    def skewed(nparts, consume, produce, carry):
        def chunk(c, k, carry):
            nxt = jnp.minimum(c + 1, nchunk - 1)
            if ups > 1:
                for i in range(nparts):
                    carry = produce(nxt, (k + 1) % ups, i, consume(c, k, i, carry))
                return carry
            for i in range(nparts):
                carry = consume(c, k, i, carry)
            for i in range(nparts):
                carry = produce(nxt, k, i, carry)
            return carry

        def step(i, carry):
            for k in range(ups):
                carry = chunk(i * ups + k, k, carry)
            return carry

        for i in range(nparts):
            carry = produce(0, 0, i, carry)
        return lax.fori_loop(0, nrblk, step, carry)

    nslab = kb // slab
    hps = H // nslab

    def score_matmuls(c, slot, i, carry):
        kc = kidx_ref[pl.ds(pl.multiple_of(c * kb, kb), kb), :]
        for h in range(i * hps, (i + 1) * hps):
            stage_ref[slot, h] = _dot_nt(kc, qidx_ref[h])
        return carry

    def score_keys(c, slot, i, carry):
        r0 = pl.multiple_of(c * kb, kb) + i * slab
        s = jnp.zeros((slab, qb), F32)
        for h in range(H):
            s = s + jnp.maximum(stage_ref[slot, h, i * slab:(i + 1) * slab, :], 0.0) * w[h:h + 1, :]
        bits = lax.bitcast_convert_type(s, I32)
        key = jnp.where(bits < 0, INT_MIN - bits, bits)
        key = jnp.where(row_minus_lane <= q0 - r0, key, NEG_KEY)
        keys_ref[pl.ds(r0, slab), :] = key
        hi_ref[pl.ds(r0, slab), :] = jnp.right_shift(key, 16).astype(I16)
        lo_ref[pl.ds(r0, slab), :] = ((key & 0xFFFF) - HALF16).astype(I16)
        return carry

    skewed(nslab, score_keys, score_matmuls, 0)

    nacc = 4

    def count_rows(pred):
        def body(r, acc):
            r0 = pl.multiple_of(r * rb, rb)
            k = keys_ref[pl.ds(r0, rb), :]
            m = jnp.where(pred(k, r0), 1, 0).astype(I32)
            return acc + m.reshape(nacc, rb // (nacc * SUBLANES), SUBLANES, qb).sum(axis=1)
        acc = lax.fori_loop(0, nrblk, body, jnp.zeros((nacc, SUBLANES, qb), I32))
        return acc.sum(axis=0).sum(axis=0, keepdims=True)

    def count16(ref, cand):
        def body(r, accs):
            r0 = pl.multiple_of(r * rb, rb)
            m = jnp.where(ref[pl.ds(r0, rb), :] >= cand, jnp.int16(1), jnp.int16(0))
            accs = list(accs)
            for i in range(rb // PACK16):
                accs[i % nacc] = accs[i % nacc] + m[i * PACK16:(i + 1) * PACK16, :]
            return tuple(accs)
        accs = lax.fori_loop(0, nrblk, body, tuple(jnp.zeros((PACK16, qb), I16) for _ in range(nacc)))
        tot = accs[0].astype(I32)
        for a in accs[1:]:
            tot = tot + a.astype(I32)
        return tot.sum(axis=0, keepdims=True)

    def bisect16(ref, target, cnt_all):
        def body(p, carry):
            t_u, cnt_ok, cnt_rej = carry
            cand_u = t_u | jnp.left_shift(jnp.int32(1), 15 - p)
            cnt = count16(ref, (cand_u - HALF16).astype(I16))
            ok = cnt >= target
            return jnp.where(ok, cand_u, t_u), jnp.where(ok, cnt, cnt_ok), jnp.where(ok, cnt_rej, cnt)
        return lax.fori_loop(0, 16, body, (jnp.zeros((1, qb), I32), cnt_all, jnp.zeros((1, qb), I32)))

    hi_u, cnt_ge_hi, cnt_gt_hi = bisect16(hi_ref, topk, jnp.full((1, qb), nrblk * rb, I32))
    hi_t = (hi_u - HALF16).astype(I16)

    def mask_lo(r, carry):
        r0 = pl.multiple_of(r * rb, rb)
        lo_ref[pl.ds(r0, rb), :] = jnp.where(hi_ref[pl.ds(r0, rb), :] == hi_t, lo_ref[pl.ds(r0, rb), :],
                                             jnp.int16(-HALF16))
        return carry

    lax.fori_loop(0, nrblk, mask_lo, 0)
    lo_u, cnt_ge_lo, _ = bisect16(lo_ref, topk - cnt_gt_hi, cnt_ge_hi - cnt_gt_hi)
    t = jnp.left_shift(hi_u - HALF16, 16) + lo_u
    cnt_t = cnt_gt_hi + cnt_ge_lo

    excess = jnp.where(t > NEG_KEY, cnt_t - topk, 0)

    @pl.when(jnp.max(excess) > 0)
    def _():
        keep = topk - count_rows(lambda k, r0: k >= t + 1)

        def idx_body(p, pos):
            cand = pos + jnp.left_shift(jnp.int32(1), idx_bits - 1 - p)
            c = count_rows(lambda k, r0: (k == t) & (
                r0 + lax.broadcasted_iota(I32, (rb, qb), 0) < cand))
            return jnp.where(c < keep, cand, pos)

        pos = lax.fori_loop(0, idx_bits, idx_body, jnp.zeros((1, qb), I32))

        def demote(r, carry):
            r0 = pl.multiple_of(r * rb, rb)
            k = keys_ref[pl.ds(r0, rb), :]
            row = r0 + lax.broadcasted_iota(I32, (rb, qb), 0)
            keys_ref[pl.ds(r0, rb), :] = jnp.where((k == t) & (row > pos), NEG_KEY, k)
            return carry

        lax.fori_loop(0, nrblk, demote, 0)

    thr = jnp.maximum(t, NEG_KEY + 1)

    acc_ref[...] = jnp.zeros(acc_ref.shape, F32)

    def put(carry, h, item):
        return carry[:h] + (item,) + carry[h + 1:]

    def attn_logits(c, slot, h, carry):
        r0 = pl.multiple_of(c * kb, kb)
        if h == 0:
            bias_ref[slot] = jnp.where(keys_ref[pl.ds(r0, kb), :] >= thr, 0.0, -jnp.inf)
        m_prev = carry[h][1]
        lg = _dot_nt(ckv_ref[pl.ds(r0, kb), :], qlat_ref[h]) + bias_ref[slot]
        stage_ref[slot, h] = lg
        return put(carry, h, (m_prev, jnp.maximum(m_prev, jnp.max(lg, axis=0, keepdims=True))))

    def attn_values(c, slot, h, carry):
        m_old, m_new = carry[h]
        kvt = ckvt_ref[:, pl.ds(pl.multiple_of(c * kb, kb), kb)]
        p = jnp.exp2(stage_ref[slot, h] - m_new).astype(BF16)
        acc_ref[h] = acc_ref[h] * jnp.exp2(m_old - m_new) + _dot(kvt, p)
        return carry

    m_init = jnp.full((1, qb), -1e30, F32)
    skewed(H, attn_values, attn_logits, ((m_init, m_init),) * H)

    out_t = jnp.concatenate(
        [_dot(wuvt_ref[h], (acc_ref[h, 0:A_KV_RANK, :] / acc_ref[h, A_KV_RANK:A_KV_RANK + 1, :]).astype(BF16))
         for h in range(H)], axis=0)
    out_ref[...] = _rms_norm_rows(out_t.T, g_ref[...])


def _dsa(qidx, qlat, misct, kidx, ckv, ckvt, wuvt, g, *, B, S):
    T = B * S
    qb = min(256, S)
    nq = S // qb
    kb = qb
    rb = 2 * qb if nq % 2 == 0 else qb
    topk = min(TOPK_MAX, S // 4)
    idx_bits = max(1, (S - 1).bit_length())
    slab = min(64, kb)
    kern = functools.partial(_dsa_kernel, qb=qb, kb=kb, rb=rb, slab=slab, topk=topk, idx_bits=idx_bits)
    return pl.pallas_call(
        kern,
        grid=(B, nq),
        in_specs=[
            pl.BlockSpec((IDX_HEADS, qb, IDX_DIM), lambda b, j: (0, b * nq + j, 0)),
            pl.BlockSpec((A_HEADS, qb, A_KV_RANK), lambda b, j: (0, b * nq + j, 0)),
            pl.BlockSpec((IDX_HEADS, qb), lambda b, j: (MISC_WIDX // IDX_HEADS, b * nq + j)),
            pl.BlockSpec((S, IDX_DIM), lambda b, j: (b, 0)),
            pl.BlockSpec((S, A_KV_RANK), lambda b, j: (b, 0)),
            pl.BlockSpec((KVT_ROWS, S), lambda b, j: (0, b)),
            pl.BlockSpec((A_HEADS, A_V_DIM, A_KV_RANK), lambda b, j: (0, 0, 0)),
            pl.BlockSpec((1, A_OUT_W), lambda b, j: (0, 0)),
        ],
        out_specs=pl.BlockSpec((qb, A_OUT_W), lambda b, j: (b * nq + j, 0)),
        out_shape=jax.ShapeDtypeStruct((T, A_OUT_W), F32),
        scratch_shapes=[
            pltpu.VMEM((S, qb), I32),
            pltpu.VMEM((S, qb), I16),
            pltpu.VMEM((S, qb), I16),
            pltpu.VMEM((A_HEADS, KVT_ROWS, qb), F32),
            pltpu.VMEM((rb // kb, A_HEADS, kb, qb), F32),
            pltpu.VMEM((rb // kb, kb, qb), F32),
        ],
        compiler_params=pltpu.CompilerParams(
            dimension_semantics=("parallel", "arbitrary"), vmem_limit_bytes=VMEM_LIMIT),
        name="dsa",
    )(qidx, qlat, misct, kidx, ckv, ckvt, wuvt, g)


def _gdn_constants(sb):
    r, c = np.arange(sb)[:, None], np.arange(sb)[None, :]
    same = (r // GDN_CHUNK) == (c // GDN_CHUNK)
    tri = same & (c <= r)
    diag = [((r // b) == (c // b)).astype(np.float32) for b in (8, 16, 32)] + [same.astype(np.float32)]
    eye = (r == c).astype(np.float32)
    mats = [diag[0]] + [diag[i] - diag[i - 1] for i in (1, 2, 3)] + [eye, 1.0 - eye, np.where(tri, 0.0, -np.inf)]
    mask_c = np.concatenate([tri, same], axis=0).astype(np.float32)
    mask_r = np.concatenate([same & (r <= c), same], axis=1).astype(np.float32)
    spread = (np.arange(DN_KEY_W)[None, :] // DN_K_DIM == np.arange(DN_HEADS)[:, None]).astype(np.float32)
    spread3 = np.zeros((3, 3, 3, DN_HEADS, DN_KEY_W), np.float32)
    for j in range(3):
        spread3[j, :, j] = spread
    return (jnp.asarray(np.stack(mats), F32), jnp.asarray(mask_c, BF16), jnp.asarray(mask_r, BF16),
            jnp.asarray(spread3.reshape(3, 9 * DN_HEADS, DN_KEY_W), BF16), jnp.asarray(np.tile(spread, (1, 2)), BF16))


def _gdn_kernel(qkv_ref, z_ref, misc_ref, at_ref, cw_ref, alog_c_ref, dtb_c_ref, alog_r_ref, dtb_r_ref,
                ng_ref, fm_ref, mc_ref, mr_ref, sr_ref, sc_ref, out_ref, xbuf_ref, state_ref, *, tb, sb):
    C = GDN_CHUNK
    nck = tb // C
    H = DN_HEADS
    step = pl.program_id(1)

    @pl.when(step == 0)
    def _():
        xbuf_ref[0:SUBLANES, :] = jnp.zeros((SUBLANES, DN_QKV_W), F32)
        state_ref[...] = jnp.zeros(state_ref.shape, F32)

    xbuf_ref[SUBLANES:SUBLANES + tb, :] = qkv_ref[...]
    xe = xbuf_ref[...]
    y = xe[SUBLANES:] * cw_ref[DN_CONV - 1:DN_CONV, :]
    for d in range(1, DN_CONV):
        y = y + pltpu.roll(xe, d, 0)[SUBLANES:] * cw_ref[DN_CONV - 1 - d:DN_CONV - d, :]
    xbuf_ref[0:SUBLANES, :] = xe[tb:tb + SUBLANES]
    y = _silu(y)

    def softplus(v):
        return jnp.maximum(v, 0.0) + jnp.log(1.0 + jnp.exp(-jnp.abs(v)))

    misc = misc_ref[...]
    beta_c = _sigmoid(misc[:, MISC_B:MISC_B + H])
    g_c = -jnp.exp(alog_r_ref[...]) * softplus(misc[:, MISC_A:MISC_A + H] + dtb_r_ref[...])
    g_r = -jnp.exp(alog_c_ref[...]) * softplus(at_ref[...] + dtb_c_ref[...])

    rows = [slice(b * sb, (b + 1) * sb) for b in range(tb // sb)]
    mask_c = mc_ref[...]
    mask_r = mr_ref[...]

    def pieces(v, axis, n=3):
        out, rest = [], v
        for _ in range(n - 1):
            out.append(rest.astype(BF16))
            rest = rest - out[-1].astype(F32)
        return jnp.concatenate(out + [rest.astype(BF16)], axis=axis)

    sums_c = [_dot(mask_c, pieces(g_c[r], 1)) for r in rows]
    sums_c = [s[:, 0:H] + s[:, H:2 * H] + s[:, 2 * H:3 * H] for s in sums_c]
    sums_r = [_dot(pieces(g_r[:, r], 0), mask_r) for r in rows]
    sums_r = [s[0:H] + s[H:2 * H] + s[2 * H:3 * H] for s in sums_r]
    gcum_c = [s[0:sb] for s in sums_c]
    glast_c = [s[sb:2 * sb] for s in sums_c]
    gcum_r = [s[:, 0:sb] for s in sums_r]
    glast_r = [s[:, sb:2 * sb] for s in sums_r]
    blk8, ring_f, eye, off_diag, tri_bias = fm_ref[0], [fm_ref[1], fm_ref[2], fm_ref[3]], fm_ref[4], fm_ref[5], fm_ref[6]

    hs = range(H)
    bh = [(b, h) for b in range(len(rows)) for h in hs]
    ch = range(len(bh))
    def expand3(a, b, c):
        p = pieces(jnp.concatenate([a, b, c], axis=1), 1)
        return [_dot(p, sr_ref[j]) for j in range(3)]

    head_sums = lambda v: _dot_nt(pieces(v, 1, 2), sc_ref[...])
    yq, yk, yv = y[:, :DN_KEY_W], y[:, DN_KEY_W:2 * DN_KEY_W], y[:, 2 * DN_KEY_W:]
    gc_all = jnp.concatenate(gcum_c, axis=0)
    eg_c = jnp.exp(gc_all)
    rq_x, rk_x, beta_x = expand3(lax.rsqrt(head_sums(yq * yq) + RMS_EPS) * (DN_K_DIM ** -0.5),
                                 lax.rsqrt(head_sums(yk * yk) + RMS_EPS), beta_c)
    eg_x, egl_x, _ = expand3(eg_c, jnp.exp(jnp.concatenate(glast_c, axis=0) - gc_all), eg_c)
    q_n, k_n = yq * rq_x, yk * rk_x
    kb_all = k_n * beta_x
    k_bf, kb_bf, q_bf = k_n.astype(BF16), kb_all.astype(BF16), q_n.astype(BF16)
    rhs_v, rhs_k = (yv * beta_x).astype(BF16), (kb_all * eg_x).astype(BF16)
    qd_all = (q_n * eg_x).astype(BF16)
    ke_all = (k_n * egl_x).astype(BF16)
    head = lambda a, i: a[rows[bh[i][0]], bh[i][1] * DN_K_DIM:(bh[i][1] + 1) * DN_K_DIM]
    gcs = [gcum_c[b][:, h:h + 1] for b, h in bh]
    kk = [_dot_nt(head(kb_bf, i), head(k_bf, i)) for i in ch]
    qk = [_dot_nt(head(q_bf, i), head(k_bf, i)) for i in ch]
    decay = [jnp.exp((gcs[i] - gcum_r[b][h:h + 1, :]) + tri_bias) for i, (b, h) in enumerate(bh)]
    lower = [kk[i] * (decay[i] * off_diag) for i in ch]
    attn = [(qk[i] * decay[i]).astype(BF16) for i in ch]
    pw = [lower[i] * blk8 for i in ch]
    x_inv = [eye - pw[i] for i in ch]
    for _ in range(2):
        pwb = [p.astype(BF16) for p in pw]
        pw = [_dot(p, p) for p in pwb]
        x_inv = [x_inv[i] + _dot(x_inv[i].astype(BF16), pw[i].astype(BF16)) for i in ch]
    for ring in ring_f:
        xb = [x.astype(BF16) for x in x_inv]
        t1 = [_dot(xb[i], (lower[i] * ring).astype(BF16)).astype(BF16) for i in ch]
        x_inv = [x_inv[i] - _dot(t1[i], xb[i]) for i in ch]
    uw = [_dot(x_inv[i].astype(BF16), jnp.concatenate([head(rhs_v, i), head(rhs_k, i)], axis=1)) for i in ch]
    us = [m[:, :DN_V_DIM] for m in uw]
    wkb = [m[:, DN_V_DIM:].astype(BF16) for m in uw]
    qd = [head(qd_all, i) for i in ch]
    ke = [head(ke_all, i) for i in ch]

    st = [state_ref[h] for h in hs]
    outs = [[] for _ in hs]
    for n in range(nck):
        b, r0 = divmod(n * C, sb)
        r = slice(r0, r0 + C)
        ws = [_dot(jnp.concatenate([wkb[b * H + h][r], qd[b * H + h][r]], axis=0), st[h].astype(BF16)) for h in hs]
        v_nb = [(us[b * H + h][r] - ws[h][0:C]).astype(BF16) for h in hs]
        for h in hs:
            outs[h].append(ws[h][C:2 * C] + _dot(attn[b * H + h][r, r], v_nb[h]))
        cd = [jnp.exp(glast_r[b][h:h + 1, r0:r0 + 1]) for h in hs]
        st = [st[h] * cd[h] + _dot_tn(ke[b * H + h][r], v_nb[h]) for h in hs]
    for h in hs:
        state_ref[h] = st[h]
    o = jnp.concatenate([jnp.concatenate(outs[h], axis=0) for h in hs], axis=1)
    r_o = lax.rsqrt(head_sums(o * o) * (1.0 / DN_V_DIM) + RMS_EPS)
    o = o * expand3(r_o, r_o, r_o)[0] * ng_ref[...]
    out_ref[...] = (o * _silu(z_ref[...])).astype(out_ref.dtype)


def _gdn(qkv, z, misc, misct, conv_w, a_log, dt_bias, norm_g, *, B, S):
    T = B * S
    sb = min(GDN_SOLVE, S)
    tb = min(GDN_SOLVES_PER_STEP * sb, S)
    ns = S // tb
    H = DN_HEADS
    kern = functools.partial(_gdn_kernel, tb=tb, sb=sb)
    consts = _gdn_constants(sb)
    const = lambda *s: pl.BlockSpec(s, lambda b, i: (0,) * len(s))
    return pl.pallas_call(
        kern,
        grid=(B, ns),
        in_specs=[
            pl.BlockSpec((tb, DN_QKV_W), lambda b, i: (b * ns + i, 0)),
            pl.BlockSpec((tb, DN_VAL_W), lambda b, i: (b * ns + i, 0)),
            pl.BlockSpec((tb, MISC_W), lambda b, i: (b * ns + i, 0)),
            pl.BlockSpec((H, tb), lambda b, i: (MISC_A // H, b * ns + i)),
            const(DN_CONV, DN_QKV_W), const(H, 1), const(H, 1), const(1, H), const(1, H), const(1, DN_VAL_W),
        ] + [const(*c.shape) for c in consts],
        out_specs=pl.BlockSpec((tb, DN_VAL_W), lambda b, i: (b * ns + i, 0)),
        out_shape=jax.ShapeDtypeStruct((T, DN_VAL_W), BF16),
        scratch_shapes=[
            pltpu.VMEM((tb + SUBLANES, DN_QKV_W), F32),
            pltpu.VMEM((H, DN_K_DIM, DN_V_DIM), F32),
        ],
        compiler_params=pltpu.CompilerParams(
            dimension_semantics=("parallel", "arbitrary"), vmem_limit_bytes=VMEM_LIMIT),
        name="gdn",
    )(qkv, z, misc, misct, conv_w, a_log.reshape(H, 1), dt_bias.reshape(H, 1),
      a_log.reshape(1, H), dt_bias.reshape(1, H), jnp.tile(norm_g.reshape(1, DN_V_DIM), (1, H)), *consts)


def _out_proj_kernel(a_ref, d_ref, x_ref, wa_ref, wd_ref, g_ref, b_ref, h_ref, *, alpha):
    mix = _dot(a_ref[...].astype(BF16), wa_ref[...]) + _dot(d_ref[...], wd_ref[...])
    h_ref[...] = _layer_norm_rows(alpha * x_ref[...] + mix, g_ref[...], b_ref[...])


def _out_proj(a, d, x2, wa, wd, g, b, *, alpha, tm):
    T = x2.shape[0]
    const = lambda *s: pl.BlockSpec(s, lambda i: (0,) * len(s))
    return pl.pallas_call(
        functools.partial(_out_proj_kernel, alpha=alpha),
        grid=(T // tm,),
        in_specs=[
            pl.BlockSpec((tm, A_OUT_W), lambda i: (i, 0)),
            pl.BlockSpec((tm, DN_VAL_W), lambda i: (i, 0)),
            pl.BlockSpec((tm, D_MODEL), lambda i: (i, 0)),
            const(A_OUT_W, D_MODEL), const(DN_VAL_W, D_MODEL), const(1, D_MODEL), const(1, D_MODEL),
        ],
        out_specs=pl.BlockSpec((tm, D_MODEL), lambda i: (i, 0)),
        out_shape=jax.ShapeDtypeStruct((T, D_MODEL), F32),
        compiler_params=pltpu.CompilerParams(dimension_semantics=("parallel",), vmem_limit_bytes=VMEM_LIMIT),
        name="out_proj",
    )(a, d, x2, wa, wd, g, b)


def _ffn_kernel(h_ref, halo_ref, wg_ref, wu_ref, cw_ref, cb_ref, wd_ref, g_ref, b_ref, out_ref,
                acc_ref, *, alpha, tiles_per_seq):
    i = pl.program_id(0)
    f = pl.program_id(1)
    nf = pl.num_programs(1)
    hb = h_ref[...].astype(BF16)
    wg = wg_ref[...]
    gate = _dot(hb, wg)
    up = _dot(hb, wu_ref[...])
    halo = _dot(halo_ref[...].astype(BF16), wg)
    halo = jnp.where(i % tiles_per_seq == 0, 0.0, halo)
    ge = jnp.concatenate([halo, gate], axis=0)
    conv = gate * cw_ref[FFN_CONV - 1:FFN_CONV, :]
    for d in range(1, FFN_CONV):
        conv = conv + pltpu.roll(ge, d, 0)[SUBLANES:] * cw_ref[FFN_CONV - 1 - d:FFN_CONV - d, :]
    act = (_silu(conv + cb_ref[...]) * up).astype(BF16)
    part = _dot(act, wd_ref[...])

    @pl.when(f == 0)
    def _():
        acc_ref[...] = part

    @pl.when(f > 0)
    def _():
        acc_ref[...] += part

    @pl.when(f == nf - 1)
    def _():
        out_ref[...] = _layer_norm_rows(alpha * h_ref[...] + acc_ref[...], g_ref[...], b_ref[...])


def _ffn(h, wg, wu, cw, cb, wd, g, b, *, alpha, S, tm, tf):
    T = h.shape[0]
    nf = D_FF // tf
    hs = tm // SUBLANES
    kern = functools.partial(_ffn_kernel, alpha=alpha, tiles_per_seq=S // tm)
    return pl.pallas_call(
        kern,
        grid=(T // tm, nf),
        in_specs=[
            pl.BlockSpec((tm, D_MODEL), lambda i, f: (i, 0)),
            pl.BlockSpec((SUBLANES, D_MODEL), lambda i, f: (jnp.maximum(i * hs - 1, 0), 0)),
            pl.BlockSpec((D_MODEL, tf), lambda i, f: (0, f)),
            pl.BlockSpec((D_MODEL, tf), lambda i, f: (0, f)),
            pl.BlockSpec((FFN_CONV, tf), lambda i, f: (0, f)),
            pl.BlockSpec((1, tf), lambda i, f: (0, f)),
            pl.BlockSpec((tf, D_MODEL), lambda i, f: (f, 0)),
            pl.BlockSpec((1, D_MODEL), lambda i, f: (0, 0)),
            pl.BlockSpec((1, D_MODEL), lambda i, f: (0, 0)),
        ],
        out_specs=pl.BlockSpec((tm, D_MODEL), lambda i, f: (i, 0)),
        out_shape=jax.ShapeDtypeStruct((T, D_MODEL), F32),
        scratch_shapes=[pltpu.VMEM((tm, D_MODEL), F32)],
        compiler_params=pltpu.CompilerParams(
            dimension_semantics=("parallel", "arbitrary"), vmem_limit_bytes=VMEM_LIMIT),
        name="ffn",
    )(h, h, wg, wu, cw, cb, wd, g, b)


def _regroup_w_in(w):
    offs = [0]
    for s in IN_SIZES:
        offs.append(offs[-1] + s)
    cq, ckv, kidx, widx, qkv, z, b, a = (w[:, offs[i]:offs[i + 1]] for i in range(8))
    pad = jnp.zeros((w.shape[0], MISC_W - (IDX_DIM + IDX_HEADS + 2 * DN_HEADS)), w.dtype)
    return jnp.concatenate([cq, ckv, kidx, widx, b, a, pad, qkv, z], axis=1).astype(BF16)


def _layer(x2, p, *, B, S, alpha):
    row = lambda v: v.reshape(1, -1)
    wc = _regroup_w_in(p["w_in"])
    wuk_bd = jnp.einsum("hdc,hg->hdgc", p["w_uk"], jnp.eye(A_HEADS, dtype=F32)).reshape(
        A_HEADS * A_QK_DIM, A_HEADS * A_KV_RANK)
    qlat, qidx, ckv, ckvt, kidx, misc, misct, qkv, z = _in_proj(
        x2, wc, row(p["q_norm_g"]), p["w_uq"].astype(BF16), wuk_bd.astype(BF16), p["w_qidx"].astype(BF16),
        row(p["kv_norm_g"]),
        row(p["kidx_ln_g"]), row(p["kidx_ln_b"]), tm=min(512, S))
    a_out = _dsa(qidx, qlat, misct, kidx, ckv, ckvt, p["w_uv"].transpose(0, 2, 1).astype(BF16),
                 row(p["attn_out_g"]), B=B, S=S)
    d_out = _gdn(qkv, z, misc, misct, p["dn_conv_w"], p["dn_a_log"], p["dn_dt_bias"], p["dn_norm_g"],
                 B=B, S=S)
    w_out = p["w_out"].astype(BF16)
    h = _out_proj(a_out, d_out, x2, w_out[:A_OUT_W], w_out[A_OUT_W:], row(p["ln1_g"]), row(p["ln1_b"]),
                  alpha=alpha, tm=min(512, S))
    w_ffn = p["ffn_w_in"].astype(BF16)
    return _ffn(h, w_ffn[:, :D_FF], w_ffn[:, D_FF:], p["ffn_conv_w"], row(p["ffn_conv_b"]),
                p["ffn_w_down"].astype(BF16), row(p["ln2_g"]), row(p["ln2_b"]),
                alpha=alpha, S=S, tm=min(512, S), tf=D_FF // 2)


_PARAM_NAMES = ("w_in", "q_norm_g", "w_uq", "w_qidx", "kv_norm_g", "w_uk", "w_uv", "kidx_ln_g", "kidx_ln_b",
                "attn_out_g", "dn_conv_w", "dn_a_log", "dn_dt_bias", "dn_norm_g", "w_out", "ln1_g", "ln1_b",
                "ffn_w_in", "ffn_conv_w", "ffn_conv_b", "ffn_w_down", "ln2_g", "ln2_b")


def kernel(x, w_in, q_norm_g, w_uq, w_qidx, kv_norm_g, w_uk, w_uv, kidx_ln_g, kidx_ln_b, attn_out_g, dn_conv_w, dn_a_log, dn_dt_bias, dn_norm_g, w_out, ln1_g, ln1_b, ffn_w_in, ffn_conv_w, ffn_conv_b, ffn_w_down, ln2_g, ln2_b):
    params = (w_in, q_norm_g, w_uq, w_qidx, kv_norm_g, w_uk, w_uv, kidx_ln_g, kidx_ln_b, attn_out_g, dn_conv_w,
              dn_a_log, dn_dt_bias, dn_norm_g, w_out, ln1_g, ln1_b, ffn_w_in, ffn_conv_w, ffn_conv_b, ffn_w_down,
              ln2_g, ln2_b)
    B, S, D = x.shape
    depth = w_in.shape[0]
    alpha = (2 * depth) ** 0.25
    x2 = x.reshape(B * S, D)
    for l in range(depth):
        x2 = _layer(x2, {n: v[l] for n, v in zip(_PARAM_NAMES, params)}, B=B, S=S, alpha=alpha)
    return x2.reshape(B, S, D)
```

```python
import functools

import jax
import jax.numpy as jnp
import numpy as np
from jax import lax
from jax.experimental import pallas as pl
from jax.experimental.pallas import tpu as pltpu

F32 = jnp.float32
BF16 = jnp.bfloat16
I32 = jnp.int32
I16 = jnp.int16

D_MODEL = 1024
A_HEADS = 8
A_QK_DIM = 64
A_V_DIM = 64
A_Q_RANK = 256
A_KV_RANK = 128
IDX_HEADS = 8
IDX_DIM = 64
TOPK_MAX = 256
DN_HEADS = 8
DN_K_DIM = 64
DN_V_DIM = 64
DN_CONV = 4
D_FF = 2816
FFN_CONV = 3
RMS_EPS = 1e-6
LN_EPS = 1e-5

A_OUT_W = A_HEADS * A_V_DIM
DN_KEY_W = DN_HEADS * DN_K_DIM
DN_VAL_W = DN_HEADS * DN_V_DIM
DN_QKV_W = 2 * DN_KEY_W + DN_VAL_W
IN_SIZES = (A_Q_RANK, A_KV_RANK, IDX_DIM, IDX_HEADS, DN_QKV_W, DN_VAL_W, DN_HEADS, DN_HEADS)

MISC_W = 128
MISC_WIDX = IDX_DIM
MISC_B = IDX_DIM + IDX_HEADS
MISC_A = MISC_B + DN_HEADS
PROJ_W = A_Q_RANK + A_KV_RANK + MISC_W + DN_QKV_W + DN_VAL_W

LOG2E = 1.4426950408889634
SUBLANES = 8
PACK16 = 16
KVT_ROWS = A_KV_RANK + PACK16
HALF16 = 2 ** 15
INT_MIN = -(2 ** 31)
NEG_KEY = INT_MIN + 0x00800000
VMEM_LIMIT = 56 * 1024 * 1024

ROW_GROUP = 256
GDN_CHUNK = 64
GDN_SOLVE = 256
GDN_SOLVES_PER_STEP = 1


def _dot(a, b, precision=None):
    return jnp.dot(a, b, preferred_element_type=F32, precision=precision)


def _dot_nt(a, b, precision=None):
    return lax.dot_general(a, b, (((1,), (1,)), ((), ())), preferred_element_type=F32, precision=precision)


def _dot_tn(a, b, precision=None):
    return lax.dot_general(a, b, (((0,), (0,)), ((), ())), preferred_element_type=F32, precision=precision)


def _sigmoid(x):
    return 1.0 / (1.0 + jnp.exp(-x))


def _silu(x):
    return x * _sigmoid(x)


def _layer_norm_rows(v, g, b):
    mu = jnp.mean(v, axis=-1, keepdims=True)
    d = v - mu
    var = jnp.mean(d * d, axis=-1, keepdims=True)
    return d * lax.rsqrt(var + LN_EPS) * g + b


def _rms_norm_rows(v, g):
    return v * lax.rsqrt(jnp.mean(v * v, axis=-1, keepdims=True) + RMS_EPS) * g


def _in_proj_kernel(x_ref, w_ref, qg_ref, wuq_ref, wuk_ref, wqi_ref, kvg_ref, lng_ref, lnb_ref,
                    qlat_ref, qidx_ref, ckv_ref, ckvt_ref, kidx_ref, misc_ref, misct_ref, qkv_ref, z_ref):
    xb = x_ref[...].astype(BF16)
    proj = _dot(xb, w_ref[...])
    o = 0
    c_q = proj[:, o:o + A_Q_RANK]; o += A_Q_RANK
    c_kv = proj[:, o:o + A_KV_RANK]; o += A_KV_RANK
    misc = proj[:, o:o + MISC_W]; o += MISC_W
    qkv_ref[...] = proj[:, o:o + DN_QKV_W]; o += DN_QKV_W
    z_ref[...] = proj[:, o:o + DN_VAL_W]

    cqn = _rms_norm_rows(c_q, qg_ref[...]).astype(BF16)
    q = _dot(cqn, wuq_ref[...]).astype(BF16)
    qlat = _dot(q, wuk_ref[...]) * (A_QK_DIM ** -0.5 * LOG2E)
    qidx = _dot(cqn, wqi_ref[...])
    for h in range(A_HEADS):
        qlat_ref[h] = qlat[:, h * A_KV_RANK:(h + 1) * A_KV_RANK].astype(BF16)
        qidx_ref[h] = qidx[:, h * IDX_DIM:(h + 1) * IDX_DIM].astype(BF16)
    ckv = _rms_norm_rows(c_kv, kvg_ref[...])
    ckv_ref[...] = ckv.astype(BF16)
    ckvt_ref[0:A_KV_RANK, :] = ckv.T.astype(BF16)
    ckvt_ref[A_KV_RANK:KVT_ROWS, :] = jnp.ones((KVT_ROWS - A_KV_RANK, ckv.shape[0]), BF16)
    kidx_ref[...] = _layer_norm_rows(misc[:, :IDX_DIM], lng_ref[...], lnb_ref[...]).astype(BF16)
    misc_ref[...] = misc
    misct_ref[...] = misc.T


def _in_proj(x2, wc, qg, wuq, wuk, wqi, kvg, lng, lnb, *, tm):
    T = x2.shape[0]
    const = lambda *s: pl.BlockSpec(s, lambda i: (0,) * len(s))
    return pl.pallas_call(
        _in_proj_kernel,
        grid=(T // tm,),
        in_specs=[
            pl.BlockSpec((tm, D_MODEL), lambda i: (i, 0)),
            const(D_MODEL, PROJ_W), const(1, A_Q_RANK), const(A_Q_RANK, A_HEADS * A_QK_DIM),
            const(A_HEADS * A_QK_DIM, A_HEADS * A_KV_RANK), const(A_Q_RANK, IDX_HEADS * IDX_DIM),
            const(1, A_KV_RANK), const(1, IDX_DIM), const(1, IDX_DIM),
        ],
        out_specs=[
            pl.BlockSpec((A_HEADS, tm, A_KV_RANK), lambda i: (0, i, 0)),
            pl.BlockSpec((IDX_HEADS, tm, IDX_DIM), lambda i: (0, i, 0)),
            pl.BlockSpec((tm, A_KV_RANK), lambda i: (i, 0)),
            pl.BlockSpec((KVT_ROWS, tm), lambda i: (0, i)),
            pl.BlockSpec((tm, IDX_DIM), lambda i: (i, 0)),
            pl.BlockSpec((tm, MISC_W), lambda i: (i, 0)),
            pl.BlockSpec((MISC_W, tm), lambda i: (0, i)),
            pl.BlockSpec((tm, DN_QKV_W), lambda i: (i, 0)),
            pl.BlockSpec((tm, DN_VAL_W), lambda i: (i, 0)),
        ],
        out_shape=[
            jax.ShapeDtypeStruct((A_HEADS, T, A_KV_RANK), BF16),
            jax.ShapeDtypeStruct((IDX_HEADS, T, IDX_DIM), BF16),
            jax.ShapeDtypeStruct((T, A_KV_RANK), BF16),
            jax.ShapeDtypeStruct((KVT_ROWS, T), BF16),
            jax.ShapeDtypeStruct((T, IDX_DIM), BF16),
            jax.ShapeDtypeStruct((T, MISC_W), F32),
            jax.ShapeDtypeStruct((MISC_W, T), F32),
            jax.ShapeDtypeStruct((T, DN_QKV_W), F32),
            jax.ShapeDtypeStruct((T, DN_VAL_W), F32),
        ],
        compiler_params=pltpu.CompilerParams(dimension_semantics=("parallel",), vmem_limit_bytes=VMEM_LIMIT),
        name="in_proj",
    )(x2, wc, qg, wuq, wuk, wqi, kvg, lng, lnb)


def _dsa_kernel(qidx_ref, qlat_ref, wt_ref, kidx_ref, ckv_ref, ckvt_ref, wuvt_ref, g_ref, out_ref,
                keys_ref, hi_ref, lo_ref, acc_ref, stage_ref, bias_ref, *, qb, kb, rb, slab, topk, idx_bits):
    H = A_HEADS
    j = pl.program_id(1)
    q0 = j * qb
    nkeys = (j + 1) * qb
    nrblk = (nkeys + rb - 1) // rb
    ups = rb // kb
    nchunk = nrblk * ups
    w = wt_ref[...] * (IDX_HEADS ** -0.5 * IDX_DIM ** -0.5)
    row_minus_lane = (lax.broadcasted_iota(I32, (slab, qb), 0) - lax.broadcasted_iota(I32, (slab, qb), 1))

    def skewed(nparts, consume, produce, carry):
        def chunk(c, k, carry):
            nxt = jnp.minimum(c + 1, nchunk - 1)
            if ups > 1:
                for i in range(nparts):
                    carry = produce(nxt, (k + 1) % ups, i, consume(c, k, i, carry))
                return carry
            for i in range(nparts):
                carry = consume(c, k, i, carry)
            for i in range(nparts):
                carry = produce(nxt, k, i, carry)
            return carry

        def step(i, carry):
            for k in range(ups):
                carry = chunk(i * ups + k, k, carry)
            return carry

        for i in range(nparts):
            carry = produce(0, 0, i, carry)
        return lax.fori_loop(0, nrblk, step, carry)

    nslab = kb // slab
    hps = H // nslab

    def score_matmuls(c, slot, i, carry):
        kc = kidx_ref[pl.ds(pl.multiple_of(c * kb, kb), kb), :]
        for h in range(i * hps, (i + 1) * hps):
            stage_ref[slot, h] = _dot_nt(kc, qidx_ref[h])
        return carry

    def score_keys(c, slot, i, carry):
        r0 = pl.multiple_of(c * kb, kb) + i * slab
        s = jnp.zeros((slab, qb), F32)
        for h in range(H):
            s = s + jnp.maximum(stage_ref[slot, h, i * slab:(i + 1) * slab, :], 0.0) * w[h:h + 1, :]
        bits = lax.bitcast_convert_type(s, I32)
        key = jnp.where(bits < 0, INT_MIN - bits, bits)
        key = jnp.where(row_minus_lane <= q0 - r0, key, NEG_KEY)
        keys_ref[pl.ds(r0, slab), :] = key
        hi_ref[pl.ds(r0, slab), :] = jnp.right_shift(key, 16).astype(I16)
        lo_ref[pl.ds(r0, slab), :] = ((key & 0xFFFF) - HALF16).astype(I16)
        return carry

    skewed(nslab, score_keys, score_matmuls, 0)

    nacc = 4

    def count_rows(pred):
        def body(r, acc):
            r0 = pl.multiple_of(r * rb, rb)
            k = keys_ref[pl.ds(r0, rb), :]
            m = jnp.where(pred(k, r0), 1, 0).astype(I32)
            return acc + m.reshape(nacc, rb // (nacc * SUBLANES), SUBLANES, qb).sum(axis=1)
        acc = lax.fori_loop(0, nrblk, body, jnp.zeros((nacc, SUBLANES, qb), I32))
        return acc.sum(axis=0).sum(axis=0, keepdims=True)

    def count16(ref, cand):
        def body(r, accs):
            r0 = pl.multiple_of(r * rb, rb)
            m = jnp.where(ref[pl.ds(r0, rb), :] >= cand, jnp.int16(1), jnp.int16(0))
            accs = list(accs)
            for i in range(rb // PACK16):
                accs[i % nacc] = accs[i % nacc] + m[i * PACK16:(i + 1) * PACK16, :]
            return tuple(accs)
        accs = lax.fori_loop(0, nrblk, body, tuple(jnp.zeros((PACK16, qb), I16) for _ in range(nacc)))
        tot = accs[0].astype(I32)
        for a in accs[1:]:
            tot = tot + a.astype(I32)
        return tot.sum(axis=0, keepdims=True)

    def bisect16(ref, target, cnt_all):
        def body(p, carry):
            t_u, cnt_ok, cnt_rej = carry
            cand_u = t_u | jnp.left_shift(jnp.int32(1), 15 - p)
            cnt = count16(ref, (cand_u - HALF16).astype(I16))
            ok = cnt >= target
            return jnp.where(ok, cand_u, t_u), jnp.where(ok, cnt, cnt_ok), jnp.where(ok, cnt_rej, cnt)
        return lax.fori_loop(0, 16, body, (jnp.zeros((1, qb), I32), cnt_all, jnp.zeros((1, qb), I32)))

    hi_u, cnt_ge_hi, cnt_gt_hi = bisect16(hi_ref, topk, jnp.full((1, qb), nrblk * rb, I32))
    hi_t = (hi_u - HALF16).astype(I16)

    def mask_lo(r, carry):
        r0 = pl.multiple_of(r * rb, rb)
        lo_ref[pl.ds(r0, rb), :] = jnp.where(hi_ref[pl.ds(r0, rb), :] == hi_t, lo_ref[pl.ds(r0, rb), :],
                                             jnp.int16(-HALF16))
        return carry

    lax.fori_loop(0, nrblk, mask_lo, 0)
    lo_u, cnt_ge_lo, _ = bisect16(lo_ref, topk - cnt_gt_hi, cnt_ge_hi - cnt_gt_hi)
    t = jnp.left_shift(hi_u - HALF16, 16) + lo_u
    cnt_t = cnt_gt_hi + cnt_ge_lo

    excess = jnp.where(t > NEG_KEY, cnt_t - topk, 0)

    @pl.when(jnp.max(excess) > 0)
    def _():
        keep = topk - count_rows(lambda k, r0: k >= t + 1)

        def idx_body(p, pos):
            cand = pos + jnp.left_shift(jnp.int32(1), idx_bits - 1 - p)
            c = count_rows(lambda k, r0: (k == t) & (
                r0 + lax.broadcasted_iota(I32, (rb, qb), 0) < cand))
            return jnp.where(c < keep, cand, pos)

        pos = lax.fori_loop(0, idx_bits, idx_body, jnp.zeros((1, qb), I32))

        def demote(r, carry):
            r0 = pl.multiple_of(r * rb, rb)
            k = keys_ref[pl.ds(r0, rb), :]
            row = r0 + lax.broadcasted_iota(I32, (rb, qb), 0)
            keys_ref[pl.ds(r0, rb), :] = jnp.where((k == t) & (row > pos), NEG_KEY, k)
            return carry

        lax.fori_loop(0, nrblk, demote, 0)

    thr = jnp.maximum(t, NEG_KEY + 1)

    acc_ref[...] = jnp.zeros(acc_ref.shape, F32)

    def put(carry, h, item):
        return carry[:h] + (item,) + carry[h + 1:]

    def attn_logits(c, slot, h, carry):
        r0 = pl.multiple_of(c * kb, kb)
        if h == 0:
            bias_ref[slot] = jnp.where(keys_ref[pl.ds(r0, kb), :] >= thr, 0.0, -jnp.inf)
        m_prev = carry[h][1]
        lg = _dot_nt(ckv_ref[pl.ds(r0, kb), :], qlat_ref[h]) + bias_ref[slot]
        stage_ref[slot, h] = lg
        return put(carry, h, (m_prev, jnp.maximum(m_prev, jnp.max(lg, axis=0, keepdims=True))))

    def attn_values(c, slot, h, carry):
        m_old, m_new = carry[h]
        kvt = ckvt_ref[:, pl.ds(pl.multiple_of(c * kb, kb), kb)]
        p = jnp.exp2(stage_ref[slot, h] - m_new).astype(BF16)
        acc_ref[h] = acc_ref[h] * jnp.exp2(m_old - m_new) + _dot(kvt, p)
        return carry

    m_init = jnp.full((1, qb), -1e30, F32)
    skewed(H, attn_values, attn_logits, ((m_init, m_init),) * H)

    out_t = jnp.concatenate(
        [_dot(wuvt_ref[h], (acc_ref[h, 0:A_KV_RANK, :] / acc_ref[h, A_KV_RANK:A_KV_RANK + 1, :]).astype(BF16))
         for h in range(H)], axis=0)
    out_ref[...] = _rms_norm_rows(out_t.T, g_ref[...])


def _dsa(qidx, qlat, misct, kidx, ckv, ckvt, wuvt, g, *, B, S):
    T = B * S
    qb = min(256, S)
    nq = S // qb
    kb = qb
    rb = 2 * qb if nq % 2 == 0 else qb
    topk = min(TOPK_MAX, S // 4)
    idx_bits = max(1, (S - 1).bit_length())
    slab = min(64, kb)
    kern = functools.partial(_dsa_kernel, qb=qb, kb=kb, rb=rb, slab=slab, topk=topk, idx_bits=idx_bits)
    return pl.pallas_call(
        kern,
        grid=(B, nq),
        in_specs=[
            pl.BlockSpec((IDX_HEADS, qb, IDX_DIM), lambda b, j: (0, b * nq + j, 0)),
            pl.BlockSpec((A_HEADS, qb, A_KV_RANK), lambda b, j: (0, b * nq + j, 0)),
            pl.BlockSpec((IDX_HEADS, qb), lambda b, j: (MISC_WIDX // IDX_HEADS, b * nq + j)),
            pl.BlockSpec((S, IDX_DIM), lambda b, j: (b, 0)),
            pl.BlockSpec((S, A_KV_RANK), lambda b, j: (b, 0)),
            pl.BlockSpec((KVT_ROWS, S), lambda b, j: (0, b)),
            pl.BlockSpec((A_HEADS, A_V_DIM, A_KV_RANK), lambda b, j: (0, 0, 0)),
            pl.BlockSpec((1, A_OUT_W), lambda b, j: (0, 0)),
        ],
        out_specs=pl.BlockSpec((qb, A_OUT_W), lambda b, j: (b * nq + j, 0)),
        out_shape=jax.ShapeDtypeStruct((T, A_OUT_W), F32),
        scratch_shapes=[
            pltpu.VMEM((S, qb), I32),
            pltpu.VMEM((S, qb), I16),
            pltpu.VMEM((S, qb), I16),
            pltpu.VMEM((A_HEADS, KVT_ROWS, qb), F32),
            pltpu.VMEM((rb // kb, A_HEADS, kb, qb), F32),
            pltpu.VMEM((rb // kb, kb, qb), F32),
        ],
        compiler_params=pltpu.CompilerParams(
            dimension_semantics=("parallel", "arbitrary"), vmem_limit_bytes=VMEM_LIMIT),
        name="dsa",
    )(qidx, qlat, misct, kidx, ckv, ckvt, wuvt, g)


def _gdn_constants(sb):
    r, c = np.arange(sb)[:, None], np.arange(sb)[None, :]
    same = (r // GDN_CHUNK) == (c // GDN_CHUNK)
    tri = same & (c <= r)
    diag = [((r // b) == (c // b)).astype(np.float32) for b in (8, 16, 32)] + [same.astype(np.float32)]
    eye = (r == c).astype(np.float32)
    mats = [diag[0]] + [diag[i] - diag[i - 1] for i in (1, 2, 3)] + [eye, 1.0 - eye, np.where(tri, 0.0, -np.inf)]
    mask_c = np.concatenate([tri, same], axis=0).astype(np.float32)
    mask_r = np.concatenate([same & (r <= c), same], axis=1).astype(np.float32)
    spread = (np.arange(DN_KEY_W)[None, :] // DN_K_DIM == np.arange(DN_HEADS)[:, None]).astype(np.float32)
    spread3 = np.zeros((3, 3, 3, DN_HEADS, DN_KEY_W), np.float32)
    for j in range(3):
        spread3[j, :, j] = spread
    return (jnp.asarray(np.stack(mats), F32), jnp.asarray(mask_c, BF16), jnp.asarray(mask_r, BF16),
            jnp.asarray(spread3.reshape(3, 9 * DN_HEADS, DN_KEY_W), BF16), jnp.asarray(np.tile(spread, (1, 2)), BF16))


def _gdn_kernel(qkv_ref, z_ref, misc_ref, at_ref, cw_ref, alog_c_ref, dtb_c_ref, alog_r_ref, dtb_r_ref,
                ng_ref, fm_ref, mc_ref, mr_ref, sr_ref, sc_ref, out_ref, xbuf_ref, state_ref, *, tb, sb):
    C = GDN_CHUNK
    nck = tb // C
    H = DN_HEADS
    step = pl.program_id(1)

    @pl.when(step == 0)
    def _():
        xbuf_ref[0:SUBLANES, :] = jnp.zeros((SUBLANES, DN_QKV_W), F32)
        state_ref[...] = jnp.zeros(state_ref.shape, F32)

    xbuf_ref[SUBLANES:SUBLANES + tb, :] = qkv_ref[...]
    xe = xbuf_ref[...]
    y = xe[SUBLANES:] * cw_ref[DN_CONV - 1:DN_CONV, :]
    for d in range(1, DN_CONV):
        y = y + pltpu.roll(xe, d, 0)[SUBLANES:] * cw_ref[DN_CONV - 1 - d:DN_CONV - d, :]
    xbuf_ref[0:SUBLANES, :] = xe[tb:tb + SUBLANES]
    y = _silu(y)

    def softplus(v):
        return jnp.maximum(v, 0.0) + jnp.log(1.0 + jnp.exp(-jnp.abs(v)))

    misc = misc_ref[...]
    beta_c = _sigmoid(misc[:, MISC_B:MISC_B + H])
    g_c = -jnp.exp(alog_r_ref[...]) * softplus(misc[:, MISC_A:MISC_A + H] + dtb_r_ref[...])
    g_r = -jnp.exp(alog_c_ref[...]) * softplus(at_ref[...] + dtb_c_ref[...])

    rows = [slice(b * sb, (b + 1) * sb) for b in range(tb // sb)]
    mask_c = mc_ref[...]
    mask_r = mr_ref[...]

    def pieces(v, axis, n=3):
        out, rest = [], v
        for _ in range(n - 1):
            out.append(rest.astype(BF16))
            rest = rest - out[-1].astype(F32)
        return jnp.concatenate(out + [rest.astype(BF16)], axis=axis)

    sums_c = [_dot(mask_c, pieces(g_c[r], 1)) for r in rows]
    sums_c = [s[:, 0:H] + s[:, H:2 * H] + s[:, 2 * H:3 * H] for s in sums_c]
    sums_r = [_dot(pieces(g_r[:, r], 0), mask_r) for r in rows]
    sums_r = [s[0:H] + s[H:2 * H] + s[2 * H:3 * H] for s in sums_r]
    gcum_c = [s[0:sb] for s in sums_c]
    glast_c = [s[sb:2 * sb] for s in sums_c]
    gcum_r = [s[:, 0:sb] for s in sums_r]
    glast_r = [s[:, sb:2 * sb] for s in sums_r]
    blk8, ring_f, eye, off_diag, tri_bias = fm_ref[0], [fm_ref[1], fm_ref[2], fm_ref[3]], fm_ref[4], fm_ref[5], fm_ref[6]

    hs = range(H)
    bh = [(b, h) for b in range(len(rows)) for h in hs]
    ch = range(len(bh))
    def expand(*vs):
        p = pieces(jnp.concatenate(list(vs) + [vs[-1]] * (3 - len(vs)), axis=1), 1)
        return [_dot(p, sr_ref[j]) for j in range(len(vs))]

    head_sums = lambda v: _dot_nt(pieces(v, 1, 2), sc_ref[...])
    yq, yk, yv = y[:, :DN_KEY_W], y[:, DN_KEY_W:2 * DN_KEY_W], y[:, 2 * DN_KEY_W:]
    gc_all = jnp.concatenate(gcum_c, axis=0)
    eg_c = jnp.exp(gc_all)
    rq_x, rk_x, beta_x = expand(lax.rsqrt(head_sums(yq * yq) + RMS_EPS) * (DN_K_DIM ** -0.5),
                                lax.rsqrt(head_sums(yk * yk) + RMS_EPS), beta_c)
    eg_x, egl_x = expand(eg_c, jnp.exp(jnp.concatenate(glast_c, axis=0) - gc_all))
    q_n, k_n = yq * rq_x, yk * rk_x
    kb_all = k_n * beta_x
    k_bf, kb_bf, q_bf = k_n.astype(BF16), kb_all.astype(BF16), q_n.astype(BF16)
    rhs_v, rhs_k = (yv * beta_x).astype(BF16), (kb_all * eg_x).astype(BF16)
    qd_all = (q_n * eg_x).astype(BF16)
    ke_all = (k_n * egl_x).astype(BF16)
    head = lambda a, i: a[rows[bh[i][0]], bh[i][1] * DN_K_DIM:(bh[i][1] + 1) * DN_K_DIM]
    gcs = [gcum_c[b][:, h:h + 1] for b, h in bh]
    kk = [_dot_nt(head(kb_bf, i), head(k_bf, i)) for i in ch]
    qk = [_dot_nt(head(q_bf, i), head(k_bf, i)) for i in ch]
    decay = [jnp.exp((gcs[i] - gcum_r[b][h:h + 1, :]) + tri_bias) for i, (b, h) in enumerate(bh)]
    lower = [kk[i] * (decay[i] * off_diag) for i in ch]
    attn = [(qk[i] * decay[i]).astype(BF16) for i in ch]
    pw = [lower[i] * blk8 for i in ch]
    x_inv = [eye - pw[i] for i in ch]
    for _ in range(2):
        pwb = [p.astype(BF16) for p in pw]
        pw = [_dot(p, p) for p in pwb]
        x_inv = [x_inv[i] + _dot(x_inv[i].astype(BF16), pw[i].astype(BF16)) for i in ch]
    for ring in ring_f:
        xb = [x.astype(BF16) for x in x_inv]
        t1 = [_dot(xb[i], (lower[i] * ring).astype(BF16)).astype(BF16) for i in ch]
        x_inv = [x_inv[i] - _dot(t1[i], xb[i]) for i in ch]
    uw = [_dot(x_inv[i].astype(BF16), jnp.concatenate([head(rhs_v, i), head(rhs_k, i)], axis=1)) for i in ch]
    us = [m[:, :DN_V_DIM] for m in uw]
    wkb = [m[:, DN_V_DIM:].astype(BF16) for m in uw]
    qd = [head(qd_all, i) for i in ch]
    ke = [head(ke_all, i) for i in ch]

    st = [state_ref[h] for h in hs]
    outs = [[] for _ in hs]
    for n in range(nck):
        b, r0 = divmod(n * C, sb)
        r = slice(r0, r0 + C)
        ws = [_dot(jnp.concatenate([wkb[b * H + h][r], qd[b * H + h][r]], axis=0), st[h].astype(BF16)) for h in hs]
        v_nb = [(us[b * H + h][r] - ws[h][0:C]).astype(BF16) for h in hs]
        for h in hs:
            outs[h].append(ws[h][C:2 * C] + _dot(attn[b * H + h][r, r], v_nb[h]))
        cd = [jnp.exp(glast_r[b][h:h + 1, r0:r0 + 1]) for h in hs]
        st = [st[h] * cd[h] + _dot_tn(ke[b * H + h][r], v_nb[h]) for h in hs]
    for h in hs:
        state_ref[h] = st[h]
    o = jnp.concatenate([jnp.concatenate(outs[h], axis=0) for h in hs], axis=1)
    r_o = lax.rsqrt(head_sums(o * o) * (1.0 / DN_V_DIM) + RMS_EPS)
    o = o * expand(r_o)[0] * ng_ref[...]
    out_ref[...] = (o * _silu(z_ref[...])).astype(out_ref.dtype)


def _gdn(qkv, z, misc, misct, conv_w, a_log, dt_bias, norm_g, *, B, S):
    T = B * S
    sb = min(GDN_SOLVE, S)
    tb = min(GDN_SOLVES_PER_STEP * sb, S)
    ns = S // tb
    H = DN_HEADS
    kern = functools.partial(_gdn_kernel, tb=tb, sb=sb)
    consts = _gdn_constants(sb)
    const = lambda *s: pl.BlockSpec(s, lambda b, i: (0,) * len(s))
    return pl.pallas_call(
        kern,
        grid=(B, ns),
        in_specs=[
            pl.BlockSpec((tb, DN_QKV_W), lambda b, i: (b * ns + i, 0)),
            pl.BlockSpec((tb, DN_VAL_W), lambda b, i: (b * ns + i, 0)),
            pl.BlockSpec((tb, MISC_W), lambda b, i: (b * ns + i, 0)),
            pl.BlockSpec((H, tb), lambda b, i: (MISC_A // H, b * ns + i)),
            const(DN_CONV, DN_QKV_W), const(H, 1), const(H, 1), const(1, H), const(1, H), const(1, DN_VAL_W),
        ] + [const(*c.shape) for c in consts],
        out_specs=pl.BlockSpec((tb, DN_VAL_W), lambda b, i: (b * ns + i, 0)),
        out_shape=jax.ShapeDtypeStruct((T, DN_VAL_W), BF16),
        scratch_shapes=[
            pltpu.VMEM((tb + SUBLANES, DN_QKV_W), F32),
            pltpu.VMEM((H, DN_K_DIM, DN_V_DIM), F32),
        ],
        compiler_params=pltpu.CompilerParams(
            dimension_semantics=("parallel", "arbitrary"), vmem_limit_bytes=VMEM_LIMIT),
        name="gdn",
    )(qkv, z, misc, misct, conv_w, a_log.reshape(H, 1), dt_bias.reshape(H, 1),
      a_log.reshape(1, H), dt_bias.reshape(1, H), jnp.tile(norm_g.reshape(1, DN_V_DIM), (1, H)), *consts)


def _out_proj_kernel(a_ref, d_ref, x_ref, wa_ref, wd_ref, g_ref, b_ref, h_ref, *, alpha):
    for r in range(0, a_ref.shape[0], ROW_GROUP):
        rows = slice(r, r + ROW_GROUP)
        mix = _dot(a_ref[rows, :].astype(BF16), wa_ref[...]) + _dot(d_ref[rows, :], wd_ref[...])
        h_ref[rows, :] = _layer_norm_rows(alpha * x_ref[rows, :] + mix, g_ref[...], b_ref[...])


def _out_proj(a, d, x2, wa, wd, g, b, *, alpha, tm):
    T = x2.shape[0]
    const = lambda *s: pl.BlockSpec(s, lambda i: (0,) * len(s))
    return pl.pallas_call(
        functools.partial(_out_proj_kernel, alpha=alpha),
        grid=(T // tm,),
        in_specs=[
            pl.BlockSpec((tm, A_OUT_W), lambda i: (i, 0)),
            pl.BlockSpec((tm, DN_VAL_W), lambda i: (i, 0)),
            pl.BlockSpec((tm, D_MODEL), lambda i: (i, 0)),
            const(A_OUT_W, D_MODEL), const(DN_VAL_W, D_MODEL), const(1, D_MODEL), const(1, D_MODEL),
        ],
        out_specs=pl.BlockSpec((tm, D_MODEL), lambda i: (i, 0)),
        out_shape=jax.ShapeDtypeStruct((T, D_MODEL), F32),
        compiler_params=pltpu.CompilerParams(dimension_semantics=("parallel",), vmem_limit_bytes=VMEM_LIMIT),
        name="out_proj",
    )(a, d, x2, wa, wd, g, b)


def _ffn_kernel(h_ref, halo_ref, wg_ref, wu_ref, cw_ref, cb_ref, wd_ref, g_ref, b_ref, out_ref,
                acc_ref, *, alpha, tiles_per_seq):
    i = pl.program_id(0)
    f = pl.program_id(1)
    nf = pl.num_programs(1)
    hb = h_ref[...].astype(BF16)
    wg = wg_ref[...]
    gate = _dot(hb, wg)
    up = _dot(hb, wu_ref[...])
    halo = _dot(halo_ref[...].astype(BF16), wg)
    halo = jnp.where(i % tiles_per_seq == 0, 0.0, halo)
    ge = jnp.concatenate([halo, gate], axis=0)
    conv = gate * cw_ref[FFN_CONV - 1:FFN_CONV, :]
    for d in range(1, FFN_CONV):
        conv = conv + pltpu.roll(ge, d, 0)[SUBLANES:] * cw_ref[FFN_CONV - 1 - d:FFN_CONV - d, :]
    act = (_silu(conv + cb_ref[...]) * up).astype(BF16)
    part = _dot(act, wd_ref[...])

    @pl.when(f == 0)
    def _():
        acc_ref[...] = part

    @pl.when(f > 0)
    def _():
        acc_ref[...] += part

    @pl.when(f == nf - 1)
    def _():
        out_ref[...] = _layer_norm_rows(alpha * h_ref[...] + acc_ref[...], g_ref[...], b_ref[...])


def _ffn(h, wg, wu, cw, cb, wd, g, b, *, alpha, S, tm, tf):
    T = h.shape[0]
    nf = D_FF // tf
    hs = tm // SUBLANES
    kern = functools.partial(_ffn_kernel, alpha=alpha, tiles_per_seq=S // tm)
    return pl.pallas_call(
        kern,
        grid=(T // tm, nf),
        in_specs=[
            pl.BlockSpec((tm, D_MODEL), lambda i, f: (i, 0)),
            pl.BlockSpec((SUBLANES, D_MODEL), lambda i, f: (jnp.maximum(i * hs - 1, 0), 0)),
            pl.BlockSpec((D_MODEL, tf), lambda i, f: (0, f)),
            pl.BlockSpec((D_MODEL, tf), lambda i, f: (0, f)),
            pl.BlockSpec((FFN_CONV, tf), lambda i, f: (0, f)),
            pl.BlockSpec((1, tf), lambda i, f: (0, f)),
            pl.BlockSpec((tf, D_MODEL), lambda i, f: (f, 0)),
            pl.BlockSpec((1, D_MODEL), lambda i, f: (0, 0)),
            pl.BlockSpec((1, D_MODEL), lambda i, f: (0, 0)),
        ],
        out_specs=pl.BlockSpec((tm, D_MODEL), lambda i, f: (i, 0)),
        out_shape=jax.ShapeDtypeStruct((T, D_MODEL), F32),
        scratch_shapes=[pltpu.VMEM((tm, D_MODEL), F32)],
        compiler_params=pltpu.CompilerParams(
            dimension_semantics=("parallel", "arbitrary"), vmem_limit_bytes=VMEM_LIMIT),
        name="ffn",
    )(h, h, wg, wu, cw, cb, wd, g, b)


def _regroup_w_in(w):
    offs = [0]
    for s in IN_SIZES:
        offs.append(offs[-1] + s)
    cq, ckv, kidx, widx, qkv, z, b, a = (w[:, offs[i]:offs[i + 1]] for i in range(8))
    pad = jnp.zeros((w.shape[0], MISC_W - (IDX_DIM + IDX_HEADS + 2 * DN_HEADS)), w.dtype)
    return jnp.concatenate([cq, ckv, kidx, widx, b, a, pad, qkv, z], axis=1).astype(BF16)


def _layer(x2, p, *, B, S, alpha):
    row = lambda v: v.reshape(1, -1)
    wc = _regroup_w_in(p["w_in"])
    wuk_bd = jnp.einsum("hdc,hg->hdgc", p["w_uk"], jnp.eye(A_HEADS, dtype=F32)).reshape(
        A_HEADS * A_QK_DIM, A_HEADS * A_KV_RANK)
    qlat, qidx, ckv, ckvt, kidx, misc, misct, qkv, z = _in_proj(
        x2, wc, row(p["q_norm_g"]), p["w_uq"].astype(BF16), wuk_bd.astype(BF16), p["w_qidx"].astype(BF16),
        row(p["kv_norm_g"]),
        row(p["kidx_ln_g"]), row(p["kidx_ln_b"]), tm=min(512, S))
    a_out = _dsa(qidx, qlat, misct, kidx, ckv, ckvt, p["w_uv"].transpose(0, 2, 1).astype(BF16),
                 row(p["attn_out_g"]), B=B, S=S)
    d_out = _gdn(qkv, z, misc, misct, p["dn_conv_w"], p["dn_a_log"], p["dn_dt_bias"], p["dn_norm_g"],
                 B=B, S=S)
    w_out = p["w_out"].astype(BF16)
    h = _out_proj(a_out, d_out, x2, w_out[:A_OUT_W], w_out[A_OUT_W:], row(p["ln1_g"]), row(p["ln1_b"]),
                  alpha=alpha, tm=min(512, S))
    w_ffn = p["ffn_w_in"].astype(BF16)
    return _ffn(h, w_ffn[:, :D_FF], w_ffn[:, D_FF:], p["ffn_conv_w"], row(p["ffn_conv_b"]),
                p["ffn_w_down"].astype(BF16), row(p["ln2_g"]), row(p["ln2_b"]),
                alpha=alpha, S=S, tm=min(512, S), tf=D_FF // 2)


_PARAM_NAMES = ("w_in", "q_norm_g", "w_uq", "w_qidx", "kv_norm_g", "w_uk", "w_uv", "kidx_ln_g", "kidx_ln_b",
                "attn_out_g", "dn_conv_w", "dn_a_log", "dn_dt_bias", "dn_norm_g", "w_out", "ln1_g", "ln1_b",
                "ffn_w_in", "ffn_conv_w", "ffn_conv_b", "ffn_w_down", "ln2_g", "ln2_b")


def kernel(x, w_in, q_norm_g, w_uq, w_qidx, kv_norm_g, w_uk, w_uv, kidx_ln_g, kidx_ln_b, attn_out_g, dn_conv_w, dn_a_log, dn_dt_bias, dn_norm_g, w_out, ln1_g, ln1_b, ffn_w_in, ffn_conv_w, ffn_conv_b, ffn_w_down, ln2_g, ln2_b):
    params = (w_in, q_norm_g, w_uq, w_qidx, kv_norm_g, w_uk, w_uv, kidx_ln_g, kidx_ln_b, attn_out_g, dn_conv_w,
              dn_a_log, dn_dt_bias, dn_norm_g, w_out, ln1_g, ln1_b, ffn_w_in, ffn_conv_w, ffn_conv_b, ffn_w_down,
              ln2_g, ln2_b)
    B, S, D = x.shape
    depth = w_in.shape[0]
    alpha = (2 * depth) ** 0.25
    x2 = x.reshape(B * S, D)
    for l in range(depth):
        x2 = _layer(x2, {n: v[l] for n, v in zip(_PARAM_NAMES, params)}, B=B, S=S, alpha=alpha)
    return x2.reshape(B, S, D)
```

```python
import functools

import jax
import jax.numpy as jnp
import numpy as np
from jax import lax
from jax.experimental import pallas as pl
from jax.experimental.pallas import tpu as pltpu

F32 = jnp.float32
BF16 = jnp.bfloat16
I32 = jnp.int32
I16 = jnp.int16

D_MODEL = 1024
A_HEADS = 8
A_QK_DIM = 64
A_V_DIM = 64
A_Q_RANK = 256
A_KV_RANK = 128
IDX_HEADS = 8
IDX_DIM = 64
TOPK_MAX = 256
DN_HEADS = 8
DN_K_DIM = 64
DN_V_DIM = 64
DN_CONV = 4
D_FF = 2816
FFN_CONV = 3
RMS_EPS = 1e-6
LN_EPS = 1e-5

A_OUT_W = A_HEADS * A_V_DIM
DN_KEY_W = DN_HEADS * DN_K_DIM
DN_VAL_W = DN_HEADS * DN_V_DIM
DN_QKV_W = 2 * DN_KEY_W + DN_VAL_W
IN_SIZES = (A_Q_RANK, A_KV_RANK, IDX_DIM, IDX_HEADS, DN_QKV_W, DN_VAL_W, DN_HEADS, DN_HEADS)

MISC_W = 128
MISC_WIDX = IDX_DIM
MISC_B = IDX_DIM + IDX_HEADS
MISC_A = MISC_B + DN_HEADS
PROJ_W = A_Q_RANK + A_KV_RANK + MISC_W + DN_QKV_W + DN_VAL_W

LOG2E = 1.4426950408889634
SUBLANES = 8
PACK16 = 16
KVT_ROWS = A_KV_RANK + PACK16
HALF16 = 2 ** 15
INT_MIN = -(2 ** 31)
NEG_KEY = INT_MIN + 0x00800000
VMEM_LIMIT = 56 * 1024 * 1024

ROW_GROUP = 256
GDN_CHUNK = 64
GDN_SOLVE = 256
GDN_SOLVES_PER_STEP = 1


def _dot(a, b, precision=None):
    return jnp.dot(a, b, preferred_element_type=F32, precision=precision)


def _dot_nt(a, b, precision=None):
    return lax.dot_general(a, b, (((1,), (1,)), ((), ())), preferred_element_type=F32, precision=precision)


def _dot_tn(a, b, precision=None):
    return lax.dot_general(a, b, (((0,), (0,)), ((), ())), preferred_element_type=F32, precision=precision)


def _sigmoid(x):
    return 1.0 / (1.0 + jnp.exp(-x))


def _silu(x):
    return x * _sigmoid(x)


def _layer_norm_rows(v, g, b):
    mu = jnp.mean(v, axis=-1, keepdims=True)
    d = v - mu
    var = jnp.mean(d * d, axis=-1, keepdims=True)
    return d * lax.rsqrt(var + LN_EPS) * g + b


def _rms_norm_rows(v, g):
    return v * lax.rsqrt(jnp.mean(v * v, axis=-1, keepdims=True) + RMS_EPS) * g


def _in_proj_kernel(x_ref, w_ref, qg_ref, wuq_ref, wuk_ref, wqi_ref, kvg_ref, lng_ref, lnb_ref,
                    qlat_ref, qidx_ref, ckv_ref, ckvt_ref, kidx_ref, misc_ref, misct_ref, qkv_ref, z_ref):
    xb = x_ref[...].astype(BF16)
    proj = _dot(xb, w_ref[...])
    o = 0
    c_q = proj[:, o:o + A_Q_RANK]; o += A_Q_RANK
    c_kv = proj[:, o:o + A_KV_RANK]; o += A_KV_RANK
    misc = proj[:, o:o + MISC_W]; o += MISC_W
    qkv_ref[...] = proj[:, o:o + DN_QKV_W]; o += DN_QKV_W
    z_ref[...] = proj[:, o:o + DN_VAL_W]

    cqn = _rms_norm_rows(c_q, qg_ref[...]).astype(BF16)
    q = _dot(cqn, wuq_ref[...]).astype(BF16)
    qlat = _dot(q, wuk_ref[...]) * (A_QK_DIM ** -0.5 * LOG2E)
    qidx = _dot(cqn, wqi_ref[...])
    for h in range(A_HEADS):
        qlat_ref[h] = qlat[:, h * A_KV_RANK:(h + 1) * A_KV_RANK].astype(BF16)
        qidx_ref[h] = qidx[:, h * IDX_DIM:(h + 1) * IDX_DIM].astype(BF16)
    ckv = _rms_norm_rows(c_kv, kvg_ref[...])
    ckv_ref[...] = ckv.astype(BF16)
    ckvt_ref[0:A_KV_RANK, :] = ckv.T.astype(BF16)
    ckvt_ref[A_KV_RANK:KVT_ROWS, :] = jnp.ones((KVT_ROWS - A_KV_RANK, ckv.shape[0]), BF16)
    kidx_ref[...] = _layer_norm_rows(misc[:, :IDX_DIM], lng_ref[...], lnb_ref[...]).astype(BF16)
    misc_ref[...] = misc
    misct_ref[...] = misc.T


def _in_proj(x2, wc, qg, wuq, wuk, wqi, kvg, lng, lnb, *, tm):
    T = x2.shape[0]
    const = lambda *s: pl.BlockSpec(s, lambda i: (0,) * len(s))
    return pl.pallas_call(
        _in_proj_kernel,
        grid=(T // tm,),
        in_specs=[
            pl.BlockSpec((tm, D_MODEL), lambda i: (i, 0)),
            const(D_MODEL, PROJ_W), const(1, A_Q_RANK), const(A_Q_RANK, A_HEADS * A_QK_DIM),
            const(A_HEADS * A_QK_DIM, A_HEADS * A_KV_RANK), const(A_Q_RANK, IDX_HEADS * IDX_DIM),
            const(1, A_KV_RANK), const(1, IDX_DIM), const(1, IDX_DIM),
        ],
        out_specs=[
            pl.BlockSpec((A_HEADS, tm, A_KV_RANK), lambda i: (0, i, 0)),
            pl.BlockSpec((IDX_HEADS, tm, IDX_DIM), lambda i: (0, i, 0)),
            pl.BlockSpec((tm, A_KV_RANK), lambda i: (i, 0)),
            pl.BlockSpec((KVT_ROWS, tm), lambda i: (0, i)),
            pl.BlockSpec((tm, IDX_DIM), lambda i: (i, 0)),
            pl.BlockSpec((tm, MISC_W), lambda i: (i, 0)),
            pl.BlockSpec((MISC_W, tm), lambda i: (0, i)),
            pl.BlockSpec((tm, DN_QKV_W), lambda i: (i, 0)),
            pl.BlockSpec((tm, DN_VAL_W), lambda i: (i, 0)),
        ],
        out_shape=[
            jax.ShapeDtypeStruct((A_HEADS, T, A_KV_RANK), BF16),
            jax.ShapeDtypeStruct((IDX_HEADS, T, IDX_DIM), BF16),
            jax.ShapeDtypeStruct((T, A_KV_RANK), BF16),
            jax.ShapeDtypeStruct((KVT_ROWS, T), BF16),
            jax.ShapeDtypeStruct((T, IDX_DIM), BF16),
            jax.ShapeDtypeStruct((T, MISC_W), F32),
            jax.ShapeDtypeStruct((MISC_W, T), F32),
            jax.ShapeDtypeStruct((T, DN_QKV_W), F32),
            jax.ShapeDtypeStruct((T, DN_VAL_W), F32),
        ],
        compiler_params=pltpu.CompilerParams(dimension_semantics=("parallel",), vmem_limit_bytes=VMEM_LIMIT),
        name="in_proj",
    )(x2, wc, qg, wuq, wuk, wqi, kvg, lng, lnb)


def _dsa_kernel(qidx_ref, qlat_ref, wt_ref, kidx_ref, ckv_ref, ckvt_ref, wuvt_ref, g_ref, out_ref,
                keys_ref, hi_ref, lo_ref, acc_ref, stage_ref, bias_ref, *, qb, kb, rb, slab, topk, idx_bits):
    H = A_HEADS
    j = pl.program_id(1)
    q0 = j * qb
    nkeys = (j + 1) * qb
    nrblk = (nkeys + rb - 1) // rb
    ups = rb // kb
    nchunk = nrblk * ups
    w = wt_ref[...] * (IDX_HEADS ** -0.5 * IDX_DIM ** -0.5)
    row_minus_lane = (lax.broadcasted_iota(I32, (slab, qb), 0) - lax.broadcasted_iota(I32, (slab, qb), 1))

    def skewed(nparts, consume, produce, carry, on_tail=None):
        def chunk(c, k, carry):
            if ups > 1:
                for i in range(nparts):
                    carry = produce(c + 1, (k + 1) % ups, i, consume(c, k, i, carry))
                return carry
            for i in range(nparts):
                carry = consume(c, k, i, carry)
            for i in range(nparts):
                carry = produce(c + 1, k, i, carry)
            return carry

        def step(i, carry):
            for k in range(ups):
                carry = chunk(i * ups + k, k, carry)
            return carry

        for i in range(nparts):
            carry = produce(0, 0, i, carry)
        carry = lax.fori_loop(0, nrblk - 1, step, carry)
        for k in range(ups - 1):
            carry = chunk((nrblk - 1) * ups + k, k, carry)
        last = nchunk - 1

        @pl.when(last * kb < nkeys)
        def _():
            for i in range(nparts):
                consume(last, ups - 1, i, carry)

        if on_tail is not None:
            @pl.when(last * kb >= nkeys)
            def _():
                on_tail(last)
        return carry

    nslab = kb // slab
    hps = H // nslab

    def score_matmuls(c, slot, i, carry):
        kc = kidx_ref[pl.ds(pl.multiple_of(c * kb, kb), kb), :]
        for h in range(i * hps, (i + 1) * hps):
            stage_ref[slot, h] = _dot_nt(kc, qidx_ref[h])
        return carry

    def score_keys(c, slot, i, carry):
        r0 = pl.multiple_of(c * kb, kb) + i * slab
        s = jnp.zeros((slab, qb), F32)
        for h in range(H):
            s = s + jnp.maximum(stage_ref[slot, h, i * slab:(i + 1) * slab, :], 0.0) * w[h:h + 1, :]
        bits = lax.bitcast_convert_type(s, I32)
        key = jnp.where(bits < 0, INT_MIN - bits, bits)
        key = jnp.where(row_minus_lane <= q0 - r0, key, NEG_KEY)
        keys_ref[pl.ds(r0, slab), :] = key
        hi_ref[pl.ds(r0, slab), :] = jnp.right_shift(key, 16).astype(I16)
        lo_ref[pl.ds(r0, slab), :] = ((key & 0xFFFF) - HALF16).astype(I16)
        return carry

    def score_tail(c):
        r0 = pl.multiple_of(c * kb, kb)
        keys_ref[pl.ds(r0, kb), :] = jnp.full((kb, qb), NEG_KEY, I32)
        hi_ref[pl.ds(r0, kb), :] = jnp.full((kb, qb), NEG_KEY >> 16, I16)
        lo_ref[pl.ds(r0, kb), :] = jnp.full((kb, qb), (NEG_KEY & 0xFFFF) - HALF16, I16)

    skewed(nslab, score_keys, score_matmuls, 0, score_tail)

    nacc = 4

    def count_rows(pred):
        def body(r, acc):
            r0 = pl.multiple_of(r * rb, rb)
            k = keys_ref[pl.ds(r0, rb), :]
            m = jnp.where(pred(k, r0), 1, 0).astype(I32)
            return acc + m.reshape(nacc, rb // (nacc * SUBLANES), SUBLANES, qb).sum(axis=1)
        acc = lax.fori_loop(0, nrblk, body, jnp.zeros((nacc, SUBLANES, qb), I32))
        return acc.sum(axis=0).sum(axis=0, keepdims=True)

    def count16(ref, cand):
        def body(r, accs):
            r0 = pl.multiple_of(r * rb, rb)
            m = jnp.where(ref[pl.ds(r0, rb), :] >= cand, jnp.int16(1), jnp.int16(0))
            accs = list(accs)
            for i in range(rb // PACK16):
                accs[i % nacc] = accs[i % nacc] + m[i * PACK16:(i + 1) * PACK16, :]
            return tuple(accs)
        accs = lax.fori_loop(0, nrblk, body, tuple(jnp.zeros((PACK16, qb), I16) for _ in range(nacc)))
        tot = accs[0].astype(I32)
        for a in accs[1:]:
            tot = tot + a.astype(I32)
        return tot.sum(axis=0, keepdims=True)

    def bisect16(ref, target, cnt_all):
        def body(p, carry):
            t_u, cnt_ok, cnt_rej = carry
            cand_u = t_u | jnp.left_shift(jnp.int32(1), 15 - p)
            cnt = count16(ref, (cand_u - HALF16).astype(I16))
            ok = cnt >= target
            return jnp.where(ok, cand_u, t_u), jnp.where(ok, cnt, cnt_ok), jnp.where(ok, cnt_rej, cnt)
        return lax.fori_loop(0, 16, body, (jnp.zeros((1, qb), I32), cnt_all, jnp.zeros((1, qb), I32)))

    hi_u, cnt_ge_hi, cnt_gt_hi = bisect16(hi_ref, topk, jnp.full((1, qb), nrblk * rb, I32))
    hi_t = (hi_u - HALF16).astype(I16)

    def mask_lo(r, carry):
        r0 = pl.multiple_of(r * rb, rb)
        lo_ref[pl.ds(r0, rb), :] = jnp.where(hi_ref[pl.ds(r0, rb), :] == hi_t, lo_ref[pl.ds(r0, rb), :],
                                             jnp.int16(-HALF16))
        return carry

    lax.fori_loop(0, nrblk, mask_lo, 0)
    lo_u, cnt_ge_lo, _ = bisect16(lo_ref, topk - cnt_gt_hi, cnt_ge_hi - cnt_gt_hi)
    t = jnp.left_shift(hi_u - HALF16, 16) + lo_u
    cnt_t = cnt_gt_hi + cnt_ge_lo

    excess = jnp.where(t > NEG_KEY, cnt_t - topk, 0)

    @pl.when(jnp.max(excess) > 0)
    def _():
        keep = topk - count_rows(lambda k, r0: k >= t + 1)

        def idx_body(p, pos):
            cand = pos + jnp.left_shift(jnp.int32(1), idx_bits - 1 - p)
            c = count_rows(lambda k, r0: (k == t) & (
                r0 + lax.broadcasted_iota(I32, (rb, qb), 0) < cand))
            return jnp.where(c < keep, cand, pos)

        pos = lax.fori_loop(0, idx_bits, idx_body, jnp.zeros((1, qb), I32))

        def demote(r, carry):
            r0 = pl.multiple_of(r * rb, rb)
            k = keys_ref[pl.ds(r0, rb), :]
            row = r0 + lax.broadcasted_iota(I32, (rb, qb), 0)
            keys_ref[pl.ds(r0, rb), :] = jnp.where((k == t) & (row > pos), NEG_KEY, k)
            return carry

        lax.fori_loop(0, nrblk, demote, 0)

    thr = jnp.maximum(t, NEG_KEY + 1)

    acc_ref[...] = jnp.zeros(acc_ref.shape, F32)

    def put(carry, h, item):
        return carry[:h] + (item,) + carry[h + 1:]

    def attn_logits(c, slot, h, carry):
        r0 = pl.multiple_of(c * kb, kb)
        if h == 0:
            bias_ref[slot] = jnp.where(keys_ref[pl.ds(r0, kb), :] >= thr, 0.0, -jnp.inf)
        m_prev = carry[h][1]
        lg = _dot_nt(ckv_ref[pl.ds(r0, kb), :], qlat_ref[h]) + bias_ref[slot]
        stage_ref[slot, h] = lg
        return put(carry, h, (m_prev, jnp.maximum(m_prev, jnp.max(lg, axis=0, keepdims=True))))

    def attn_values(c, slot, h, carry):
        m_old, m_new = carry[h]
        kvt = ckvt_ref[:, pl.ds(pl.multiple_of(c * kb, kb), kb)]
        p = jnp.exp2(stage_ref[slot, h] - m_new).astype(BF16)
        acc_ref[h] = acc_ref[h] * jnp.exp2(m_old - m_new) + _dot(kvt, p)
        return carry

    m_init = jnp.full((1, qb), -1e30, F32)
    skewed(H, attn_values, attn_logits, ((m_init, m_init),) * H)

    out_t = jnp.concatenate(
        [_dot(wuvt_ref[h], (acc_ref[h, 0:A_KV_RANK, :] / acc_ref[h, A_KV_RANK:A_KV_RANK + 1, :]).astype(BF16))
         for h in range(H)], axis=0)
    out_ref[...] = _rms_norm_rows(out_t.T, g_ref[...])


def _dsa(qidx, qlat, misct, kidx, ckv, ckvt, wuvt, g, *, B, S):
    T = B * S
    qb = min(256, S)
    nq = S // qb
    kb = qb
    rb = 2 * qb if nq % 2 == 0 else qb
    topk = min(TOPK_MAX, S // 4)
    idx_bits = max(1, (S - 1).bit_length())
    slab = min(64, kb)
    kern = functools.partial(_dsa_kernel, qb=qb, kb=kb, rb=rb, slab=slab, topk=topk, idx_bits=idx_bits)
    return pl.pallas_call(
        kern,
        grid=(B, nq),
        in_specs=[
            pl.BlockSpec((IDX_HEADS, qb, IDX_DIM), lambda b, j: (0, b * nq + j, 0)),
            pl.BlockSpec((A_HEADS, qb, A_KV_RANK), lambda b, j: (0, b * nq + j, 0)),
            pl.BlockSpec((IDX_HEADS, qb), lambda b, j: (MISC_WIDX // IDX_HEADS, b * nq + j)),
            pl.BlockSpec((S, IDX_DIM), lambda b, j: (b, 0)),
            pl.BlockSpec((S, A_KV_RANK), lambda b, j: (b, 0)),
            pl.BlockSpec((KVT_ROWS, S), lambda b, j: (0, b)),
            pl.BlockSpec((A_HEADS, A_V_DIM, A_KV_RANK), lambda b, j: (0, 0, 0)),
            pl.BlockSpec((1, A_OUT_W), lambda b, j: (0, 0)),
        ],
        out_specs=pl.BlockSpec((qb, A_OUT_W), lambda b, j: (b * nq + j, 0)),
        out_shape=jax.ShapeDtypeStruct((T, A_OUT_W), F32),
        scratch_shapes=[
            pltpu.VMEM((S, qb), I32),
            pltpu.VMEM((S, qb), I16),
            pltpu.VMEM((S, qb), I16),
            pltpu.VMEM((A_HEADS, KVT_ROWS, qb), F32),
            pltpu.VMEM((rb // kb, A_HEADS, kb, qb), F32),
            pltpu.VMEM((rb // kb, kb, qb), F32),
        ],
        compiler_params=pltpu.CompilerParams(
            dimension_semantics=("parallel", "arbitrary"), vmem_limit_bytes=VMEM_LIMIT),
        name="dsa",
    )(qidx, qlat, misct, kidx, ckv, ckvt, wuvt, g)


def _gdn_constants(sb):
    r, c = np.arange(sb)[:, None], np.arange(sb)[None, :]
    same = (r // GDN_CHUNK) == (c // GDN_CHUNK)
    tri = same & (c <= r)
    diag = [((r // b) == (c // b)).astype(np.float32) for b in (8, 16, 32)] + [same.astype(np.float32)]
    eye = (r == c).astype(np.float32)
    mats = [diag[0]] + [diag[i] - diag[i - 1] for i in (1, 2, 3)] + [eye, 1.0 - eye, np.where(tri, 0.0, -np.inf)]
    mask_c = np.concatenate([tri, same], axis=0).astype(np.float32)
    mask_r = np.concatenate([same & (r <= c), same], axis=1).astype(np.float32)
    spread = (np.arange(DN_KEY_W)[None, :] // DN_K_DIM == np.arange(DN_HEADS)[:, None]).astype(np.float32)
    spread3 = np.zeros((3, 3, 3, DN_HEADS, DN_KEY_W), np.float32)
    for j in range(3):
        spread3[j, :, j] = spread
    return (jnp.asarray(np.stack(mats), F32), jnp.asarray(mask_c, BF16), jnp.asarray(mask_r, BF16),
            jnp.asarray(spread3.reshape(3, 9 * DN_HEADS, DN_KEY_W), BF16), jnp.asarray(np.tile(spread, (1, 2)), BF16))


def _gdn_kernel(qkv_ref, z_ref, misc_ref, at_ref, cw_ref, alog_c_ref, dtb_c_ref, alog_r_ref, dtb_r_ref,
                ng_ref, fm_ref, mc_ref, mr_ref, sr_ref, sc_ref, out_ref, xbuf_ref, state_ref, *, tb, sb):
    C = GDN_CHUNK
    nck = tb // C
    H = DN_HEADS
    step = pl.program_id(1)

    @pl.when(step == 0)
    def _():
        xbuf_ref[0:SUBLANES, :] = jnp.zeros((SUBLANES, DN_QKV_W), F32)
        state_ref[...] = jnp.zeros(state_ref.shape, F32)

    xbuf_ref[SUBLANES:SUBLANES + tb, :] = qkv_ref[...]
    xe = xbuf_ref[...]
    y = xe[SUBLANES:] * cw_ref[DN_CONV - 1:DN_CONV, :]
    for d in range(1, DN_CONV):
        y = y + pltpu.roll(xe, d, 0)[SUBLANES:] * cw_ref[DN_CONV - 1 - d:DN_CONV - d, :]
    xbuf_ref[0:SUBLANES, :] = xe[tb:tb + SUBLANES]
    y = _silu(y)

    def softplus(v):
        return jnp.maximum(v, 0.0) + jnp.log(1.0 + jnp.exp(-jnp.abs(v)))

    misc = misc_ref[...]
    beta_c = _sigmoid(misc[:, MISC_B:MISC_B + H])
    g_c = -jnp.exp(alog_r_ref[...]) * softplus(misc[:, MISC_A:MISC_A + H] + dtb_r_ref[...])
    g_r = -jnp.exp(alog_c_ref[...]) * softplus(at_ref[...] + dtb_c_ref[...])

    rows = [slice(b * sb, (b + 1) * sb) for b in range(tb // sb)]
    mask_c = mc_ref[...]
    mask_r = mr_ref[...]

    def pieces(v, axis, n=3):
        out, rest = [], v
        for _ in range(n - 1):
            out.append(rest.astype(BF16))
            rest = rest - out[-1].astype(F32)
        return jnp.concatenate(out + [rest.astype(BF16)], axis=axis)

    sums_c = [_dot(mask_c, pieces(g_c[r], 1)) for r in rows]
    sums_c = [s[:, 0:H] + s[:, H:2 * H] + s[:, 2 * H:3 * H] for s in sums_c]
    sums_r = [_dot(pieces(g_r[:, r], 0), mask_r) for r in rows]
    sums_r = [s[0:H] + s[H:2 * H] + s[2 * H:3 * H] for s in sums_r]
    gcum_c = [s[0:sb] for s in sums_c]
    glast_c = [s[sb:2 * sb] for s in sums_c]
    gcum_r = [s[:, 0:sb] for s in sums_r]
    glast_r = [s[:, sb:2 * sb] for s in sums_r]
    blk8, ring_f, eye, off_diag, tri_bias = fm_ref[0], [fm_ref[1], fm_ref[2], fm_ref[3]], fm_ref[4], fm_ref[5], fm_ref[6]

    hs = range(H)
    bh = [(b, h) for b in range(len(rows)) for h in hs]
    ch = range(len(bh))
    def expand(*vs):
        p = pieces(jnp.concatenate(list(vs) + [vs[-1]] * (3 - len(vs)), axis=1), 1)
        return [_dot(p, sr_ref[j]) for j in range(len(vs))]

    head_sums = lambda v: _dot_nt(pieces(v, 1, 2), sc_ref[...])
    yq, yk, yv = y[:, :DN_KEY_W], y[:, DN_KEY_W:2 * DN_KEY_W], y[:, 2 * DN_KEY_W:]
    gc_all = jnp.concatenate(gcum_c, axis=0)
    eg_c = jnp.exp(gc_all)
    rq_x, rk_x, beta_x = expand(lax.rsqrt(head_sums(yq * yq) + RMS_EPS) * (DN_K_DIM ** -0.5),
                                lax.rsqrt(head_sums(yk * yk) + RMS_EPS), beta_c)
    eg_x, egl_x = expand(eg_c, jnp.exp(jnp.concatenate(glast_c, axis=0) - gc_all))
    q_n, k_n = yq * rq_x, yk * rk_x
    kb_all = k_n * beta_x
    k_bf, kb_bf, q_bf = k_n.astype(BF16), kb_all.astype(BF16), q_n.astype(BF16)
    rhs_v, rhs_k = (yv * beta_x).astype(BF16), (kb_all * eg_x).astype(BF16)
    qd_all = (q_n * eg_x).astype(BF16)
    ke_all = (k_n * egl_x).astype(BF16)
    head = lambda a, i: a[rows[bh[i][0]], bh[i][1] * DN_K_DIM:(bh[i][1] + 1) * DN_K_DIM]
    gcs = [gcum_c[b][:, h:h + 1] for b, h in bh]
    kk = [_dot_nt(head(kb_bf, i), head(k_bf, i)) for i in ch]
    qk = [_dot_nt(head(q_bf, i), head(k_bf, i)) for i in ch]
    decay = [jnp.exp((gcs[i] - gcum_r[b][h:h + 1, :]) + tri_bias) for i, (b, h) in enumerate(bh)]
    lower = [kk[i] * (decay[i] * off_diag) for i in ch]
    attn = [(qk[i] * decay[i]).astype(BF16) for i in ch]
    pw = [lower[i] * blk8 for i in ch]
    x_inv = [eye - pw[i] for i in ch]
    for _ in range(2):
        pwb = [p.astype(BF16) for p in pw]
        pw = [_dot(p, p) for p in pwb]
        x_inv = [x_inv[i] + _dot(x_inv[i].astype(BF16), pw[i].astype(BF16)) for i in ch]
    for ring in ring_f:
        xb = [x.astype(BF16) for x in x_inv]
        t1 = [_dot(xb[i], (lower[i] * ring).astype(BF16)).astype(BF16) for i in ch]
        x_inv = [x_inv[i] - _dot(t1[i], xb[i]) for i in ch]
    uw = [_dot(x_inv[i].astype(BF16), jnp.concatenate([head(rhs_v, i), head(rhs_k, i)], axis=1)) for i in ch]
    us = [m[:, :DN_V_DIM] for m in uw]
    wkb = [m[:, DN_V_DIM:].astype(BF16) for m in uw]
    qd = [head(qd_all, i) for i in ch]
    ke = [head(ke_all, i) for i in ch]

    st = [state_ref[h] for h in hs]
    outs = [[] for _ in hs]
    for n in range(nck):
        b, r0 = divmod(n * C, sb)
        r = slice(r0, r0 + C)
        ws = [_dot(jnp.concatenate([wkb[b * H + h][r], qd[b * H + h][r]], axis=0), st[h].astype(BF16)) for h in hs]
        v_nb = [(us[b * H + h][r] - ws[h][0:C]).astype(BF16) for h in hs]
        for h in hs:
            outs[h].append(ws[h][C:2 * C] + _dot(attn[b * H + h][r, r], v_nb[h]))
        cd = [jnp.exp(glast_r[b][h:h + 1, r0:r0 + 1]) for h in hs]
        st = [st[h] * cd[h] + _dot_tn(ke[b * H + h][r], v_nb[h]) for h in hs]
    for h in hs:
        state_ref[h] = st[h]
    o = jnp.concatenate([jnp.concatenate(outs[h], axis=0) for h in hs], axis=1)
    r_o = lax.rsqrt(head_sums(o * o) * (1.0 / DN_V_DIM) + RMS_EPS)
    o = o * expand(r_o)[0] * ng_ref[...]
    out_ref[...] = (o * _silu(z_ref[...])).astype(out_ref.dtype)


def _gdn(qkv, z, misc, misct, conv_w, a_log, dt_bias, norm_g, *, B, S):
    T = B * S
    sb = min(GDN_SOLVE, S)
    tb = min(GDN_SOLVES_PER_STEP * sb, S)
    ns = S // tb
    H = DN_HEADS
    kern = functools.partial(_gdn_kernel, tb=tb, sb=sb)
    consts = _gdn_constants(sb)
    const = lambda *s: pl.BlockSpec(s, lambda b, i: (0,) * len(s))
    return pl.pallas_call(
        kern,
        grid=(B, ns),
        in_specs=[
            pl.BlockSpec((tb, DN_QKV_W), lambda b, i: (b * ns + i, 0)),
            pl.BlockSpec((tb, DN_VAL_W), lambda b, i: (b * ns + i, 0)),
            pl.BlockSpec((tb, MISC_W), lambda b, i: (b * ns + i, 0)),
            pl.BlockSpec((H, tb), lambda b, i: (MISC_A // H, b * ns + i)),
            const(DN_CONV, DN_QKV_W), const(H, 1), const(H, 1), const(1, H), const(1, H), const(1, DN_VAL_W),
        ] + [const(*c.shape) for c in consts],
        out_specs=pl.BlockSpec((tb, DN_VAL_W), lambda b, i: (b * ns + i, 0)),
        out_shape=jax.ShapeDtypeStruct((T, DN_VAL_W), BF16),
        scratch_shapes=[
            pltpu.VMEM((tb + SUBLANES, DN_QKV_W), F32),
            pltpu.VMEM((H, DN_K_DIM, DN_V_DIM), F32),
        ],
        compiler_params=pltpu.CompilerParams(
            dimension_semantics=("parallel", "arbitrary"), vmem_limit_bytes=VMEM_LIMIT),
        name="gdn",
    )(qkv, z, misc, misct, conv_w, a_log.reshape(H, 1), dt_bias.reshape(H, 1),
      a_log.reshape(1, H), dt_bias.reshape(1, H), jnp.tile(norm_g.reshape(1, DN_V_DIM), (1, H)), *consts)


def _out_proj_kernel(a_ref, d_ref, x_ref, wa_ref, wd_ref, g_ref, b_ref, h_ref, *, alpha):
    for r in range(0, a_ref.shape[0], ROW_GROUP):
        rows = slice(r, r + ROW_GROUP)
        mix = _dot(a_ref[rows, :].astype(BF16), wa_ref[...]) + _dot(d_ref[rows, :], wd_ref[...])
        h_ref[rows, :] = _layer_norm_rows(alpha * x_ref[rows, :] + mix, g_ref[...], b_ref[...])


def _out_proj(a, d, x2, wa, wd, g, b, *, alpha, tm):
    T = x2.shape[0]
    const = lambda *s: pl.BlockSpec(s, lambda i: (0,) * len(s))
    return pl.pallas_call(
        functools.partial(_out_proj_kernel, alpha=alpha),
        grid=(T // tm,),
        in_specs=[
            pl.BlockSpec((tm, A_OUT_W), lambda i: (i, 0)),
            pl.BlockSpec((tm, DN_VAL_W), lambda i: (i, 0)),
            pl.BlockSpec((tm, D_MODEL), lambda i: (i, 0)),
            const(A_OUT_W, D_MODEL), const(DN_VAL_W, D_MODEL), const(1, D_MODEL), const(1, D_MODEL),
        ],
        out_specs=pl.BlockSpec((tm, D_MODEL), lambda i: (i, 0)),
        out_shape=jax.ShapeDtypeStruct((T, D_MODEL), F32),
        compiler_params=pltpu.CompilerParams(dimension_semantics=("parallel",), vmem_limit_bytes=VMEM_LIMIT),
        name="out_proj",
    )(a, d, x2, wa, wd, g, b)


def _ffn_kernel(h_ref, halo_ref, wg_ref, wu_ref, cw_ref, cb_ref, wd_ref, g_ref, b_ref, out_ref,
                acc_ref, *, alpha, tiles_per_seq):
    i = pl.program_id(0)
    f = pl.program_id(1)
    nf = pl.num_programs(1)
    hb = h_ref[...].astype(BF16)
    wg = wg_ref[...]
    gate = _dot(hb, wg)
    up = _dot(hb, wu_ref[...])
    halo = _dot(halo_ref[...].astype(BF16), wg)
    halo = jnp.where(i % tiles_per_seq == 0, 0.0, halo)
    ge = jnp.concatenate([halo, gate], axis=0)
    conv = gate * cw_ref[FFN_CONV - 1:FFN_CONV, :]
    for d in range(1, FFN_CONV):
        conv = conv + pltpu.roll(ge, d, 0)[SUBLANES:] * cw_ref[FFN_CONV - 1 - d:FFN_CONV - d, :]
    act = (_silu(conv + cb_ref[...]) * up).astype(BF16)
    part = _dot(act, wd_ref[...])

    @pl.when(f == 0)
    def _():
        acc_ref[...] = part

    @pl.when(f > 0)
    def _():
        acc_ref[...] += part

    @pl.when(f == nf - 1)
    def _():
        out_ref[...] = _layer_norm_rows(alpha * h_ref[...] + acc_ref[...], g_ref[...], b_ref[...])


def _ffn(h, wg, wu, cw, cb, wd, g, b, *, alpha, S, tm, tf):
    T = h.shape[0]
    nf = D_FF // tf
    hs = tm // SUBLANES
    kern = functools.partial(_ffn_kernel, alpha=alpha, tiles_per_seq=S // tm)
    return pl.pallas_call(
        kern,
        grid=(T // tm, nf),
        in_specs=[
            pl.BlockSpec((tm, D_MODEL), lambda i, f: (i, 0)),
            pl.BlockSpec((SUBLANES, D_MODEL), lambda i, f: (jnp.maximum(i * hs - 1, 0), 0)),
            pl.BlockSpec((D_MODEL, tf), lambda i, f: (0, f)),
            pl.BlockSpec((D_MODEL, tf), lambda i, f: (0, f)),
            pl.BlockSpec((FFN_CONV, tf), lambda i, f: (0, f)),
            pl.BlockSpec((1, tf), lambda i, f: (0, f)),
            pl.BlockSpec((tf, D_MODEL), lambda i, f: (f, 0)),
            pl.BlockSpec((1, D_MODEL), lambda i, f: (0, 0)),
            pl.BlockSpec((1, D_MODEL), lambda i, f: (0, 0)),
        ],
        out_specs=pl.BlockSpec((tm, D_MODEL), lambda i, f: (i, 0)),
        out_shape=jax.ShapeDtypeStruct((T, D_MODEL), F32),
        scratch_shapes=[pltpu.VMEM((tm, D_MODEL), F32)],
        compiler_params=pltpu.CompilerParams(
            dimension_semantics=("parallel", "arbitrary"), vmem_limit_bytes=VMEM_LIMIT),
        name="ffn",
    )(h, h, wg, wu, cw, cb, wd, g, b)


def _regroup_w_in(w):
    offs = [0]
    for s in IN_SIZES:
        offs.append(offs[-1] + s)
    cq, ckv, kidx, widx, qkv, z, b, a = (w[:, offs[i]:offs[i + 1]] for i in range(8))
    pad = jnp.zeros((w.shape[0], MISC_W - (IDX_DIM + IDX_HEADS + 2 * DN_HEADS)), w.dtype)
    return jnp.concatenate([cq, ckv, kidx, widx, b, a, pad, qkv, z], axis=1).astype(BF16)


def _layer(x2, p, *, B, S, alpha):
    row = lambda v: v.reshape(1, -1)
    wc = _regroup_w_in(p["w_in"])
    wuk_bd = jnp.einsum("hdc,hg->hdgc", p["w_uk"], jnp.eye(A_HEADS, dtype=F32)).reshape(
        A_HEADS * A_QK_DIM, A_HEADS * A_KV_RANK)
    qlat, qidx, ckv, ckvt, kidx, misc, misct, qkv, z = _in_proj(
        x2, wc, row(p["q_norm_g"]), p["w_uq"].astype(BF16), wuk_bd.astype(BF16), p["w_qidx"].astype(BF16),
        row(p["kv_norm_g"]),
        row(p["kidx_ln_g"]), row(p["kidx_ln_b"]), tm=min(512, S))
    a_out = _dsa(qidx, qlat, misct, kidx, ckv, ckvt, p["w_uv"].transpose(0, 2, 1).astype(BF16),
                 row(p["attn_out_g"]), B=B, S=S)
    d_out = _gdn(qkv, z, misc, misct, p["dn_conv_w"], p["dn_a_log"], p["dn_dt_bias"], p["dn_norm_g"],
                 B=B, S=S)
    w_out = p["w_out"].astype(BF16)
    h = _out_proj(a_out, d_out, x2, w_out[:A_OUT_W], w_out[A_OUT_W:], row(p["ln1_g"]), row(p["ln1_b"]),
                  alpha=alpha, tm=min(512, S))
    w_ffn = p["ffn_w_in"].astype(BF16)
    return _ffn(h, w_ffn[:, :D_FF], w_ffn[:, D_FF:], p["ffn_conv_w"], row(p["ffn_conv_b"]),
                p["ffn_w_down"].astype(BF16), row(p["ln2_g"]), row(p["ln2_b"]),
                alpha=alpha, S=S, tm=min(512, S), tf=D_FF // 2)


_PARAM_NAMES = ("w_in", "q_norm_g", "w_uq", "w_qidx", "kv_norm_g", "w_uk", "w_uv", "kidx_ln_g", "kidx_ln_b",
                "attn_out_g", "dn_conv_w", "dn_a_log", "dn_dt_bias", "dn_norm_g", "w_out", "ln1_g", "ln1_b",
                "ffn_w_in", "ffn_conv_w", "ffn_conv_b", "ffn_w_down", "ln2_g", "ln2_b")


def kernel(x, w_in, q_norm_g, w_uq, w_qidx, kv_norm_g, w_uk, w_uv, kidx_ln_g, kidx_ln_b, attn_out_g, dn_conv_w, dn_a_log, dn_dt_bias, dn_norm_g, w_out, ln1_g, ln1_b, ffn_w_in, ffn_conv_w, ffn_conv_b, ffn_w_down, ln2_g, ln2_b):
    params = (w_in, q_norm_g, w_uq, w_qidx, kv_norm_g, w_uk, w_uv, kidx_ln_g, kidx_ln_b, attn_out_g, dn_conv_w,
              dn_a_log, dn_dt_bias, dn_norm_g, w_out, ln1_g, ln1_b, ffn_w_in, ffn_conv_w, ffn_conv_b, ffn_w_down,
              ln2_g, ln2_b)
    B, S, D = x.shape
    depth = w_in.shape[0]
    alpha = (2 * depth) ** 0.25
    x2 = x.reshape(B * S, D)
    for l in range(depth):
        x2 = _layer(x2, {n: v[l] for n, v in zip(_PARAM_NAMES, params)}, B=B, S=S, alpha=alpha)
    return x2.reshape(B, S, D)
```

```python
import functools

import jax
import jax.numpy as jnp
import numpy as np
from jax import lax
from jax.experimental import pallas as pl
from jax.experimental.pallas import tpu as pltpu

F32 = jnp.float32
BF16 = jnp.bfloat16
I32 = jnp.int32
I16 = jnp.int16

D_MODEL = 1024
A_HEADS = 8
A_QK_DIM = 64
A_V_DIM = 64
A_Q_RANK = 256
A_KV_RANK = 128
IDX_HEADS = 8
IDX_DIM = 64
TOPK_MAX = 256
DN_HEADS = 8
DN_K_DIM = 64
DN_V_DIM = 64
DN_CONV = 4
D_FF = 2816
FFN_CONV = 3
RMS_EPS = 1e-6
LN_EPS = 1e-5

A_OUT_W = A_HEADS * A_V_DIM
DN_KEY_W = DN_HEADS * DN_K_DIM
DN_VAL_W = DN_HEADS * DN_V_DIM
DN_QKV_W = 2 * DN_KEY_W + DN_VAL_W
IN_SIZES = (A_Q_RANK, A_KV_RANK, IDX_DIM, IDX_HEADS, DN_QKV_W, DN_VAL_W, DN_HEADS, DN_HEADS)

MISC_W = 128
MISC_WIDX = IDX_DIM
MISC_B = IDX_DIM + IDX_HEADS
MISC_A = MISC_B + DN_HEADS
PROJ_W = A_Q_RANK + A_KV_RANK + MISC_W + DN_QKV_W + DN_VAL_W

LOG2E = 1.4426950408889634
SUBLANES = 8
PACK16 = 16
KVT_ROWS = A_KV_RANK + PACK16
HALF16 = 2 ** 15
INT_MIN = -(2 ** 31)
NEG_KEY = INT_MIN + 0x00800000
VMEM_LIMIT = 56 * 1024 * 1024

ROW_GROUP = 256
GDN_CHUNK = 64
GDN_SOLVE = 256
GDN_SOLVES_PER_STEP = 1


def _dot(a, b, precision=None):
    return jnp.dot(a, b, preferred_element_type=F32, precision=precision)


def _dot_nt(a, b, precision=None):
    return lax.dot_general(a, b, (((1,), (1,)), ((), ())), preferred_element_type=F32, precision=precision)


def _dot_tn(a, b, precision=None):
    return lax.dot_general(a, b, (((0,), (0,)), ((), ())), preferred_element_type=F32, precision=precision)


def _sigmoid(x):
    return 1.0 / (1.0 + jnp.exp(-x))


def _silu(x):
    return x * _sigmoid(x)


def _layer_norm_rows(v, g, b):
    mu = jnp.mean(v, axis=-1, keepdims=True)
    d = v - mu
    var = jnp.mean(d * d, axis=-1, keepdims=True)
    return d * lax.rsqrt(var + LN_EPS) * g + b


def _rms_norm_rows(v, g):
    return v * lax.rsqrt(jnp.mean(v * v, axis=-1, keepdims=True) + RMS_EPS) * g


def _in_proj_kernel(x_ref, w_ref, qg_ref, wuq_ref, wuk_ref, wqi_ref, kvg_ref, lng_ref, lnb_ref,
                    qlat_ref, qidx_ref, ckv_ref, ckvt_ref, kidx_ref, misc_ref, misct_ref, qkv_ref, z_ref):
    xb = x_ref[...].astype(BF16)
    proj = _dot(xb, w_ref[...])
    o = 0
    c_q = proj[:, o:o + A_Q_RANK]; o += A_Q_RANK
    c_kv = proj[:, o:o + A_KV_RANK]; o += A_KV_RANK
    misc = proj[:, o:o + MISC_W]; o += MISC_W
    qkv_ref[...] = proj[:, o:o + DN_QKV_W]; o += DN_QKV_W
    z_ref[...] = proj[:, o:o + DN_VAL_W]

    cqn_t = _rms_norm_rows(c_q, qg_ref[...]).T.astype(BF16)
    q_t = _dot(wuq_ref[...], cqn_t).astype(BF16)
    qlat_t = _dot(wuk_ref[...], q_t) * (A_QK_DIM ** -0.5 * LOG2E)
    qidx_t = _dot(wqi_ref[...], cqn_t)
    for h in range(A_HEADS):
        qlat_ref[h] = qlat_t[h * A_KV_RANK:(h + 1) * A_KV_RANK, :].astype(BF16)
        qidx_ref[h] = qidx_t[h * IDX_DIM:(h + 1) * IDX_DIM, :].astype(BF16)
    ckv = _rms_norm_rows(c_kv, kvg_ref[...])
    ckv_ref[...] = ckv.astype(BF16)
    ckvt_ref[0:A_KV_RANK, :] = ckv.T.astype(BF16)
    ckvt_ref[A_KV_RANK:KVT_ROWS, :] = jnp.ones((KVT_ROWS - A_KV_RANK, ckv.shape[0]), BF16)
    kidx_ref[...] = _layer_norm_rows(misc[:, :IDX_DIM], lng_ref[...], lnb_ref[...]).astype(BF16)
    misc_ref[...] = misc
    misct_ref[...] = misc.T


def _in_proj(x2, wc, qg, wuq, wuk, wqi, kvg, lng, lnb, *, tm):
    T = x2.shape[0]
    const = lambda *s: pl.BlockSpec(s, lambda i: (0,) * len(s))
    return pl.pallas_call(
        _in_proj_kernel,
        grid=(T // tm,),
        in_specs=[
            pl.BlockSpec((tm, D_MODEL), lambda i: (i, 0)),
            const(D_MODEL, PROJ_W), const(1, A_Q_RANK), const(A_HEADS * A_QK_DIM, A_Q_RANK),
            const(A_HEADS * A_KV_RANK, A_HEADS * A_QK_DIM), const(IDX_HEADS * IDX_DIM, A_Q_RANK),
            const(1, A_KV_RANK), const(1, IDX_DIM), const(1, IDX_DIM),
        ],
        out_specs=[
            pl.BlockSpec((A_HEADS, A_KV_RANK, tm), lambda i: (0, 0, i)),
            pl.BlockSpec((IDX_HEADS, IDX_DIM, tm), lambda i: (0, 0, i)),
            pl.BlockSpec((tm, A_KV_RANK), lambda i: (i, 0)),
            pl.BlockSpec((KVT_ROWS, tm), lambda i: (0, i)),
            pl.BlockSpec((tm, IDX_DIM), lambda i: (i, 0)),
            pl.BlockSpec((tm, MISC_W), lambda i: (i, 0)),
            pl.BlockSpec((MISC_W, tm), lambda i: (0, i)),
            pl.BlockSpec((tm, DN_QKV_W), lambda i: (i, 0)),
            pl.BlockSpec((tm, DN_VAL_W), lambda i: (i, 0)),
        ],
        out_shape=[
            jax.ShapeDtypeStruct((A_HEADS, A_KV_RANK, T), BF16),
            jax.ShapeDtypeStruct((IDX_HEADS, IDX_DIM, T), BF16),
            jax.ShapeDtypeStruct((T, A_KV_RANK), BF16),
            jax.ShapeDtypeStruct((KVT_ROWS, T), BF16),
            jax.ShapeDtypeStruct((T, IDX_DIM), BF16),
            jax.ShapeDtypeStruct((T, MISC_W), F32),
            jax.ShapeDtypeStruct((MISC_W, T), F32),
            jax.ShapeDtypeStruct((T, DN_QKV_W), F32),
            jax.ShapeDtypeStruct((T, DN_VAL_W), F32),
        ],
        compiler_params=pltpu.CompilerParams(dimension_semantics=("parallel",), vmem_limit_bytes=VMEM_LIMIT),
        name="in_proj",
    )(x2, wc, qg, wuq, wuk, wqi, kvg, lng, lnb)


def _dsa_kernel(qidx_ref, qlat_ref, wt_ref, kidx_ref, ckv_ref, ckvt_ref, wuvt_ref, g_ref, out_ref,
                keys_ref, hi_ref, lo_ref, acc_ref, stage_ref, bias_ref, *, qb, kb, rb, slab, topk, idx_bits):
    H = A_HEADS
    j = pl.program_id(1)
    q0 = j * qb
    nkeys = (j + 1) * qb
    nrblk = (nkeys + rb - 1) // rb
    ups = rb // kb
    nchunk = nrblk * ups
    w = wt_ref[...] * (IDX_HEADS ** -0.5 * IDX_DIM ** -0.5)
    row_minus_lane = (lax.broadcasted_iota(I32, (slab, qb), 0) - lax.broadcasted_iota(I32, (slab, qb), 1))

    def skewed(nparts, consume, produce, carry, on_tail=None):
        def chunk(c, k, carry):
            if ups > 1:
                for i in range(nparts):
                    carry = produce(c + 1, (k + 1) % ups, i, consume(c, k, i, carry))
                return carry
            for i in range(nparts):
                carry = consume(c, k, i, carry)
            for i in range(nparts):
                carry = produce(c + 1, k, i, carry)
            return carry

        def step(i, carry):
            for k in range(ups):
                carry = chunk(i * ups + k, k, carry)
            return carry

        for i in range(nparts):
            carry = produce(0, 0, i, carry)
        carry = lax.fori_loop(0, nrblk - 1, step, carry)
        for k in range(ups - 1):
            carry = chunk((nrblk - 1) * ups + k, k, carry)
        last = nchunk - 1

        @pl.when(last * kb < nkeys)
        def _():
            for i in range(nparts):
                consume(last, ups - 1, i, carry)

        if on_tail is not None:
            @pl.when(last * kb >= nkeys)
            def _():
                on_tail(last)
        return carry

    nslab = kb // slab
    hps = H // nslab

    def score_matmuls(c, slot, i, carry):
        kc = kidx_ref[pl.ds(pl.multiple_of(c * kb, kb), kb), :]
        for h in range(i * hps, (i + 1) * hps):
            stage_ref[slot, h] = _dot(kc, qidx_ref[h])
        return carry

    def score_keys(c, slot, i, carry):
        r0 = pl.multiple_of(c * kb, kb) + i * slab
        s = jnp.zeros((slab, qb), F32)
        for h in range(H):
            s = s + jnp.maximum(stage_ref[slot, h, i * slab:(i + 1) * slab, :], 0.0) * w[h:h + 1, :]
        bits = lax.bitcast_convert_type(s, I32)
        key = jnp.where(bits < 0, INT_MIN - bits, bits)
        key = jnp.where(row_minus_lane <= q0 - r0, key, NEG_KEY)
        keys_ref[pl.ds(r0, slab), :] = key
        hi_ref[pl.ds(r0, slab), :] = jnp.right_shift(key, 16).astype(I16)
        lo_ref[pl.ds(r0, slab), :] = ((key & 0xFFFF) - HALF16).astype(I16)
        return carry

    def score_tail(c):
        r0 = pl.multiple_of(c * kb, kb)
        keys_ref[pl.ds(r0, kb), :] = jnp.full((kb, qb), NEG_KEY, I32)
        hi_ref[pl.ds(r0, kb), :] = jnp.full((kb, qb), NEG_KEY >> 16, I16)
        lo_ref[pl.ds(r0, kb), :] = jnp.full((kb, qb), (NEG_KEY & 0xFFFF) - HALF16, I16)

    skewed(nslab, score_keys, score_matmuls, 0, score_tail)

    nacc = 4

    def count_rows(pred):
        def body(r, acc):
            r0 = pl.multiple_of(r * rb, rb)
            k = keys_ref[pl.ds(r0, rb), :]
            m = jnp.where(pred(k, r0), 1, 0).astype(I32)
            return acc + m.reshape(nacc, rb // (nacc * SUBLANES), SUBLANES, qb).sum(axis=1)
        acc = lax.fori_loop(0, nrblk, body, jnp.zeros((nacc, SUBLANES, qb), I32))
        return acc.sum(axis=0).sum(axis=0, keepdims=True)

    def count16(ref, cand):
        def body(r, accs):
            r0 = pl.multiple_of(r * rb, rb)
            m = jnp.where(ref[pl.ds(r0, rb), :] >= cand, jnp.int16(1), jnp.int16(0))
            accs = list(accs)
            for i in range(rb // PACK16):
                accs[i % nacc] = accs[i % nacc] + m[i * PACK16:(i + 1) * PACK16, :]
            return tuple(accs)
        accs = lax.fori_loop(0, nrblk, body, tuple(jnp.zeros((PACK16, qb), I16) for _ in range(nacc)))
        tot = accs[0].astype(I32)
        for a in accs[1:]:
            tot = tot + a.astype(I32)
        return tot.sum(axis=0, keepdims=True)

    def bisect16(ref, target, cnt_all):
        def body(p, carry):
            t_u, cnt_ok, cnt_rej = carry
            cand_u = t_u | jnp.left_shift(jnp.int32(1), 15 - p)
            cnt = count16(ref, (cand_u - HALF16).astype(I16))
            ok = cnt >= target
            return jnp.where(ok, cand_u, t_u), jnp.where(ok, cnt, cnt_ok), jnp.where(ok, cnt_rej, cnt)
        return lax.fori_loop(0, 16, body, (jnp.zeros((1, qb), I32), cnt_all, jnp.zeros((1, qb), I32)))

    hi_u, cnt_ge_hi, cnt_gt_hi = bisect16(hi_ref, topk, jnp.full((1, qb), nrblk * rb, I32))
    hi_t = (hi_u - HALF16).astype(I16)

    def mask_lo(r, carry):
        r0 = pl.multiple_of(r * rb, rb)
        lo_ref[pl.ds(r0, rb), :] = jnp.where(hi_ref[pl.ds(r0, rb), :] == hi_t, lo_ref[pl.ds(r0, rb), :],
                                             jnp.int16(-HALF16))
        return carry

    lax.fori_loop(0, nrblk, mask_lo, 0)
    lo_u, cnt_ge_lo, _ = bisect16(lo_ref, topk - cnt_gt_hi, cnt_ge_hi - cnt_gt_hi)
    t = jnp.left_shift(hi_u - HALF16, 16) + lo_u
    cnt_t = cnt_gt_hi + cnt_ge_lo

    excess = jnp.where(t > NEG_KEY, cnt_t - topk, 0)

    @pl.when(jnp.max(excess) > 0)
    def _():
        keep = topk - count_rows(lambda k, r0: k >= t + 1)

        def idx_body(p, pos):
            cand = pos + jnp.left_shift(jnp.int32(1), idx_bits - 1 - p)
            c = count_rows(lambda k, r0: (k == t) & (
                r0 + lax.broadcasted_iota(I32, (rb, qb), 0) < cand))
            return jnp.where(c < keep, cand, pos)

        pos = lax.fori_loop(0, idx_bits, idx_body, jnp.zeros((1, qb), I32))

        def demote(r, carry):
            r0 = pl.multiple_of(r * rb, rb)
            k = keys_ref[pl.ds(r0, rb), :]
            row = r0 + lax.broadcasted_iota(I32, (rb, qb), 0)
            keys_ref[pl.ds(r0, rb), :] = jnp.where((k == t) & (row > pos), NEG_KEY, k)
            return carry

        lax.fori_loop(0, nrblk, demote, 0)

    thr = jnp.maximum(t, NEG_KEY + 1)

    acc_ref[...] = jnp.zeros(acc_ref.shape, F32)

    def put(carry, h, item):
        return carry[:h] + (item,) + carry[h + 1:]

    def attn_logits(c, slot, h, carry):
        r0 = pl.multiple_of(c * kb, kb)
        if h == 0:
            bias_ref[slot] = jnp.where(keys_ref[pl.ds(r0, kb), :] >= thr, 0.0, -jnp.inf)
        m_prev = carry[h][1]
        lg = _dot(ckv_ref[pl.ds(r0, kb), :], qlat_ref[h]) + bias_ref[slot]
        stage_ref[slot, h] = lg
        return put(carry, h, (m_prev, jnp.maximum(m_prev, jnp.max(lg, axis=0, keepdims=True))))

    def attn_values(c, slot, h, carry):
        m_old, m_new = carry[h]
        kvt = ckvt_ref[:, pl.ds(pl.multiple_of(c * kb, kb), kb)]
        p = jnp.exp2(stage_ref[slot, h] - m_new).astype(BF16)
        acc_ref[h] = acc_ref[h] * jnp.exp2(m_old - m_new) + _dot(kvt, p)
        return carry

    m_init = jnp.full((1, qb), -1e30, F32)
    skewed(H, attn_values, attn_logits, ((m_init, m_init),) * H)

    out_t = jnp.concatenate(
        [_dot(wuvt_ref[h], (acc_ref[h, 0:A_KV_RANK, :] / acc_ref[h, A_KV_RANK:A_KV_RANK + 1, :]).astype(BF16))
         for h in range(H)], axis=0)
    out_ref[...] = _rms_norm_rows(out_t.T, g_ref[...])


def _dsa(qidx, qlat, misct, kidx, ckv, ckvt, wuvt, g, *, B, S):
    T = B * S
    qb = min(256, S)
    nq = S // qb
    kb = qb
    rb = 2 * qb if nq % 2 == 0 else qb
    topk = min(TOPK_MAX, S // 4)
    idx_bits = max(1, (S - 1).bit_length())
    slab = min(64, kb)
    kern = functools.partial(_dsa_kernel, qb=qb, kb=kb, rb=rb, slab=slab, topk=topk, idx_bits=idx_bits)
    return pl.pallas_call(
        kern,
        grid=(B, nq),
        in_specs=[
            pl.BlockSpec((IDX_HEADS, IDX_DIM, qb), lambda b, j: (0, 0, b * nq + j)),
            pl.BlockSpec((A_HEADS, A_KV_RANK, qb), lambda b, j: (0, 0, b * nq + j)),
            pl.BlockSpec((IDX_HEADS, qb), lambda b, j: (MISC_WIDX // IDX_HEADS, b * nq + j)),
            pl.BlockSpec((S, IDX_DIM), lambda b, j: (b, 0)),
            pl.BlockSpec((S, A_KV_RANK), lambda b, j: (b, 0)),
            pl.BlockSpec((KVT_ROWS, S), lambda b, j: (0, b)),
            pl.BlockSpec((A_HEADS, A_V_DIM, A_KV_RANK), lambda b, j: (0, 0, 0)),
            pl.BlockSpec((1, A_OUT_W), lambda b, j: (0, 0)),
        ],
        out_specs=pl.BlockSpec((qb, A_OUT_W), lambda b, j: (b * nq + j, 0)),
        out_shape=jax.ShapeDtypeStruct((T, A_OUT_W), F32),
        scratch_shapes=[
            pltpu.VMEM((S, qb), I32),
            pltpu.VMEM((S, qb), I16),
            pltpu.VMEM((S, qb), I16),
            pltpu.VMEM((A_HEADS, KVT_ROWS, qb), F32),
            pltpu.VMEM((rb // kb, A_HEADS, kb, qb), F32),
            pltpu.VMEM((rb // kb, kb, qb), F32),
        ],
        compiler_params=pltpu.CompilerParams(
            dimension_semantics=("parallel", "arbitrary"), vmem_limit_bytes=VMEM_LIMIT),
        name="dsa",
    )(qidx, qlat, misct, kidx, ckv, ckvt, wuvt, g)


def _gdn_constants(sb):
    r, c = np.arange(sb)[:, None], np.arange(sb)[None, :]
    same = (r // GDN_CHUNK) == (c // GDN_CHUNK)
    tri = same & (c <= r)
    diag = [((r // b) == (c // b)).astype(np.float32) for b in (8, 16, 32)] + [same.astype(np.float32)]
    eye = (r == c).astype(np.float32)
    mats = [diag[0]] + [diag[i] - diag[i - 1] for i in (1, 2, 3)] + [eye, 1.0 - eye, np.where(tri, 0.0, -np.inf)]
    mask_c = np.concatenate([tri, same], axis=0).astype(np.float32)
    mask_r = np.concatenate([same & (r <= c), same], axis=1).astype(np.float32)
    spread = (np.arange(DN_KEY_W)[None, :] // DN_K_DIM == np.arange(DN_HEADS)[:, None]).astype(np.float32)
    spread3 = np.zeros((3, 3, 3, DN_HEADS, DN_KEY_W), np.float32)
    for j in range(3):
        spread3[j, :, j] = spread
    return (jnp.asarray(np.stack(mats), F32), jnp.asarray(mask_c, BF16), jnp.asarray(mask_r, BF16),
            jnp.asarray(spread3.reshape(3, 9 * DN_HEADS, DN_KEY_W), BF16), jnp.asarray(np.tile(spread, (1, 2)), BF16))


def _gdn_kernel(qkv_ref, z_ref, misc_ref, at_ref, cw_ref, alog_c_ref, dtb_c_ref, alog_r_ref, dtb_r_ref,
                ng_ref, fm_ref, mc_ref, mr_ref, sr_ref, sc_ref, out_ref, xbuf_ref, state_ref, *, tb, sb):
    C = GDN_CHUNK
    nck = tb // C
    H = DN_HEADS
    step = pl.program_id(1)

    @pl.when(step == 0)
    def _():
        xbuf_ref[0:SUBLANES, :] = jnp.zeros((SUBLANES, DN_QKV_W), F32)
        state_ref[...] = jnp.zeros(state_ref.shape, F32)

    xbuf_ref[SUBLANES:SUBLANES + tb, :] = qkv_ref[...]
    xe = xbuf_ref[...]
    y = xe[SUBLANES:] * cw_ref[DN_CONV - 1:DN_CONV, :]
    for d in range(1, DN_CONV):
        y = y + pltpu.roll(xe, d, 0)[SUBLANES:] * cw_ref[DN_CONV - 1 - d:DN_CONV - d, :]
    xbuf_ref[0:SUBLANES, :] = xe[tb:tb + SUBLANES]
    y = _silu(y)

    def softplus(v):
        return jnp.maximum(v, 0.0) + jnp.log(1.0 + jnp.exp(-jnp.abs(v)))

    misc = misc_ref[...]
    beta_c = _sigmoid(misc[:, MISC_B:MISC_B + H])
    g_c = -jnp.exp(alog_r_ref[...]) * softplus(misc[:, MISC_A:MISC_A + H] + dtb_r_ref[...])
    g_r = -jnp.exp(alog_c_ref[...]) * softplus(at_ref[...] + dtb_c_ref[...])

    rows = [slice(b * sb, (b + 1) * sb) for b in range(tb // sb)]
    mask_c = mc_ref[...]
    mask_r = mr_ref[...]

    def pieces(v, axis, n=3):
        out, rest = [], v
        for _ in range(n - 1):
            out.append(rest.astype(BF16))
            rest = rest - out[-1].astype(F32)
        return jnp.concatenate(out + [rest.astype(BF16)], axis=axis)

    sums_c = [_dot(mask_c, pieces(g_c[r], 1)) for r in rows]
    sums_c = [s[:, 0:H] + s[:, H:2 * H] + s[:, 2 * H:3 * H] for s in sums_c]
    sums_r = [_dot(pieces(g_r[:, r], 0), mask_r) for r in rows]
    sums_r = [s[0:H] + s[H:2 * H] + s[2 * H:3 * H] for s in sums_r]
    gcum_c = [s[0:sb] for s in sums_c]
    glast_c = [s[sb:2 * sb] for s in sums_c]
    gcum_r = [s[:, 0:sb] for s in sums_r]
    glast_r = [s[:, sb:2 * sb] for s in sums_r]
    blk8, ring_f, eye, off_diag, tri_bias = fm_ref[0], [fm_ref[1], fm_ref[2], fm_ref[3]], fm_ref[4], fm_ref[5], fm_ref[6]

    hs = range(H)
    bh = [(b, h) for b in range(len(rows)) for h in hs]
    ch = range(len(bh))
    def expand(*vs):
        p = pieces(jnp.concatenate(list(vs) + [vs[-1]] * (3 - len(vs)), axis=1), 1)
        return [_dot(p, sr_ref[j]) for j in range(len(vs))]

    head_sums = lambda v: _dot_nt(pieces(v, 1, 2), sc_ref[...])
    yq, yk, yv = y[:, :DN_KEY_W], y[:, DN_KEY_W:2 * DN_KEY_W], y[:, 2 * DN_KEY_W:]
    gc_all = jnp.concatenate(gcum_c, axis=0)
    eg_c = jnp.exp(gc_all)
    rq_x, rk_x, beta_x = expand(lax.rsqrt(head_sums(yq * yq) + RMS_EPS) * (DN_K_DIM ** -0.5),
                                lax.rsqrt(head_sums(yk * yk) + RMS_EPS), beta_c)
    eg_x, egl_x = expand(eg_c, jnp.exp(jnp.concatenate(glast_c, axis=0) - gc_all))
    q_n, k_n = yq * rq_x, yk * rk_x
    kb_all = k_n * beta_x
    k_bf, kb_bf, q_bf = k_n.astype(BF16), kb_all.astype(BF16), q_n.astype(BF16)
    rhs_v, rhs_k = (yv * beta_x).astype(BF16), (kb_all * eg_x).astype(BF16)
    qd_all = (q_n * eg_x).astype(BF16)
    ke_all = (k_n * egl_x).astype(BF16)
    head = lambda a, i: a[rows[bh[i][0]], bh[i][1] * DN_K_DIM:(bh[i][1] + 1) * DN_K_DIM]
    gcs = [gcum_c[b][:, h:h + 1] for b, h in bh]
    kk = [_dot_nt(head(kb_bf, i), head(k_bf, i)) for i in ch]
    qk = [_dot_nt(head(q_bf, i), head(k_bf, i)) for i in ch]
    decay = [jnp.exp((gcs[i] - gcum_r[b][h:h + 1, :]) + tri_bias) for i, (b, h) in enumerate(bh)]
    lower = [kk[i] * (decay[i] * off_diag) for i in ch]
    attn = [(qk[i] * decay[i]).astype(BF16) for i in ch]
    pw = [lower[i] * blk8 for i in ch]
    x_inv = [eye - pw[i] for i in ch]
    for _ in range(2):
        pwb = [p.astype(BF16) for p in pw]
        pw = [_dot(p, p) for p in pwb]
        x_inv = [x_inv[i] + _dot(x_inv[i].astype(BF16), pw[i].astype(BF16)) for i in ch]
    for ring in ring_f:
        xb = [x.astype(BF16) for x in x_inv]
        t1 = [_dot(xb[i], (lower[i] * ring).astype(BF16)).astype(BF16) for i in ch]
        x_inv = [x_inv[i] - _dot(t1[i], xb[i]) for i in ch]
    uw = [_dot(x_inv[i].astype(BF16), jnp.concatenate([head(rhs_v, i), head(rhs_k, i)], axis=1)) for i in ch]
    us = [m[:, :DN_V_DIM] for m in uw]
    wkb = [m[:, DN_V_DIM:].astype(BF16) for m in uw]
    qd = [head(qd_all, i) for i in ch]
    ke = [head(ke_all, i) for i in ch]

    st = [state_ref[h] for h in hs]
    outs = [[] for _ in hs]
    for n in range(nck):
        b, r0 = divmod(n * C, sb)
        r = slice(r0, r0 + C)
        ws = [_dot(jnp.concatenate([wkb[b * H + h][r], qd[b * H + h][r]], axis=0), st[h].astype(BF16)) for h in hs]
        v_nb = [(us[b * H + h][r] - ws[h][0:C]).astype(BF16) for h in hs]
        for h in hs:
            outs[h].append(ws[h][C:2 * C] + _dot(attn[b * H + h][r, r], v_nb[h]))
        cd = [jnp.exp(glast_r[b][h:h + 1, r0:r0 + 1]) for h in hs]
        st = [st[h] * cd[h] + _dot_tn(ke[b * H + h][r], v_nb[h]) for h in hs]
    for h in hs:
        state_ref[h] = st[h]
    o = jnp.concatenate([jnp.concatenate(outs[h], axis=0) for h in hs], axis=1)
    r_o = lax.rsqrt(head_sums(o * o) * (1.0 / DN_V_DIM) + RMS_EPS)
    o = o * expand(r_o)[0] * ng_ref[...]
    out_ref[...] = (o * _silu(z_ref[...])).astype(out_ref.dtype)


def _gdn(qkv, z, misc, misct, conv_w, a_log, dt_bias, norm_g, *, B, S):
    T = B * S
    sb = min(GDN_SOLVE, S)
    tb = min(GDN_SOLVES_PER_STEP * sb, S)
    ns = S // tb
    H = DN_HEADS
    kern = functools.partial(_gdn_kernel, tb=tb, sb=sb)
    consts = _gdn_constants(sb)
    const = lambda *s: pl.BlockSpec(s, lambda b, i: (0,) * len(s))
    return pl.pallas_call(
        kern,
        grid=(B, ns),
        in_specs=[
            pl.BlockSpec((tb, DN_QKV_W), lambda b, i: (b * ns + i, 0)),
            pl.BlockSpec((tb, DN_VAL_W), lambda b, i: (b * ns + i, 0)),
            pl.BlockSpec((tb, MISC_W), lambda b, i: (b * ns + i, 0)),
            pl.BlockSpec((H, tb), lambda b, i: (MISC_A // H, b * ns + i)),
            const(DN_CONV, DN_QKV_W), const(H, 1), const(H, 1), const(1, H), const(1, H), const(1, DN_VAL_W),
        ] + [const(*c.shape) for c in consts],
        out_specs=pl.BlockSpec((tb, DN_VAL_W), lambda b, i: (b * ns + i, 0)),
        out_shape=jax.ShapeDtypeStruct((T, DN_VAL_W), BF16),
        scratch_shapes=[
            pltpu.VMEM((tb + SUBLANES, DN_QKV_W), F32),
            pltpu.VMEM((H, DN_K_DIM, DN_V_DIM), F32),
        ],
        compiler_params=pltpu.CompilerParams(
            dimension_semantics=("parallel", "arbitrary"), vmem_limit_bytes=VMEM_LIMIT),
        name="gdn",
    )(qkv, z, misc, misct, conv_w, a_log.reshape(H, 1), dt_bias.reshape(H, 1),
      a_log.reshape(1, H), dt_bias.reshape(1, H), jnp.tile(norm_g.reshape(1, DN_V_DIM), (1, H)), *consts)


def _out_proj_kernel(a_ref, d_ref, x_ref, wa_ref, wd_ref, g_ref, b_ref, h_ref, *, alpha):
    for r in range(0, a_ref.shape[0], ROW_GROUP):
        rows = slice(r, r + ROW_GROUP)
        mix = _dot(a_ref[rows, :].astype(BF16), wa_ref[...]) + _dot(d_ref[rows, :], wd_ref[...])
        h_ref[rows, :] = _layer_norm_rows(alpha * x_ref[rows, :] + mix, g_ref[...], b_ref[...])


def _out_proj(a, d, x2, wa, wd, g, b, *, alpha, tm):
    T = x2.shape[0]
    const = lambda *s: pl.BlockSpec(s, lambda i: (0,) * len(s))
    return pl.pallas_call(
        functools.partial(_out_proj_kernel, alpha=alpha),
        grid=(T // tm,),
        in_specs=[
            pl.BlockSpec((tm, A_OUT_W), lambda i: (i, 0)),
            pl.BlockSpec((tm, DN_VAL_W), lambda i: (i, 0)),
            pl.BlockSpec((tm, D_MODEL), lambda i: (i, 0)),
            const(A_OUT_W, D_MODEL), const(DN_VAL_W, D_MODEL), const(1, D_MODEL), const(1, D_MODEL),
        ],
        out_specs=pl.BlockSpec((tm, D_MODEL), lambda i: (i, 0)),
        out_shape=jax.ShapeDtypeStruct((T, D_MODEL), F32),
        compiler_params=pltpu.CompilerParams(dimension_semantics=("parallel",), vmem_limit_bytes=VMEM_LIMIT),
        name="out_proj",
    )(a, d, x2, wa, wd, g, b)


def _ffn_kernel(h_ref, halo_ref, wg_ref, wu_ref, cw_ref, cb_ref, wd_ref, g_ref, b_ref, out_ref,
                acc_ref, *, alpha, tiles_per_seq):
    i = pl.program_id(0)
    f = pl.program_id(1)
    nf = pl.num_programs(1)
    hb = h_ref[...].astype(BF16)
    wg = wg_ref[...]
    gate = _dot(hb, wg)
    up = _dot(hb, wu_ref[...])
    halo = _dot(halo_ref[...].astype(BF16), wg)
    halo = jnp.where(i % tiles_per_seq == 0, 0.0, halo)
    ge = jnp.concatenate([halo, gate], axis=0)
    conv = gate * cw_ref[FFN_CONV - 1:FFN_CONV, :]
    for d in range(1, FFN_CONV):
        conv = conv + pltpu.roll(ge, d, 0)[SUBLANES:] * cw_ref[FFN_CONV - 1 - d:FFN_CONV - d, :]
    act = (_silu(conv + cb_ref[...]) * up).astype(BF16)
    part = _dot(act, wd_ref[...])

    @pl.when(f == 0)
    def _():
        acc_ref[...] = part

    @pl.when(f > 0)
    def _():
        acc_ref[...] += part

    @pl.when(f == nf - 1)
    def _():
        out_ref[...] = _layer_norm_rows(alpha * h_ref[...] + acc_ref[...], g_ref[...], b_ref[...])


def _ffn(h, wg, wu, cw, cb, wd, g, b, *, alpha, S, tm, tf):
    T = h.shape[0]
    nf = D_FF // tf
    hs = tm // SUBLANES
    kern = functools.partial(_ffn_kernel, alpha=alpha, tiles_per_seq=S // tm)
    return pl.pallas_call(
        kern,
        grid=(T // tm, nf),
        in_specs=[
            pl.BlockSpec((tm, D_MODEL), lambda i, f: (i, 0)),
            pl.BlockSpec((SUBLANES, D_MODEL), lambda i, f: (jnp.maximum(i * hs - 1, 0), 0)),
            pl.BlockSpec((D_MODEL, tf), lambda i, f: (0, f)),
            pl.BlockSpec((D_MODEL, tf), lambda i, f: (0, f)),
            pl.BlockSpec((FFN_CONV, tf), lambda i, f: (0, f)),
            pl.BlockSpec((1, tf), lambda i, f: (0, f)),
            pl.BlockSpec((tf, D_MODEL), lambda i, f: (f, 0)),
            pl.BlockSpec((1, D_MODEL), lambda i, f: (0, 0)),
            pl.BlockSpec((1, D_MODEL), lambda i, f: (0, 0)),
        ],
        out_specs=pl.BlockSpec((tm, D_MODEL), lambda i, f: (i, 0)),
        out_shape=jax.ShapeDtypeStruct((T, D_MODEL), F32),
        scratch_shapes=[pltpu.VMEM((tm, D_MODEL), F32)],
        compiler_params=pltpu.CompilerParams(
            dimension_semantics=("parallel", "arbitrary"), vmem_limit_bytes=VMEM_LIMIT),
        name="ffn",
    )(h, h, wg, wu, cw, cb, wd, g, b)


def _regroup_w_in(w):
    offs = [0]
    for s in IN_SIZES:
        offs.append(offs[-1] + s)
    cq, ckv, kidx, widx, qkv, z, b, a = (w[:, offs[i]:offs[i + 1]] for i in range(8))
    pad = jnp.zeros((w.shape[0], MISC_W - (IDX_DIM + IDX_HEADS + 2 * DN_HEADS)), w.dtype)
    return jnp.concatenate([cq, ckv, kidx, widx, b, a, pad, qkv, z], axis=1).astype(BF16)


def _layer(x2, p, *, B, S, alpha):
    row = lambda v: v.reshape(1, -1)
    wc = _regroup_w_in(p["w_in"])
    wuk_bd_t = jnp.einsum("hdc,hg->hcgd", p["w_uk"], jnp.eye(A_HEADS, dtype=F32)).reshape(
        A_HEADS * A_KV_RANK, A_HEADS * A_QK_DIM)
    qlat, qidx, ckv, ckvt, kidx, misc, misct, qkv, z = _in_proj(
        x2, wc, row(p["q_norm_g"]), p["w_uq"].T.astype(BF16), wuk_bd_t.astype(BF16), p["w_qidx"].T.astype(BF16),
        row(p["kv_norm_g"]),
        row(p["kidx_ln_g"]), row(p["kidx_ln_b"]), tm=min(512, S))
    a_out = _dsa(qidx, qlat, misct, kidx, ckv, ckvt, p["w_uv"].transpose(0, 2, 1).astype(BF16),
                 row(p["attn_out_g"]), B=B, S=S)
    d_out = _gdn(qkv, z, misc, misct, p["dn_conv_w"], p["dn_a_log"], p["dn_dt_bias"], p["dn_norm_g"],
                 B=B, S=S)
    w_out = p["w_out"].astype(BF16)
    h = _out_proj(a_out, d_out, x2, w_out[:A_OUT_W], w_out[A_OUT_W:], row(p["ln1_g"]), row(p["ln1_b"]),
                  alpha=alpha, tm=min(512, S))
    w_ffn = p["ffn_w_in"].astype(BF16)
    return _ffn(h, w_ffn[:, :D_FF], w_ffn[:, D_FF:], p["ffn_conv_w"], row(p["ffn_conv_b"]),
                p["ffn_w_down"].astype(BF16), row(p["ln2_g"]), row(p["ln2_b"]),
                alpha=alpha, S=S, tm=min(512, S), tf=D_FF // 2)


_PARAM_NAMES = ("w_in", "q_norm_g", "w_uq", "w_qidx", "kv_norm_g", "w_uk", "w_uv", "kidx_ln_g", "kidx_ln_b",
                "attn_out_g", "dn_conv_w", "dn_a_log", "dn_dt_bias", "dn_norm_g", "w_out", "ln1_g", "ln1_b",
                "ffn_w_in", "ffn_conv_w", "ffn_conv_b", "ffn_w_down", "ln2_g", "ln2_b")


def kernel(x, w_in, q_norm_g, w_uq, w_qidx, kv_norm_g, w_uk, w_uv, kidx_ln_g, kidx_ln_b, attn_out_g, dn_conv_w, dn_a_log, dn_dt_bias, dn_norm_g, w_out, ln1_g, ln1_b, ffn_w_in, ffn_conv_w, ffn_conv_b, ffn_w_down, ln2_g, ln2_b):
    params = (w_in, q_norm_g, w_uq, w_qidx, kv_norm_g, w_uk, w_uv, kidx_ln_g, kidx_ln_b, attn_out_g, dn_conv_w,
              dn_a_log, dn_dt_bias, dn_norm_g, w_out, ln1_g, ln1_b, ffn_w_in, ffn_conv_w, ffn_conv_b, ffn_w_down,
              ln2_g, ln2_b)
    B, S, D = x.shape
    depth = w_in.shape[0]
    alpha = (2 * depth) ** 0.25
    x2 = x.reshape(B * S, D)
    for l in range(depth):
        x2 = _layer(x2, {n: v[l] for n, v in zip(_PARAM_NAMES, params)}, B=B, S=S, alpha=alpha)
    return x2.reshape(B, S, D)
```

```python
import functools

import jax
import jax.numpy as jnp
import numpy as np
from jax import lax
from jax.experimental import pallas as pl
from jax.experimental.pallas import tpu as pltpu

F32 = jnp.float32
BF16 = jnp.bfloat16
I32 = jnp.int32
I16 = jnp.int16

D_MODEL = 1024
A_HEADS = 8
A_QK_DIM = 64
A_V_DIM = 64
A_Q_RANK = 256
A_KV_RANK = 128
IDX_HEADS = 8
IDX_DIM = 64
TOPK_MAX = 256
DN_HEADS = 8
DN_K_DIM = 64
DN_V_DIM = 64
DN_CONV = 4
D_FF = 2816
FFN_CONV = 3
RMS_EPS = 1e-6
LN_EPS = 1e-5

A_OUT_W = A_HEADS * A_V_DIM
DN_KEY_W = DN_HEADS * DN_K_DIM
DN_VAL_W = DN_HEADS * DN_V_DIM
DN_QKV_W = 2 * DN_KEY_W + DN_VAL_W
IN_SIZES = (A_Q_RANK, A_KV_RANK, IDX_DIM, IDX_HEADS, DN_QKV_W, DN_VAL_W, DN_HEADS, DN_HEADS)

MISC_W = 128
MISC_WIDX = IDX_DIM
MISC_B = IDX_DIM + IDX_HEADS
MISC_A = MISC_B + DN_HEADS
PROJ_W = A_Q_RANK + A_KV_RANK + MISC_W + DN_QKV_W + DN_VAL_W

LOG2E = 1.4426950408889634
SUBLANES = 8
PACK16 = 16
KVT_ROWS = A_KV_RANK + PACK16
HALF16 = 2 ** 15
INT_MIN = -(2 ** 31)
NEG_KEY = INT_MIN + 0x00800000
VMEM_LIMIT = 56 * 1024 * 1024

ROW_GROUP = 256
GDN_CHUNK = 64
GDN_SOLVE = 256
GDN_SOLVES_PER_STEP = 1


def _dot(a, b, precision=None):
    return jnp.dot(a, b, preferred_element_type=F32, precision=precision)


def _dot_nt(a, b, precision=None):
    return lax.dot_general(a, b, (((1,), (1,)), ((), ())), preferred_element_type=F32, precision=precision)


def _dot_tn(a, b, precision=None):
    return lax.dot_general(a, b, (((0,), (0,)), ((), ())), preferred_element_type=F32, precision=precision)


def _sigmoid(x):
    return 1.0 / (1.0 + jnp.exp(-x))


def _silu(x):
    return x * _sigmoid(x)


def _layer_norm_rows(v, g, b):
    mu = jnp.mean(v, axis=-1, keepdims=True)
    d = v - mu
    var = jnp.mean(d * d, axis=-1, keepdims=True)
    return d * lax.rsqrt(var + LN_EPS) * g + b


def _rms_norm_rows(v, g):
    return v * lax.rsqrt(jnp.mean(v * v, axis=-1, keepdims=True) + RMS_EPS) * g


def _in_proj_kernel(x_ref, w_ref, qg_ref, wuq_ref, wuk_ref, wqi_ref, kvg_ref, lng_ref, lnb_ref,
                    qlat_ref, qidx_ref, ckv_ref, ckvt_ref, kidx_ref, misc_ref, misct_ref, qkv_ref, z_ref):
    xb = x_ref[...].astype(BF16)
    proj = _dot(xb, w_ref[...])
    o = 0
    c_q = proj[:, o:o + A_Q_RANK]; o += A_Q_RANK
    c_kv = proj[:, o:o + A_KV_RANK]; o += A_KV_RANK
    misc = proj[:, o:o + MISC_W]; o += MISC_W
    qkv_ref[...] = proj[:, o:o + DN_QKV_W]; o += DN_QKV_W
    z_ref[...] = proj[:, o:o + DN_VAL_W]

    cqn_t = _rms_norm_rows(c_q, qg_ref[...]).T.astype(BF16)
    q_t = _dot(wuq_ref[...], cqn_t).astype(BF16)
    qlat_t = _dot(wuk_ref[...], q_t) * (A_QK_DIM ** -0.5 * LOG2E)
    qidx_t = _dot(wqi_ref[...], cqn_t)
    for h in range(A_HEADS):
        qlat_ref[h] = qlat_t[h * A_KV_RANK:(h + 1) * A_KV_RANK, :].astype(BF16)
        qidx_ref[h] = qidx_t[h * IDX_DIM:(h + 1) * IDX_DIM, :].astype(BF16)
    ckv = _rms_norm_rows(c_kv, kvg_ref[...])
    ckv_ref[...] = ckv.astype(BF16)
    ckvt_ref[0:A_KV_RANK, :] = ckv.T.astype(BF16)
    ckvt_ref[A_KV_RANK:KVT_ROWS, :] = jnp.ones((KVT_ROWS - A_KV_RANK, ckv.shape[0]), BF16)
    kidx_ref[...] = _layer_norm_rows(misc[:, :IDX_DIM], lng_ref[...], lnb_ref[...]).astype(BF16)
    misc_ref[...] = misc
    misct_ref[...] = misc.T


def _in_proj(x2, wc, qg, wuq, wuk, wqi, kvg, lng, lnb, *, tm):
    T = x2.shape[0]
    const = lambda *s: pl.BlockSpec(s, lambda i: (0,) * len(s))
    return pl.pallas_call(
        _in_proj_kernel,
        grid=(T // tm,),
        in_specs=[
            pl.BlockSpec((tm, D_MODEL), lambda i: (i, 0)),
            const(D_MODEL, PROJ_W), const(1, A_Q_RANK), const(A_HEADS * A_QK_DIM, A_Q_RANK),
            const(A_HEADS * A_KV_RANK, A_HEADS * A_QK_DIM), const(IDX_HEADS * IDX_DIM, A_Q_RANK),
            const(1, A_KV_RANK), const(1, IDX_DIM), const(1, IDX_DIM),
        ],
        out_specs=[
            pl.BlockSpec((A_HEADS, A_KV_RANK, tm), lambda i: (0, 0, i)),
            pl.BlockSpec((IDX_HEADS, IDX_DIM, tm), lambda i: (0, 0, i)),
            pl.BlockSpec((tm, A_KV_RANK), lambda i: (i, 0)),
            pl.BlockSpec((KVT_ROWS, tm), lambda i: (0, i)),
            pl.BlockSpec((tm, IDX_DIM), lambda i: (i, 0)),
            pl.BlockSpec((tm, MISC_W), lambda i: (i, 0)),
            pl.BlockSpec((MISC_W, tm), lambda i: (0, i)),
            pl.BlockSpec((tm, DN_QKV_W), lambda i: (i, 0)),
            pl.BlockSpec((tm, DN_VAL_W), lambda i: (i, 0)),
        ],
        out_shape=[
            jax.ShapeDtypeStruct((A_HEADS, A_KV_RANK, T), BF16),
            jax.ShapeDtypeStruct((IDX_HEADS, IDX_DIM, T), BF16),
            jax.ShapeDtypeStruct((T, A_KV_RANK), BF16),
            jax.ShapeDtypeStruct((KVT_ROWS, T), BF16),
            jax.ShapeDtypeStruct((T, IDX_DIM), BF16),
            jax.ShapeDtypeStruct((T, MISC_W), F32),
            jax.ShapeDtypeStruct((MISC_W, T), F32),
            jax.ShapeDtypeStruct((T, DN_QKV_W), F32),
            jax.ShapeDtypeStruct((T, DN_VAL_W), F32),
        ],
        compiler_params=pltpu.CompilerParams(dimension_semantics=("parallel",), vmem_limit_bytes=VMEM_LIMIT),
        name="in_proj",
    )(x2, wc, qg, wuq, wuk, wqi, kvg, lng, lnb)


def _dsa_kernel(qidx_ref, qlat_ref, wt_ref, kidx_ref, ckv_ref, ckvt_ref, wuvt_ref, g_ref, out_ref,
                keys_ref, hi_ref, lo_ref, acc_ref, stage_ref, bias_ref, *, qb, kb, rb, slab, topk, idx_bits):
    H = A_HEADS
    j = pl.program_id(1)
    q0 = j * qb
    nkeys = (j + 1) * qb
    nrblk = (nkeys + rb - 1) // rb
    ups = rb // kb
    nchunk = nrblk * ups
    w = wt_ref[...] * (IDX_HEADS ** -0.5 * IDX_DIM ** -0.5)
    row_minus_lane = (lax.broadcasted_iota(I32, (slab, qb), 0) - lax.broadcasted_iota(I32, (slab, qb), 1))

    def skewed(nparts, consume, produce, carry, on_tail=None):
        def chunk(c, k, carry):
            if ups > 1:
                for i in range(nparts):
                    carry = produce(c + 1, (k + 1) % ups, i, consume(c, k, i, carry))
                return carry
            for i in range(nparts):
                carry = consume(c, k, i, carry)
            for i in range(nparts):
                carry = produce(c + 1, k, i, carry)
            return carry

        def step(i, carry):
            for k in range(ups):
                carry = chunk(i * ups + k, k, carry)
            return carry

        for i in range(nparts):
            carry = produce(0, 0, i, carry)
        carry = lax.fori_loop(0, nrblk - 1, step, carry)
        for k in range(ups - 1):
            carry = chunk((nrblk - 1) * ups + k, k, carry)
        last = nchunk - 1

        @pl.when(last * kb < nkeys)
        def _():
            for i in range(nparts):
                consume(last, ups - 1, i, carry)

        if on_tail is not None:
            @pl.when(last * kb >= nkeys)
            def _():
                on_tail(last)
        return carry

    nslab = kb // slab
    hps = H // nslab

    def score_matmuls(c, slot, i, carry):
        kc = kidx_ref[pl.ds(pl.multiple_of(c * kb, kb), kb), :]
        for h in range(i * hps, (i + 1) * hps):
            stage_ref[slot, h] = _dot(kc, qidx_ref[h])
        return carry

    def score_keys(c, slot, i, carry):
        r0 = pl.multiple_of(c * kb, kb) + i * slab
        s = jnp.zeros((slab, qb), F32)
        for h in range(H):
            s = s + jnp.maximum(stage_ref[slot, h, i * slab:(i + 1) * slab, :], 0.0) * w[h:h + 1, :]
        bits = lax.bitcast_convert_type(s, I32)
        key = jnp.where(bits < 0, INT_MIN - bits, bits)
        key = jnp.where(row_minus_lane <= q0 - r0, key, NEG_KEY)
        keys_ref[pl.ds(r0, slab), :] = key
        hi_ref[pl.ds(r0, slab), :] = jnp.right_shift(key, 16).astype(I16)
        lo_ref[pl.ds(r0, slab), :] = ((key & 0xFFFF) - HALF16).astype(I16)
        return carry

    def score_tail(c):
        r0 = pl.multiple_of(c * kb, kb)
        keys_ref[pl.ds(r0, kb), :] = jnp.full((kb, qb), NEG_KEY, I32)
        hi_ref[pl.ds(r0, kb), :] = jnp.full((kb, qb), NEG_KEY >> 16, I16)
        lo_ref[pl.ds(r0, kb), :] = jnp.full((kb, qb), (NEG_KEY & 0xFFFF) - HALF16, I16)

    skewed(nslab, score_keys, score_matmuls, 0, score_tail)

    nacc = 4

    def count16(ref, cand, below=False):
        def body(r, accs):
            r0 = pl.multiple_of(r * rb, rb)
            k = ref[pl.ds(r0, rb), :]
            m = jnp.where(k < cand if below else k >= cand, jnp.int16(1), jnp.int16(0))
            accs = list(accs)
            for i in range(rb // PACK16):
                accs[i % nacc] = accs[i % nacc] + m[i * PACK16:(i + 1) * PACK16, :]
            return tuple(accs)
        accs = lax.fori_loop(0, nrblk, body, tuple(jnp.zeros((PACK16, qb), I16) for _ in range(nacc)))
        tot = accs[0].astype(I32)
        for a in accs[1:]:
            tot = tot + a.astype(I32)
        return tot.sum(axis=0, keepdims=True)

    def bisect16(ref, target, cnt_all):
        def body(p, carry):
            t_u, cnt_ok, cnt_rej = carry
            cand_u = t_u | jnp.left_shift(jnp.int32(1), 15 - p)
            cnt = count16(ref, (cand_u - HALF16).astype(I16))
            ok = cnt >= target
            return jnp.where(ok, cand_u, t_u), jnp.where(ok, cnt, cnt_ok), jnp.where(ok, cnt_rej, cnt)
        return lax.fori_loop(0, 16, body, (jnp.zeros((1, qb), I32), cnt_all, jnp.zeros((1, qb), I32)))

    hi_u, cnt_ge_hi, cnt_gt_hi = bisect16(hi_ref, topk, jnp.full((1, qb), nrblk * rb, I32))
    hi_t = (hi_u - HALF16).astype(I16)

    def mask_lo(r, carry):
        r0 = pl.multiple_of(r * rb, rb)
        lo_ref[pl.ds(r0, rb), :] = jnp.where(hi_ref[pl.ds(r0, rb), :] == hi_t, lo_ref[pl.ds(r0, rb), :],
                                             jnp.int16(-HALF16))
        return carry

    lax.fori_loop(0, nrblk, mask_lo, 0)
    lo_u, cnt_ge_lo, cnt_gt_lo = bisect16(lo_ref, topk - cnt_gt_hi, cnt_ge_hi - cnt_gt_hi)
    t = jnp.left_shift(hi_u - HALF16, 16) + lo_u
    cnt_t = cnt_gt_hi + cnt_ge_lo

    excess = jnp.where(t > NEG_KEY, cnt_t - topk, 0)

    @pl.when(jnp.max(excess) > 0)
    def _():
        keep = topk - (cnt_gt_hi + cnt_gt_lo)
        lo_t = (lo_u - HALF16).astype(I16)

        def tie_rows(r, carry):
            r0 = pl.multiple_of(r * rb, rb)
            tie = (hi_ref[pl.ds(r0, rb), :] == hi_t) & (lo_ref[pl.ds(r0, rb), :] == lo_t)
            row = (r0 + lax.broadcasted_iota(I32, (rb, qb), 0)).astype(I16)
            lo_ref[pl.ds(r0, rb), :] = jnp.where(tie, row, jnp.int16(HALF16 - 1))
            return carry

        lax.fori_loop(0, nrblk, tie_rows, 0)

        def idx_body(p, pos):
            cand = pos + jnp.left_shift(jnp.int32(1), idx_bits - 1 - p)
            return jnp.where(count16(lo_ref, cand.astype(I16), below=True) < keep, cand, pos)

        pos = lax.fori_loop(0, idx_bits, idx_body, jnp.zeros((1, qb), I32))

        def demote(r, carry):
            r0 = pl.multiple_of(r * rb, rb)
            k = keys_ref[pl.ds(r0, rb), :]
            row = r0 + lax.broadcasted_iota(I32, (rb, qb), 0)
            keys_ref[pl.ds(r0, rb), :] = jnp.where((k == t) & (row > pos), NEG_KEY, k)
            return carry

        lax.fori_loop(0, nrblk, demote, 0)

    thr = jnp.maximum(t, NEG_KEY + 1)

    acc_ref[...] = jnp.zeros(acc_ref.shape, F32)

    def put(carry, h, item):
        return carry[:h] + (item,) + carry[h + 1:]

    def attn_logits(c, slot, h, carry):
        r0 = pl.multiple_of(c * kb, kb)
        if h == 0:
            bias_ref[slot] = jnp.where(keys_ref[pl.ds(r0, kb), :] >= thr, 0.0, -jnp.inf)
        m_prev = carry[h][1]
        lg = _dot(ckv_ref[pl.ds(r0, kb), :], qlat_ref[h]) + bias_ref[slot]
        stage_ref[slot, h] = lg
        return put(carry, h, (m_prev, jnp.maximum(m_prev, jnp.max(lg, axis=0, keepdims=True))))

    def attn_values(c, slot, h, carry):
        m_old, m_new = carry[h]
        kvt = ckvt_ref[:, pl.ds(pl.multiple_of(c * kb, kb), kb)]
        p = jnp.exp2(stage_ref[slot, h] - m_new).astype(BF16)
        acc_ref[h] = acc_ref[h] * jnp.exp2(m_old - m_new) + _dot(kvt, p)
        return carry

    m_init = jnp.full((1, qb), -1e30, F32)
    skewed(H, attn_values, attn_logits, ((m_init, m_init),) * H)

    out_t = jnp.concatenate(
        [_dot(wuvt_ref[h], (acc_ref[h, 0:A_KV_RANK, :] / acc_ref[h, A_KV_RANK:A_KV_RANK + 1, :]).astype(BF16))
         for h in range(H)], axis=0)
    out_ref[...] = _rms_norm_rows(out_t.T, g_ref[...])


def _dsa(qidx, qlat, misct, kidx, ckv, ckvt, wuvt, g, *, B, S):
    T = B * S
    qb = min(256, S)
    nq = S // qb
    kb = qb
    rb = 2 * qb if nq % 2 == 0 else qb
    topk = min(TOPK_MAX, S // 4)
    idx_bits = max(1, (S - 1).bit_length())
    assert S < HALF16, "tie-breaking keeps key positions in int16"
    slab = min(64, kb)
    kern = functools.partial(_dsa_kernel, qb=qb, kb=kb, rb=rb, slab=slab, topk=topk, idx_bits=idx_bits)
    return pl.pallas_call(
        kern,
        grid=(B, nq),
        in_specs=[
            pl.BlockSpec((IDX_HEADS, IDX_DIM, qb), lambda b, j: (0, 0, b * nq + j)),
            pl.BlockSpec((A_HEADS, A_KV_RANK, qb), lambda b, j: (0, 0, b * nq + j)),
            pl.BlockSpec((IDX_HEADS, qb), lambda b, j: (MISC_WIDX // IDX_HEADS, b * nq + j)),
            pl.BlockSpec((S, IDX_DIM), lambda b, j: (b, 0)),
            pl.BlockSpec((S, A_KV_RANK), lambda b, j: (b, 0)),
            pl.BlockSpec((KVT_ROWS, S), lambda b, j: (0, b)),
            pl.BlockSpec((A_HEADS, A_V_DIM, A_KV_RANK), lambda b, j: (0, 0, 0)),
            pl.BlockSpec((1, A_OUT_W), lambda b, j: (0, 0)),
        ],
        out_specs=pl.BlockSpec((qb, A_OUT_W), lambda b, j: (b * nq + j, 0)),
        out_shape=jax.ShapeDtypeStruct((T, A_OUT_W), F32),
        scratch_shapes=[
            pltpu.VMEM((S, qb), I32),
            pltpu.VMEM((S, qb), I16),
            pltpu.VMEM((S, qb), I16),
            pltpu.VMEM((A_HEADS, KVT_ROWS, qb), F32),
            pltpu.VMEM((rb // kb, A_HEADS, kb, qb), F32),
            pltpu.VMEM((rb // kb, kb, qb), F32),
        ],
        compiler_params=pltpu.CompilerParams(
            dimension_semantics=("parallel", "arbitrary"), vmem_limit_bytes=VMEM_LIMIT),
        name="dsa",
    )(qidx, qlat, misct, kidx, ckv, ckvt, wuvt, g)


def _gdn_constants(sb):
    r, c = np.arange(sb)[:, None], np.arange(sb)[None, :]
    same = (r // GDN_CHUNK) == (c // GDN_CHUNK)
    tri = same & (c <= r)
    diag = [((r // b) == (c // b)).astype(np.float32) for b in (8, 16, 32)] + [same.astype(np.float32)]
    eye = (r == c).astype(np.float32)
    mats = [diag[0]] + [diag[i] - diag[i - 1] for i in (1, 2, 3)] + [eye, 1.0 - eye, np.where(tri, 0.0, -np.inf)]
    mask_c = np.concatenate([tri, same], axis=0).astype(np.float32)
    mask_r = np.concatenate([same & (r <= c), same], axis=1).astype(np.float32)
    spread = (np.arange(DN_KEY_W)[None, :] // DN_K_DIM == np.arange(DN_HEADS)[:, None]).astype(np.float32)
    spread3 = np.zeros((3, 3, 3, DN_HEADS, DN_KEY_W), np.float32)
    for j in range(3):
        spread3[j, :, j] = spread
    return (jnp.asarray(np.stack(mats), F32), jnp.asarray(mask_c, BF16), jnp.asarray(mask_r, BF16),
            jnp.asarray(spread3.reshape(3, 9 * DN_HEADS, DN_KEY_W), BF16), jnp.asarray(np.tile(spread, (1, 2)), BF16))


def _gdn_kernel(qkv_ref, z_ref, misc_ref, at_ref, cw_ref, alog_c_ref, dtb_c_ref, alog_r_ref, dtb_r_ref,
                ng_ref, fm_ref, mc_ref, mr_ref, sr_ref, sc_ref, out_ref, xbuf_ref, state_ref, *, tb, sb):
    C = GDN_CHUNK
    nck = tb // C
    H = DN_HEADS
    step = pl.program_id(1)

    @pl.when(step == 0)
    def _():
        xbuf_ref[0:SUBLANES, :] = jnp.zeros((SUBLANES, DN_QKV_W), F32)
        state_ref[...] = jnp.zeros(state_ref.shape, F32)

    xbuf_ref[SUBLANES:SUBLANES + tb, :] = qkv_ref[...]
    xe = xbuf_ref[...]
    y = xe[SUBLANES:] * cw_ref[DN_CONV - 1:DN_CONV, :]
    for d in range(1, DN_CONV):
        y = y + pltpu.roll(xe, d, 0)[SUBLANES:] * cw_ref[DN_CONV - 1 - d:DN_CONV - d, :]
    xbuf_ref[0:SUBLANES, :] = xe[tb:tb + SUBLANES]
    y = _silu(y)

    def softplus(v):
        return jnp.maximum(v, 0.0) + jnp.log(1.0 + jnp.exp(-jnp.abs(v)))

    misc = misc_ref[...]
    beta_c = _sigmoid(misc[:, MISC_B:MISC_B + H])
    g_c = -jnp.exp(alog_r_ref[...]) * softplus(misc[:, MISC_A:MISC_A + H] + dtb_r_ref[...])
    g_r = -jnp.exp(alog_c_ref[...]) * softplus(at_ref[...] + dtb_c_ref[...])

    rows = [slice(b * sb, (b + 1) * sb) for b in range(tb // sb)]
    mask_c = mc_ref[...]
    mask_r = mr_ref[...]

    def pieces(v, axis, n=3):
        out, rest = [], v
        for _ in range(n - 1):
            out.append(rest.astype(BF16))
            rest = rest - out[-1].astype(F32)
        return jnp.concatenate(out + [rest.astype(BF16)], axis=axis)

    sums_c = [_dot(mask_c, pieces(g_c[r], 1)) for r in rows]
    sums_c = [s[:, 0:H] + s[:, H:2 * H] + s[:, 2 * H:3 * H] for s in sums_c]
    sums_r = [_dot(pieces(g_r[:, r], 0), mask_r) for r in rows]
    sums_r = [s[0:H] + s[H:2 * H] + s[2 * H:3 * H] for s in sums_r]
    gcum_c = [s[0:sb] for s in sums_c]
    glast_c = [s[sb:2 * sb] for s in sums_c]
    gcum_r = [s[:, 0:sb] for s in sums_r]
    glast_r = [s[:, sb:2 * sb] for s in sums_r]
    blk8, ring_f, eye, off_diag, tri_bias = fm_ref[0], [fm_ref[1], fm_ref[2], fm_ref[3]], fm_ref[4], fm_ref[5], fm_ref[6]

    hs = range(H)
    bh = [(b, h) for b in range(len(rows)) for h in hs]
    ch = range(len(bh))
    def expand(*vs):
        p = pieces(jnp.concatenate(list(vs) + [vs[-1]] * (3 - len(vs)), axis=1), 1)
        return [_dot(p, sr_ref[j]) for j in range(len(vs))]

    head_sums = lambda v: _dot_nt(pieces(v, 1, 2), sc_ref[...])
    yq, yk, yv = y[:, :DN_KEY_W], y[:, DN_KEY_W:2 * DN_KEY_W], y[:, 2 * DN_KEY_W:]
    gc_all = jnp.concatenate(gcum_c, axis=0)
    eg_c = jnp.exp(gc_all)
    rq_x, rk_x, beta_x = expand(lax.rsqrt(head_sums(yq * yq) + RMS_EPS) * (DN_K_DIM ** -0.5),
                                lax.rsqrt(head_sums(yk * yk) + RMS_EPS), beta_c)
    eg_x, egl_x = expand(eg_c, jnp.exp(jnp.concatenate(glast_c, axis=0) - gc_all))
    q_n, k_n = yq * rq_x, yk * rk_x
    kb_all = k_n * beta_x
    k_bf, kb_bf, q_bf = k_n.astype(BF16), kb_all.astype(BF16), q_n.astype(BF16)
    rhs_v, rhs_k = (yv * beta_x).astype(BF16), (kb_all * eg_x).astype(BF16)
    qd_all = (q_n * eg_x).astype(BF16)
    ke_all = (k_n * egl_x).astype(BF16)
    head = lambda a, i: a[rows[bh[i][0]], bh[i][1] * DN_K_DIM:(bh[i][1] + 1) * DN_K_DIM]
    gcs = [gcum_c[b][:, h:h + 1] for b, h in bh]
    kk = [_dot_nt(head(kb_bf, i), head(k_bf, i)) for i in ch]
    qk = [_dot_nt(head(q_bf, i), head(k_bf, i)) for i in ch]
    decay = [jnp.exp((gcs[i] - gcum_r[b][h:h + 1, :]) + tri_bias) for i, (b, h) in enumerate(bh)]
    lower = [kk[i] * (decay[i] * off_diag) for i in ch]
    attn = [(qk[i] * decay[i]).astype(BF16) for i in ch]
    pw = [lower[i] * blk8 for i in ch]
    x_inv = [eye - pw[i] for i in ch]
    for _ in range(2):
        pwb = [p.astype(BF16) for p in pw]
        pw = [_dot(p, p) for p in pwb]
        x_inv = [x_inv[i] + _dot(x_inv[i].astype(BF16), pw[i].astype(BF16)) for i in ch]
    for ring in ring_f:
        xb = [x.astype(BF16) for x in x_inv]
        t1 = [_dot(xb[i], (lower[i] * ring).astype(BF16)).astype(BF16) for i in ch]
        x_inv = [x_inv[i] - _dot(t1[i], xb[i]) for i in ch]
    uw = [_dot(x_inv[i].astype(BF16), jnp.concatenate([head(rhs_v, i), head(rhs_k, i)], axis=1)) for i in ch]
    us = [m[:, :DN_V_DIM] for m in uw]
    wkb = [m[:, DN_V_DIM:].astype(BF16) for m in uw]
    qd = [head(qd_all, i) for i in ch]
    ke = [head(ke_all, i) for i in ch]

    st = [state_ref[h] for h in hs]
    outs = [[] for _ in hs]
    for n in range(nck):
        b, r0 = divmod(n * C, sb)
        r = slice(r0, r0 + C)
        ws = [_dot(jnp.concatenate([wkb[b * H + h][r], qd[b * H + h][r]], axis=0), st[h].astype(BF16)) for h in hs]
        v_nb = [(us[b * H + h][r] - ws[h][0:C]).astype(BF16) for h in hs]
        for h in hs:
            outs[h].append(ws[h][C:2 * C] + _dot(attn[b * H + h][r, r], v_nb[h]))
        cd = [jnp.exp(glast_r[b][h:h + 1, r0:r0 + 1]) for h in hs]
        st = [st[h] * cd[h] + _dot_tn(ke[b * H + h][r], v_nb[h]) for h in hs]
    for h in hs:
        state_ref[h] = st[h]
    o = jnp.concatenate([jnp.concatenate(outs[h], axis=0) for h in hs], axis=1)
    r_o = lax.rsqrt(head_sums(o * o) * (1.0 / DN_V_DIM) + RMS_EPS)
    o = o * expand(r_o)[0] * ng_ref[...]
    out_ref[...] = (o * _silu(z_ref[...])).astype(out_ref.dtype)


def _gdn(qkv, z, misc, misct, conv_w, a_log, dt_bias, norm_g, *, B, S):
    T = B * S
    sb = min(GDN_SOLVE, S)
    tb = min(GDN_SOLVES_PER_STEP * sb, S)
    ns = S // tb
    H = DN_HEADS
    kern = functools.partial(_gdn_kernel, tb=tb, sb=sb)
    consts = _gdn_constants(sb)
    const = lambda *s: pl.BlockSpec(s, lambda b, i: (0,) * len(s))
    return pl.pallas_call(
        kern,
        grid=(B, ns),
        in_specs=[
            pl.BlockSpec((tb, DN_QKV_W), lambda b, i: (b * ns + i, 0)),
            pl.BlockSpec((tb, DN_VAL_W), lambda b, i: (b * ns + i, 0)),
            pl.BlockSpec((tb, MISC_W), lambda b, i: (b * ns + i, 0)),
            pl.BlockSpec((H, tb), lambda b, i: (MISC_A // H, b * ns + i)),
            const(DN_CONV, DN_QKV_W), const(H, 1), const(H, 1), const(1, H), const(1, H), const(1, DN_VAL_W),
        ] + [const(*c.shape) for c in consts],
        out_specs=pl.BlockSpec((tb, DN_VAL_W), lambda b, i: (b * ns + i, 0)),
        out_shape=jax.ShapeDtypeStruct((T, DN_VAL_W), BF16),
        scratch_shapes=[
            pltpu.VMEM((tb + SUBLANES, DN_QKV_W), F32),
            pltpu.VMEM((H, DN_K_DIM, DN_V_DIM), F32),
        ],
        compiler_params=pltpu.CompilerParams(
            dimension_semantics=("parallel", "arbitrary"), vmem_limit_bytes=VMEM_LIMIT),
        name="gdn",
    )(qkv, z, misc, misct, conv_w, a_log.reshape(H, 1), dt_bias.reshape(H, 1),
      a_log.reshape(1, H), dt_bias.reshape(1, H), jnp.tile(norm_g.reshape(1, DN_V_DIM), (1, H)), *consts)


def _out_proj_kernel(a_ref, d_ref, x_ref, wa_ref, wd_ref, g_ref, b_ref, h_ref, *, alpha):
    for r in range(0, a_ref.shape[0], ROW_GROUP):
        rows = slice(r, r + ROW_GROUP)
        mix = _dot(a_ref[rows, :].astype(BF16), wa_ref[...]) + _dot(d_ref[rows, :], wd_ref[...])
        h_ref[rows, :] = _layer_norm_rows(alpha * x_ref[rows, :] + mix, g_ref[...], b_ref[...])


def _out_proj(a, d, x2, wa, wd, g, b, *, alpha, tm):
    T = x2.shape[0]
    const = lambda *s: pl.BlockSpec(s, lambda i: (0,) * len(s))
    return pl.pallas_call(
        functools.partial(_out_proj_kernel, alpha=alpha),
        grid=(T // tm,),
        in_specs=[
            pl.BlockSpec((tm, A_OUT_W), lambda i: (i, 0)),
            pl.BlockSpec((tm, DN_VAL_W), lambda i: (i, 0)),
            pl.BlockSpec((tm, D_MODEL), lambda i: (i, 0)),
            const(A_OUT_W, D_MODEL), const(DN_VAL_W, D_MODEL), const(1, D_MODEL), const(1, D_MODEL),
        ],
        out_specs=pl.BlockSpec((tm, D_MODEL), lambda i: (i, 0)),
        out_shape=jax.ShapeDtypeStruct((T, D_MODEL), F32),
        compiler_params=pltpu.CompilerParams(dimension_semantics=("parallel",), vmem_limit_bytes=VMEM_LIMIT),
        name="out_proj",
    )(a, d, x2, wa, wd, g, b)


def _ffn_kernel(h_ref, halo_ref, wg_ref, wu_ref, cw_ref, cb_ref, wd_ref, g_ref, b_ref, out_ref,
                acc_ref, *, alpha, tiles_per_seq):
    i = pl.program_id(0)
    f = pl.program_id(1)
    nf = pl.num_programs(1)
    hb = h_ref[...].astype(BF16)
    wg = wg_ref[...]
    gate = _dot(hb, wg)
    up = _dot(hb, wu_ref[...])
    halo = _dot(halo_ref[...].astype(BF16), wg)
    halo = jnp.where(i % tiles_per_seq == 0, 0.0, halo)
    ge = jnp.concatenate([halo, gate], axis=0)
    conv = gate * cw_ref[FFN_CONV - 1:FFN_CONV, :]
    for d in range(1, FFN_CONV):
        conv = conv + pltpu.roll(ge, d, 0)[SUBLANES:] * cw_ref[FFN_CONV - 1 - d:FFN_CONV - d, :]
    act = (_silu(conv + cb_ref[...]) * up).astype(BF16)
    part = _dot(act, wd_ref[...])

    @pl.when(f == 0)
    def _():
        acc_ref[...] = part

    @pl.when(f > 0)
    def _():
        acc_ref[...] += part

    @pl.when(f == nf - 1)
    def _():
        out_ref[...] = _layer_norm_rows(alpha * h_ref[...] + acc_ref[...], g_ref[...], b_ref[...])


def _ffn(h, wg, wu, cw, cb, wd, g, b, *, alpha, S, tm, tf):
    T = h.shape[0]
    nf = D_FF // tf
    hs = tm // SUBLANES
    kern = functools.partial(_ffn_kernel, alpha=alpha, tiles_per_seq=S // tm)
    return pl.pallas_call(
        kern,
        grid=(T // tm, nf),
        in_specs=[
            pl.BlockSpec((tm, D_MODEL), lambda i, f: (i, 0)),
            pl.BlockSpec((SUBLANES, D_MODEL), lambda i, f: (jnp.maximum(i * hs - 1, 0), 0)),
            pl.BlockSpec((D_MODEL, tf), lambda i, f: (0, f)),
            pl.BlockSpec((D_MODEL, tf), lambda i, f: (0, f)),
            pl.BlockSpec((FFN_CONV, tf), lambda i, f: (0, f)),
            pl.BlockSpec((1, tf), lambda i, f: (0, f)),
            pl.BlockSpec((tf, D_MODEL), lambda i, f: (f, 0)),
            pl.BlockSpec((1, D_MODEL), lambda i, f: (0, 0)),
            pl.BlockSpec((1, D_MODEL), lambda i, f: (0, 0)),
        ],
        out_specs=pl.BlockSpec((tm, D_MODEL), lambda i, f: (i, 0)),
        out_shape=jax.ShapeDtypeStruct((T, D_MODEL), F32),
        scratch_shapes=[pltpu.VMEM((tm, D_MODEL), F32)],
        compiler_params=pltpu.CompilerParams(
            dimension_semantics=("parallel", "arbitrary"), vmem_limit_bytes=VMEM_LIMIT),
        name="ffn",
    )(h, h, wg, wu, cw, cb, wd, g, b)


def _regroup_w_in(w):
    offs = [0]
    for s in IN_SIZES:
        offs.append(offs[-1] + s)
    cq, ckv, kidx, widx, qkv, z, b, a = (w[:, offs[i]:offs[i + 1]] for i in range(8))
    pad = jnp.zeros((w.shape[0], MISC_W - (IDX_DIM + IDX_HEADS + 2 * DN_HEADS)), w.dtype)
    return jnp.concatenate([cq, ckv, kidx, widx, b, a, pad, qkv, z], axis=1).astype(BF16)


def _layer(x2, p, *, B, S, alpha):
    row = lambda v: v.reshape(1, -1)
    wc = _regroup_w_in(p["w_in"])
    wuk_bd_t = jnp.einsum("hdc,hg->hcgd", p["w_uk"], jnp.eye(A_HEADS, dtype=F32)).reshape(
        A_HEADS * A_KV_RANK, A_HEADS * A_QK_DIM)
    qlat, qidx, ckv, ckvt, kidx, misc, misct, qkv, z = _in_proj(
        x2, wc, row(p["q_norm_g"]), p["w_uq"].T.astype(BF16), wuk_bd_t.astype(BF16), p["w_qidx"].T.astype(BF16),
        row(p["kv_norm_g"]),
        row(p["kidx_ln_g"]), row(p["kidx_ln_b"]), tm=min(512, S))
    a_out = _dsa(qidx, qlat, misct, kidx, ckv, ckvt, p["w_uv"].transpose(0, 2, 1).astype(BF16),
                 row(p["attn_out_g"]), B=B, S=S)
    d_out = _gdn(qkv, z, misc, misct, p["dn_conv_w"], p["dn_a_log"], p["dn_dt_bias"], p["dn_norm_g"],
                 B=B, S=S)
    w_out = p["w_out"].astype(BF16)
    h = _out_proj(a_out, d_out, x2, w_out[:A_OUT_W], w_out[A_OUT_W:], row(p["ln1_g"]), row(p["ln1_b"]),
                  alpha=alpha, tm=min(512, S))
    w_ffn = p["ffn_w_in"].astype(BF16)
    return _ffn(h, w_ffn[:, :D_FF], w_ffn[:, D_FF:], p["ffn_conv_w"], row(p["ffn_conv_b"]),
                p["ffn_w_down"].astype(BF16), row(p["ln2_g"]), row(p["ln2_b"]),
                alpha=alpha, S=S, tm=min(512, S), tf=D_FF // 2)


_PARAM_NAMES = ("w_in", "q_norm_g", "w_uq", "w_qidx", "kv_norm_g", "w_uk", "w_uv", "kidx_ln_g", "kidx_ln_b",
                "attn_out_g", "dn_conv_w", "dn_a_log", "dn_dt_bias", "dn_norm_g", "w_out", "ln1_g", "ln1_b",
                "ffn_w_in", "ffn_conv_w", "ffn_conv_b", "ffn_w_down", "ln2_g", "ln2_b")


def kernel(x, w_in, q_norm_g, w_uq, w_qidx, kv_norm_g, w_uk, w_uv, kidx_ln_g, kidx_ln_b, attn_out_g, dn_conv_w, dn_a_log, dn_dt_bias, dn_norm_g, w_out, ln1_g, ln1_b, ffn_w_in, ffn_conv_w, ffn_conv_b, ffn_w_down, ln2_g, ln2_b):
    params = (w_in, q_norm_g, w_uq, w_qidx, kv_norm_g, w_uk, w_uv, kidx_ln_g, kidx_ln_b, attn_out_g, dn_conv_w,
              dn_a_log, dn_dt_bias, dn_norm_g, w_out, ln1_g, ln1_b, ffn_w_in, ffn_conv_w, ffn_conv_b, ffn_w_down,
              ln2_g, ln2_b)
    B, S, D = x.shape
    depth = w_in.shape[0]
    alpha = (2 * depth) ** 0.25
    x2 = x.reshape(B * S, D)
    for l in range(depth):
        x2 = _layer(x2, {n: v[l] for n, v in zip(_PARAM_NAMES, params)}, B=B, S=S, alpha=alpha)
    return x2.reshape(B, S, D)
```

```python
import functools

import jax
import jax.numpy as jnp
import numpy as np
from jax import lax
from jax.experimental import pallas as pl
from jax.experimental.pallas import tpu as pltpu

F32 = jnp.float32
BF16 = jnp.bfloat16
I32 = jnp.int32
I16 = jnp.int16

D_MODEL = 1024
A_HEADS = 8
A_QK_DIM = 64
A_V_DIM = 64
A_Q_RANK = 256
A_KV_RANK = 128
IDX_HEADS = 8
IDX_DIM = 64
TOPK_MAX = 256
DN_HEADS = 8
DN_K_DIM = 64
DN_V_DIM = 64
DN_CONV = 4
D_FF = 2816
FFN_CONV = 3
RMS_EPS = 1e-6
LN_EPS = 1e-5

A_OUT_W = A_HEADS * A_V_DIM
DN_KEY_W = DN_HEADS * DN_K_DIM
DN_VAL_W = DN_HEADS * DN_V_DIM
DN_QKV_W = 2 * DN_KEY_W + DN_VAL_W
IN_SIZES = (A_Q_RANK, A_KV_RANK, IDX_DIM, IDX_HEADS, DN_QKV_W, DN_VAL_W, DN_HEADS, DN_HEADS)

MISC_W = 128
MISC_WIDX = IDX_DIM
MISC_B = IDX_DIM + IDX_HEADS
MISC_A = MISC_B + DN_HEADS
PROJ_W = A_Q_RANK + A_KV_RANK + MISC_W + DN_QKV_W + DN_VAL_W

LOG2E = 1.4426950408889634
SUBLANES = 8
PACK16 = 16
KVT_ROWS = A_KV_RANK + PACK16
HALF16 = 2 ** 15
INT_MIN = -(2 ** 31)
NEG_KEY = INT_MIN + 0x00800000
VMEM_LIMIT = 56 * 1024 * 1024

TIE_SCAN_MAX = 8
ROW_GROUP = 256
GDN_CHUNK = 64
GDN_SOLVE = 256
GDN_SOLVES_PER_STEP = 1


def _dot(a, b, precision=None):
    return jnp.dot(a, b, preferred_element_type=F32, precision=precision)


def _dot_nt(a, b, precision=None):
    return lax.dot_general(a, b, (((1,), (1,)), ((), ())), preferred_element_type=F32, precision=precision)


def _dot_tn(a, b, precision=None):
    return lax.dot_general(a, b, (((0,), (0,)), ((), ())), preferred_element_type=F32, precision=precision)


def _sigmoid(x):
    return 1.0 / (1.0 + jnp.exp(-x))


def _silu(x):
    return x * _sigmoid(x)


def _layer_norm_rows(v, g, b):
    mu = jnp.mean(v, axis=-1, keepdims=True)
    d = v - mu
    var = jnp.mean(d * d, axis=-1, keepdims=True)
    return d * lax.rsqrt(var + LN_EPS) * g + b


def _rms_norm_rows(v, g):
    return v * lax.rsqrt(jnp.mean(v * v, axis=-1, keepdims=True) + RMS_EPS) * g


def _in_proj_kernel(x_ref, w_ref, qg_ref, wuq_ref, wuk_ref, wqi_ref, kvg_ref, lng_ref, lnb_ref,
                    qlat_ref, qidx_ref, ckv_ref, ckvt_ref, kidx_ref, misc_ref, misct_ref, qkv_ref, z_ref):
    xb = x_ref[...].astype(BF16)
    proj = _dot(xb, w_ref[...])
    o = 0
    c_q = proj[:, o:o + A_Q_RANK]; o += A_Q_RANK
    c_kv = proj[:, o:o + A_KV_RANK]; o += A_KV_RANK
    misc = proj[:, o:o + MISC_W]; o += MISC_W
    qkv_ref[...] = proj[:, o:o + DN_QKV_W]; o += DN_QKV_W
    z_ref[...] = proj[:, o:o + DN_VAL_W]

    cqn_t = _rms_norm_rows(c_q, qg_ref[...]).T.astype(BF16)
    q_t = _dot(wuq_ref[...], cqn_t).astype(BF16)
    qlat_t = _dot(wuk_ref[...], q_t) * (A_QK_DIM ** -0.5 * LOG2E)
    qidx_t = _dot(wqi_ref[...], cqn_t)
    for h in range(A_HEADS):
        qlat_ref[h] = qlat_t[h * A_KV_RANK:(h + 1) * A_KV_RANK, :].astype(BF16)
        qidx_ref[h] = qidx_t[h * IDX_DIM:(h + 1) * IDX_DIM, :].astype(BF16)
    ckv = _rms_norm_rows(c_kv, kvg_ref[...])
    ckv_ref[...] = ckv.astype(BF16)
    ckvt_ref[0:A_KV_RANK, :] = ckv.T.astype(BF16)
    ckvt_ref[A_KV_RANK:KVT_ROWS, :] = jnp.ones((KVT_ROWS - A_KV_RANK, ckv.shape[0]), BF16)
    kidx_ref[...] = _layer_norm_rows(misc[:, :IDX_DIM], lng_ref[...], lnb_ref[...]).astype(BF16)
    misc_ref[...] = misc
    misct_ref[...] = misc.T


def _in_proj(x2, wc, qg, wuq, wuk, wqi, kvg, lng, lnb, *, tm):
    T = x2.shape[0]
    const = lambda *s: pl.BlockSpec(s, lambda i: (0,) * len(s))
    return pl.pallas_call(
        _in_proj_kernel,
        grid=(T // tm,),
        in_specs=[
            pl.BlockSpec((tm, D_MODEL), lambda i: (i, 0)),
            const(D_MODEL, PROJ_W), const(1, A_Q_RANK), const(A_HEADS * A_QK_DIM, A_Q_RANK),
            const(A_HEADS * A_KV_RANK, A_HEADS * A_QK_DIM), const(IDX_HEADS * IDX_DIM, A_Q_RANK),
            const(1, A_KV_RANK), const(1, IDX_DIM), const(1, IDX_DIM),
        ],
        out_specs=[
            pl.BlockSpec((A_HEADS, A_KV_RANK, tm), lambda i: (0, 0, i)),
            pl.BlockSpec((IDX_HEADS, IDX_DIM, tm), lambda i: (0, 0, i)),
            pl.BlockSpec((tm, A_KV_RANK), lambda i: (i, 0)),
            pl.BlockSpec((KVT_ROWS, tm), lambda i: (0, i)),
            pl.BlockSpec((tm, IDX_DIM), lambda i: (i, 0)),
            pl.BlockSpec((tm, MISC_W), lambda i: (i, 0)),
            pl.BlockSpec((MISC_W, tm), lambda i: (0, i)),
            pl.BlockSpec((tm, DN_QKV_W), lambda i: (i, 0)),
            pl.BlockSpec((tm, DN_VAL_W), lambda i: (i, 0)),
        ],
        out_shape=[
            jax.ShapeDtypeStruct((A_HEADS, A_KV_RANK, T), BF16),
            jax.ShapeDtypeStruct((IDX_HEADS, IDX_DIM, T), BF16),
            jax.ShapeDtypeStruct((T, A_KV_RANK), BF16),
            jax.ShapeDtypeStruct((KVT_ROWS, T), BF16),
            jax.ShapeDtypeStruct((T, IDX_DIM), BF16),
            jax.ShapeDtypeStruct((T, MISC_W), F32),
            jax.ShapeDtypeStruct((MISC_W, T), F32),
            jax.ShapeDtypeStruct((T, DN_QKV_W), F32),
            jax.ShapeDtypeStruct((T, DN_VAL_W), F32),
        ],
        compiler_params=pltpu.CompilerParams(dimension_semantics=("parallel",), vmem_limit_bytes=VMEM_LIMIT),
        name="in_proj",
    )(x2, wc, qg, wuq, wuk, wqi, kvg, lng, lnb)


def _dsa_kernel(qidx_ref, qlat_ref, wt_ref, kidx_ref, ckv_ref, ckvt_ref, wuvt_ref, g_ref, out_ref,
                keys_ref, hi_ref, lo_ref, acc_ref, stage_ref, bias_ref, *, qb, kb, rb, slab, topk, idx_bits):
    H = A_HEADS
    j = pl.program_id(1)
    q0 = j * qb
    nkeys = (j + 1) * qb
    nrblk = (nkeys + rb - 1) // rb
    ups = rb // kb
    nchunk = nrblk * ups
    w = wt_ref[...] * (IDX_HEADS ** -0.5 * IDX_DIM ** -0.5)
    row_minus_lane = (lax.broadcasted_iota(I32, (slab, qb), 0) - lax.broadcasted_iota(I32, (slab, qb), 1))

    def skewed(nparts, consume, produce, carry, on_tail=None):
        def chunk(c, k, carry):
            if ups > 1:
                for i in range(nparts):
                    carry = produce(c + 1, (k + 1) % ups, i, consume(c, k, i, carry))
                return carry
            for i in range(nparts):
                carry = consume(c, k, i, carry)
            for i in range(nparts):
                carry = produce(c + 1, k, i, carry)
            return carry

        def step(i, carry):
            for k in range(ups):
                carry = chunk(i * ups + k, k, carry)
            return carry

        for i in range(nparts):
            carry = produce(0, 0, i, carry)
        carry = lax.fori_loop(0, nrblk - 1, step, carry)
        for k in range(ups - 1):
            carry = chunk((nrblk - 1) * ups + k, k, carry)
        last = nchunk - 1

        @pl.when(last * kb < nkeys)
        def _():
            for i in range(nparts):
                consume(last, ups - 1, i, carry)

        if on_tail is not None:
            @pl.when(last * kb >= nkeys)
            def _():
                on_tail(last)
        return carry

    nslab = kb // slab
    hps = H // nslab

    def score_matmuls(c, slot, i, carry):
        kc = kidx_ref[pl.ds(pl.multiple_of(c * kb, kb), kb), :]
        for h in range(i * hps, (i + 1) * hps):
            stage_ref[slot, h] = _dot(kc, qidx_ref[h])
        return carry

    def score_keys(c, slot, i, carry):
        r0 = pl.multiple_of(c * kb, kb) + i * slab
        s = jnp.zeros((slab, qb), F32)
        for h in range(H):
            s = s + jnp.maximum(stage_ref[slot, h, i * slab:(i + 1) * slab, :], 0.0) * w[h:h + 1, :]
        bits = lax.bitcast_convert_type(s, I32)
        key = jnp.where(bits < 0, INT_MIN - bits, bits)
        key = jnp.where(row_minus_lane <= q0 - r0, key, NEG_KEY)
        keys_ref[pl.ds(r0, slab), :] = key
        hi_ref[pl.ds(r0, slab), :] = jnp.right_shift(key, 16).astype(I16)
        lo_ref[pl.ds(r0, slab), :] = ((key & 0xFFFF) - HALF16).astype(I16)
        return carry

    def score_tail(c):
        r0 = pl.multiple_of(c * kb, kb)
        keys_ref[pl.ds(r0, kb), :] = jnp.full((kb, qb), NEG_KEY, I32)
        hi_ref[pl.ds(r0, kb), :] = jnp.full((kb, qb), NEG_KEY >> 16, I16)
        lo_ref[pl.ds(r0, kb), :] = jnp.full((kb, qb), (NEG_KEY & 0xFFFF) - HALF16, I16)

    skewed(nslab, score_keys, score_matmuls, 0, score_tail)

    nacc = 4

    def count16(ref, cand, below=False):
        def body(r, accs):
            r0 = pl.multiple_of(r * rb, rb)
            k = ref[pl.ds(r0, rb), :]
            m = jnp.where(k < cand if below else k >= cand, jnp.int16(1), jnp.int16(0))
            accs = list(accs)
            for i in range(rb // PACK16):
                accs[i % nacc] = accs[i % nacc] + m[i * PACK16:(i + 1) * PACK16, :]
            return tuple(accs)
        accs = lax.fori_loop(0, nrblk, body, tuple(jnp.zeros((PACK16, qb), I16) for _ in range(nacc)))
        tot = accs[0].astype(I32)
        for a in accs[1:]:
            tot = tot + a.astype(I32)
        return tot.sum(axis=0, keepdims=True)

    def max16_below(ref, bound):
        def body(r, accs):
            r0 = pl.multiple_of(r * rb, rb)
            k = ref[pl.ds(r0, rb), :]
            m = jnp.where(k < bound, k, jnp.int16(-1))
            accs = list(accs)
            for i in range(rb // PACK16):
                tile = m[i * PACK16:(i + 1) * PACK16, :]
                accs[i % nacc] = jnp.where(tile > accs[i % nacc], tile, accs[i % nacc])
            return tuple(accs)
        accs = lax.fori_loop(0, nrblk, body, tuple(jnp.full((PACK16, qb), -1, I16) for _ in range(nacc)))
        top = accs[0].astype(I32)
        for a in accs[1:]:
            top = jnp.maximum(top, a.astype(I32))
        return top.max(axis=0, keepdims=True)

    def bisect16(ref, target, cnt_all):
        def body(p, carry):
            t_u, cnt_ok, cnt_rej = carry
            cand_u = t_u | jnp.left_shift(jnp.int32(1), 15 - p)
            cnt = count16(ref, (cand_u - HALF16).astype(I16))
            ok = cnt >= target
            return jnp.where(ok, cand_u, t_u), jnp.where(ok, cnt, cnt_ok), jnp.where(ok, cnt_rej, cnt)
        return lax.fori_loop(0, 16, body, (jnp.zeros((1, qb), I32), cnt_all, jnp.zeros((1, qb), I32)))

    hi_u, cnt_ge_hi, cnt_gt_hi = bisect16(hi_ref, topk, jnp.full((1, qb), nrblk * rb, I32))
    hi_t = (hi_u - HALF16).astype(I16)

    def mask_lo(r, carry):
        r0 = pl.multiple_of(r * rb, rb)
        lo_ref[pl.ds(r0, rb), :] = jnp.where(hi_ref[pl.ds(r0, rb), :] == hi_t, lo_ref[pl.ds(r0, rb), :],
                                             jnp.int16(-HALF16))
        return carry

    lax.fori_loop(0, nrblk, mask_lo, 0)
    lo_u, cnt_ge_lo, cnt_gt_lo = bisect16(lo_ref, topk - cnt_gt_hi, cnt_ge_hi - cnt_gt_hi)
    t = jnp.left_shift(hi_u - HALF16, 16) + lo_u
    cnt_t = cnt_gt_hi + cnt_ge_lo

    excess = jnp.where(t > NEG_KEY, cnt_t - topk, 0)

    worst = jnp.max(excess)

    @pl.when(worst > 0)
    def _():
        keep = topk - (cnt_gt_hi + cnt_gt_lo)
        lo_t = (lo_u - HALF16).astype(I16)

        def tie_rows(r, carry):
            r0 = pl.multiple_of(r * rb, rb)
            tie = (hi_ref[pl.ds(r0, rb), :] == hi_t) & (lo_ref[pl.ds(r0, rb), :] == lo_t)
            row = (r0 + lax.broadcasted_iota(I32, (rb, qb), 0)).astype(I16)
            lo_ref[pl.ds(r0, rb), :] = jnp.where(tie, row, jnp.int16(HALF16 - 1))
            return carry

        lax.fori_loop(0, nrblk, tie_rows, 0)

        def demote_from(first):
            def demote(r, carry):
                r0 = pl.multiple_of(r * rb, rb)
                k = keys_ref[pl.ds(r0, rb), :]
                row = r0 + lax.broadcasted_iota(I32, (rb, qb), 0)
                keys_ref[pl.ds(r0, rb), :] = jnp.where((k == t) & (row >= first), NEG_KEY, k)
                return carry
            lax.fori_loop(0, nrblk, demote, 0)

        @pl.when(worst <= TIE_SCAN_MAX)
        def _():
            def peel(c):
                first, left = c
                top = max16_below(lo_ref, first.astype(I16))
                return jnp.where(left > 0, top, first), jnp.maximum(left - 1, 0)
            first, _ = lax.while_loop(lambda c: jnp.max(c[1]) > 0, peel,
                                      (jnp.full((1, qb), HALF16 - 1, I32), jnp.maximum(excess, 0)))
            demote_from(first)

        @pl.when(worst > TIE_SCAN_MAX)
        def _():
            def idx_body(p, pos):
                cand = pos + jnp.left_shift(jnp.int32(1), idx_bits - 1 - p)
                return jnp.where(count16(lo_ref, cand.astype(I16), below=True) < keep, cand, pos)
            demote_from(lax.fori_loop(0, idx_bits, idx_body, jnp.zeros((1, qb), I32)) + 1)

    thr = jnp.maximum(t, NEG_KEY + 1)

    acc_ref[...] = jnp.zeros(acc_ref.shape, F32)

    def put(carry, h, item):
        return carry[:h] + (item,) + carry[h + 1:]

    def attn_logits(c, slot, h, carry):
        r0 = pl.multiple_of(c * kb, kb)
        if h == 0:
            bias_ref[slot] = jnp.where(keys_ref[pl.ds(r0, kb), :] >= thr, 0.0, -jnp.inf)
        m_prev = carry[h][1]
        lg = _dot(ckv_ref[pl.ds(r0, kb), :], qlat_ref[h]) + bias_ref[slot]
        stage_ref[slot, h] = lg
        return put(carry, h, (m_prev, jnp.maximum(m_prev, jnp.max(lg, axis=0, keepdims=True))))

    def attn_values(c, slot, h, carry):
        m_old, m_new = carry[h]
        kvt = ckvt_ref[:, pl.ds(pl.multiple_of(c * kb, kb), kb)]
        p = jnp.exp2(stage_ref[slot, h] - m_new).astype(BF16)
        acc_ref[h] = acc_ref[h] * jnp.exp2(m_old - m_new) + _dot(kvt, p)
        return carry

    m_init = jnp.full((1, qb), -1e30, F32)
    skewed(H, attn_values, attn_logits, ((m_init, m_init),) * H)

    out_t = jnp.concatenate(
        [_dot(wuvt_ref[h], (acc_ref[h, 0:A_KV_RANK, :] / acc_ref[h, A_KV_RANK:A_KV_RANK + 1, :]).astype(BF16))
         for h in range(H)], axis=0)
    out_ref[...] = _rms_norm_rows(out_t.T, g_ref[...])


def _dsa(qidx, qlat, misct, kidx, ckv, ckvt, wuvt, g, *, B, S):
    T = B * S
    qb = min(256, S)
    nq = S // qb
    kb = qb
    rb = 2 * qb if nq % 2 == 0 else qb
    topk = min(TOPK_MAX, S // 4)
    idx_bits = max(1, (S - 1).bit_length())
    assert S < HALF16, "tie-breaking keeps key positions in int16"
    slab = min(64, kb)
    kern = functools.partial(_dsa_kernel, qb=qb, kb=kb, rb=rb, slab=slab, topk=topk, idx_bits=idx_bits)
    return pl.pallas_call(
        kern,
        grid=(B, nq),
        in_specs=[
            pl.BlockSpec((IDX_HEADS, IDX_DIM, qb), lambda b, j: (0, 0, b * nq + j)),
            pl.BlockSpec((A_HEADS, A_KV_RANK, qb), lambda b, j: (0, 0, b * nq + j)),
            pl.BlockSpec((IDX_HEADS, qb), lambda b, j: (MISC_WIDX // IDX_HEADS, b * nq + j)),
            pl.BlockSpec((S, IDX_DIM), lambda b, j: (b, 0)),
            pl.BlockSpec((S, A_KV_RANK), lambda b, j: (b, 0)),
            pl.BlockSpec((KVT_ROWS, S), lambda b, j: (0, b)),
            pl.BlockSpec((A_HEADS, A_V_DIM, A_KV_RANK), lambda b, j: (0, 0, 0)),
            pl.BlockSpec((1, A_OUT_W), lambda b, j: (0, 0)),
        ],
        out_specs=pl.BlockSpec((qb, A_OUT_W), lambda b, j: (b * nq + j, 0)),
        out_shape=jax.ShapeDtypeStruct((T, A_OUT_W), F32),
        scratch_shapes=[
            pltpu.VMEM((S, qb), I32),
            pltpu.VMEM((S, qb), I16),
            pltpu.VMEM((S, qb), I16),
            pltpu.VMEM((A_HEADS, KVT_ROWS, qb), F32),
            pltpu.VMEM((rb // kb, A_HEADS, kb, qb), F32),
            pltpu.VMEM((rb // kb, kb, qb), F32),
        ],
        compiler_params=pltpu.CompilerParams(
            dimension_semantics=("parallel", "arbitrary"), vmem_limit_bytes=VMEM_LIMIT),
        name="dsa",
    )(qidx, qlat, misct, kidx, ckv, ckvt, wuvt, g)


def _gdn_constants(sb):
    r, c = np.arange(sb)[:, None], np.arange(sb)[None, :]
    same = (r // GDN_CHUNK) == (c // GDN_CHUNK)
    tri = same & (c <= r)
    diag = [((r // b) == (c // b)).astype(np.float32) for b in (8, 16, 32)] + [same.astype(np.float32)]
    eye = (r == c).astype(np.float32)
    mats = [diag[0]] + [diag[i] - diag[i - 1] for i in (1, 2, 3)] + [eye, 1.0 - eye, np.where(tri, 0.0, -np.inf)]
    mask_c = np.concatenate([tri, same], axis=0).astype(np.float32)
    mask_r = np.concatenate([same & (r <= c), same], axis=1).astype(np.float32)
    spread = (np.arange(DN_KEY_W)[None, :] // DN_K_DIM == np.arange(DN_HEADS)[:, None]).astype(np.float32)
    spread3 = np.zeros((3, 3, 3, DN_HEADS, DN_KEY_W), np.float32)
    for j in range(3):
        spread3[j, :, j] = spread
    return (jnp.asarray(np.stack(mats), F32), jnp.asarray(mask_c, BF16), jnp.asarray(mask_r, BF16),
            jnp.asarray(spread3.reshape(3, 9 * DN_HEADS, DN_KEY_W), BF16), jnp.asarray(np.tile(spread, (1, 2)), BF16))


def _gdn_kernel(qkv_ref, z_ref, misc_ref, at_ref, cw_ref, alog_c_ref, dtb_c_ref, alog_r_ref, dtb_r_ref,
                ng_ref, fm_ref, mc_ref, mr_ref, sr_ref, sc_ref, out_ref, xbuf_ref, state_ref, *, tb, sb):
    C = GDN_CHUNK
    nck = tb // C
    H = DN_HEADS
    step = pl.program_id(1)

    @pl.when(step == 0)
    def _():
        xbuf_ref[0:SUBLANES, :] = jnp.zeros((SUBLANES, DN_QKV_W), F32)
        state_ref[...] = jnp.zeros(state_ref.shape, F32)

    xbuf_ref[SUBLANES:SUBLANES + tb, :] = qkv_ref[...]
    xe = xbuf_ref[...]
    y = xe[SUBLANES:] * cw_ref[DN_CONV - 1:DN_CONV, :]
    for d in range(1, DN_CONV):
        y = y + pltpu.roll(xe, d, 0)[SUBLANES:] * cw_ref[DN_CONV - 1 - d:DN_CONV - d, :]
    xbuf_ref[0:SUBLANES, :] = xe[tb:tb + SUBLANES]
    y = _silu(y)

    def softplus(v):
        return jnp.maximum(v, 0.0) + jnp.log(1.0 + jnp.exp(-jnp.abs(v)))

    misc = misc_ref[...]
    beta_c = _sigmoid(misc[:, MISC_B:MISC_B + H])
    g_c = -jnp.exp(alog_r_ref[...]) * softplus(misc[:, MISC_A:MISC_A + H] + dtb_r_ref[...])
    g_r = -jnp.exp(alog_c_ref[...]) * softplus(at_ref[...] + dtb_c_ref[...])

    rows = [slice(b * sb, (b + 1) * sb) for b in range(tb // sb)]
    mask_c = mc_ref[...]
    mask_r = mr_ref[...]

    def pieces(v, axis, n=3):
        out, rest = [], v
        for _ in range(n - 1):
            out.append(rest.astype(BF16))
            rest = rest - out[-1].astype(F32)
        return jnp.concatenate(out + [rest.astype(BF16)], axis=axis)

    sums_c = [_dot(mask_c, pieces(g_c[r], 1)) for r in rows]
    sums_c = [s[:, 0:H] + s[:, H:2 * H] + s[:, 2 * H:3 * H] for s in sums_c]
    sums_r = [_dot(pieces(g_r[:, r], 0), mask_r) for r in rows]
    sums_r = [s[0:H] + s[H:2 * H] + s[2 * H:3 * H] for s in sums_r]
    gcum_c = [s[0:sb] for s in sums_c]
    glast_c = [s[sb:2 * sb] for s in sums_c]
    gcum_r = [s[:, 0:sb] for s in sums_r]
    glast_r = [s[:, sb:2 * sb] for s in sums_r]
    blk8, ring_f, eye, off_diag, tri_bias = fm_ref[0], [fm_ref[1], fm_ref[2], fm_ref[3]], fm_ref[4], fm_ref[5], fm_ref[6]

    hs = range(H)
    bh = [(b, h) for b in range(len(rows)) for h in hs]
    ch = range(len(bh))
    def expand(*vs):
        p = pieces(jnp.concatenate(list(vs) + [vs[-1]] * (3 - len(vs)), axis=1), 1)
        return [_dot(p, sr_ref[j]) for j in range(len(vs))]

    head_sums = lambda v: _dot_nt(pieces(v, 1, 2), sc_ref[...])
    yq, yk, yv = y[:, :DN_KEY_W], y[:, DN_KEY_W:2 * DN_KEY_W], y[:, 2 * DN_KEY_W:]
    gc_all = jnp.concatenate(gcum_c, axis=0)
    eg_c = jnp.exp(gc_all)
    rq_x, rk_x, beta_x = expand(lax.rsqrt(head_sums(yq * yq) + RMS_EPS) * (DN_K_DIM ** -0.5),
                                lax.rsqrt(head_sums(yk * yk) + RMS_EPS), beta_c)
    eg_x, egl_x = expand(eg_c, jnp.exp(jnp.concatenate(glast_c, axis=0) - gc_all))
    q_n, k_n = yq * rq_x, yk * rk_x
    kb_all = k_n * beta_x
    k_bf, kb_bf, q_bf = k_n.astype(BF16), kb_all.astype(BF16), q_n.astype(BF16)
    rhs_v, rhs_k = (yv * beta_x).astype(BF16), (kb_all * eg_x).astype(BF16)
    qd_all = (q_n * eg_x).astype(BF16)
    ke_all = (k_n * egl_x).astype(BF16)
    head = lambda a, i: a[rows[bh[i][0]], bh[i][1] * DN_K_DIM:(bh[i][1] + 1) * DN_K_DIM]
    gcs = [gcum_c[b][:, h:h + 1] for b, h in bh]
    kk = [_dot_nt(head(kb_bf, i), head(k_bf, i)) for i in ch]
    qk = [_dot_nt(head(q_bf, i), head(k_bf, i)) for i in ch]
    decay = [jnp.exp((gcs[i] - gcum_r[b][h:h + 1, :]) + tri_bias) for i, (b, h) in enumerate(bh)]
    lower = [kk[i] * (decay[i] * off_diag) for i in ch]
    attn = [(qk[i] * decay[i]).astype(BF16) for i in ch]
    pw = [lower[i] * blk8 for i in ch]
    x_inv = [eye - pw[i] for i in ch]
    for _ in range(2):
        pwb = [p.astype(BF16) for p in pw]
        pw = [_dot(p, p) for p in pwb]
        x_inv = [x_inv[i] + _dot(x_inv[i].astype(BF16), pw[i].astype(BF16)) for i in ch]
    for ring in ring_f:
        xb = [x.astype(BF16) for x in x_inv]
        t1 = [_dot(xb[i], (lower[i] * ring).astype(BF16)).astype(BF16) for i in ch]
        x_inv = [x_inv[i] - _dot(t1[i], xb[i]) for i in ch]
    uw = [_dot(x_inv[i].astype(BF16), jnp.concatenate([head(rhs_v, i), head(rhs_k, i)], axis=1)) for i in ch]
    us = [m[:, :DN_V_DIM] for m in uw]
    wkb = [m[:, DN_V_DIM:].astype(BF16) for m in uw]
    qd = [head(qd_all, i) for i in ch]
    ke = [head(ke_all, i) for i in ch]

    st = [state_ref[h] for h in hs]
    outs = [[] for _ in hs]
    for n in range(nck):
        b, r0 = divmod(n * C, sb)
        r = slice(r0, r0 + C)
        ws = [_dot(jnp.concatenate([wkb[b * H + h][r], qd[b * H + h][r]], axis=0), st[h].astype(BF16)) for h in hs]
        v_nb = [(us[b * H + h][r] - ws[h][0:C]).astype(BF16) for h in hs]
        for h in hs:
            outs[h].append(ws[h][C:2 * C] + _dot(attn[b * H + h][r, r], v_nb[h]))
        cd = [jnp.exp(glast_r[b][h:h + 1, r0:r0 + 1]) for h in hs]
        st = [st[h] * cd[h] + _dot_tn(ke[b * H + h][r], v_nb[h]) for h in hs]
    for h in hs:
        state_ref[h] = st[h]
    o = jnp.concatenate([jnp.concatenate(outs[h], axis=0) for h in hs], axis=1)
    r_o = lax.rsqrt(head_sums(o * o) * (1.0 / DN_V_DIM) + RMS_EPS)
    o = o * expand(r_o)[0] * ng_ref[...]
    out_ref[...] = (o * _silu(z_ref[...])).astype(out_ref.dtype)


def _gdn(qkv, z, misc, misct, conv_w, a_log, dt_bias, norm_g, *, B, S):
    T = B * S
    sb = min(GDN_SOLVE, S)
    tb = min(GDN_SOLVES_PER_STEP * sb, S)
    ns = S // tb
    H = DN_HEADS
    kern = functools.partial(_gdn_kernel, tb=tb, sb=sb)
    consts = _gdn_constants(sb)
    const = lambda *s: pl.BlockSpec(s, lambda b, i: (0,) * len(s))
    return pl.pallas_call(
        kern,
        grid=(B, ns),
        in_specs=[
            pl.BlockSpec((tb, DN_QKV_W), lambda b, i: (b * ns + i, 0)),
            pl.BlockSpec((tb, DN_VAL_W), lambda b, i: (b * ns + i, 0)),
            pl.BlockSpec((tb, MISC_W), lambda b, i: (b * ns + i, 0)),
            pl.BlockSpec((H, tb), lambda b, i: (MISC_A // H, b * ns + i)),
            const(DN_CONV, DN_QKV_W), const(H, 1), const(H, 1), const(1, H), const(1, H), const(1, DN_VAL_W),
        ] + [const(*c.shape) for c in consts],
        out_specs=pl.BlockSpec((tb, DN_VAL_W), lambda b, i: (b * ns + i, 0)),
        out_shape=jax.ShapeDtypeStruct((T, DN_VAL_W), BF16),
        scratch_shapes=[
            pltpu.VMEM((tb + SUBLANES, DN_QKV_W), F32),
            pltpu.VMEM((H, DN_K_DIM, DN_V_DIM), F32),
        ],
        compiler_params=pltpu.CompilerParams(
            dimension_semantics=("parallel", "arbitrary"), vmem_limit_bytes=VMEM_LIMIT),
        name="gdn",
    )(qkv, z, misc, misct, conv_w, a_log.reshape(H, 1), dt_bias.reshape(H, 1),
      a_log.reshape(1, H), dt_bias.reshape(1, H), jnp.tile(norm_g.reshape(1, DN_V_DIM), (1, H)), *consts)


def _out_proj_kernel(a_ref, d_ref, x_ref, wa_ref, wd_ref, g_ref, b_ref, h_ref, *, alpha):
    for r in range(0, a_ref.shape[0], ROW_GROUP):
        rows = slice(r, r + ROW_GROUP)
        mix = _dot(a_ref[rows, :].astype(BF16), wa_ref[...]) + _dot(d_ref[rows, :], wd_ref[...])
        h_ref[rows, :] = _layer_norm_rows(alpha * x_ref[rows, :] + mix, g_ref[...], b_ref[...])


def _out_proj(a, d, x2, wa, wd, g, b, *, alpha, tm):
    T = x2.shape[0]
    const = lambda *s: pl.BlockSpec(s, lambda i: (0,) * len(s))
    return pl.pallas_call(
        functools.partial(_out_proj_kernel, alpha=alpha),
        grid=(T // tm,),
        in_specs=[
            pl.BlockSpec((tm, A_OUT_W), lambda i: (i, 0)),
            pl.BlockSpec((tm, DN_VAL_W), lambda i: (i, 0)),
            pl.BlockSpec((tm, D_MODEL), lambda i: (i, 0)),
            const(A_OUT_W, D_MODEL), const(DN_VAL_W, D_MODEL), const(1, D_MODEL), const(1, D_MODEL),
        ],
        out_specs=pl.BlockSpec((tm, D_MODEL), lambda i: (i, 0)),
        out_shape=jax.ShapeDtypeStruct((T, D_MODEL), F32),
        compiler_params=pltpu.CompilerParams(dimension_semantics=("parallel",), vmem_limit_bytes=VMEM_LIMIT),
        name="out_proj",
    )(a, d, x2, wa, wd, g, b)


def _ffn_kernel(h_ref, halo_ref, wg_ref, wu_ref, cw_ref, cb_ref, wd_ref, g_ref, b_ref, out_ref,
                acc_ref, *, alpha, tiles_per_seq):
    i = pl.program_id(0)
    f = pl.program_id(1)
    nf = pl.num_programs(1)
    hb = h_ref[...].astype(BF16)
    wg = wg_ref[...]
    gate = _dot(hb, wg)
    up = _dot(hb, wu_ref[...])
    halo = _dot(halo_ref[...].astype(BF16), wg)
    halo = jnp.where(i % tiles_per_seq == 0, 0.0, halo)
    ge = jnp.concatenate([halo, gate], axis=0)
    conv = gate * cw_ref[FFN_CONV - 1:FFN_CONV, :]
    for d in range(1, FFN_CONV):
        conv = conv + pltpu.roll(ge, d, 0)[SUBLANES:] * cw_ref[FFN_CONV - 1 - d:FFN_CONV - d, :]
    act = (_silu(conv + cb_ref[...]) * up).astype(BF16)
    part = _dot(act, wd_ref[...])

    @pl.when(f == 0)
    def _():
        acc_ref[...] = part

    @pl.when(f > 0)
    def _():
        acc_ref[...] += part

    @pl.when(f == nf - 1)
    def _():
        out_ref[...] = _layer_norm_rows(alpha * h_ref[...] + acc_ref[...], g_ref[...], b_ref[...])


def _ffn(h, wg, wu, cw, cb, wd, g, b, *, alpha, S, tm, tf):
    T = h.shape[0]
    nf = D_FF // tf
    hs = tm // SUBLANES
    kern = functools.partial(_ffn_kernel, alpha=alpha, tiles_per_seq=S // tm)
    return pl.pallas_call(
        kern,
        grid=(T // tm, nf),
        in_specs=[
            pl.BlockSpec((tm, D_MODEL), lambda i, f: (i, 0)),
            pl.BlockSpec((SUBLANES, D_MODEL), lambda i, f: (jnp.maximum(i * hs - 1, 0), 0)),
            pl.BlockSpec((D_MODEL, tf), lambda i, f: (0, f)),
            pl.BlockSpec((D_MODEL, tf), lambda i, f: (0, f)),
            pl.BlockSpec((FFN_CONV, tf), lambda i, f: (0, f)),
            pl.BlockSpec((1, tf), lambda i, f: (0, f)),
            pl.BlockSpec((tf, D_MODEL), lambda i, f: (f, 0)),
            pl.BlockSpec((1, D_MODEL), lambda i, f: (0, 0)),
            pl.BlockSpec((1, D_MODEL), lambda i, f: (0, 0)),
        ],
        out_specs=pl.BlockSpec((tm, D_MODEL), lambda i, f: (i, 0)),
        out_shape=jax.ShapeDtypeStruct((T, D_MODEL), F32),
        scratch_shapes=[pltpu.VMEM((tm, D_MODEL), F32)],
        compiler_params=pltpu.CompilerParams(
            dimension_semantics=("parallel", "arbitrary"), vmem_limit_bytes=VMEM_LIMIT),
        name="ffn",
    )(h, h, wg, wu, cw, cb, wd, g, b)


def _regroup_w_in(w):
    offs = [0]
    for s in IN_SIZES:
        offs.append(offs[-1] + s)
    cq, ckv, kidx, widx, qkv, z, b, a = (w[:, offs[i]:offs[i + 1]] for i in range(8))
    pad = jnp.zeros((w.shape[0], MISC_W - (IDX_DIM + IDX_HEADS + 2 * DN_HEADS)), w.dtype)
    return jnp.concatenate([cq, ckv, kidx, widx, b, a, pad, qkv, z], axis=1).astype(BF16)


def _layer(x2, p, *, B, S, alpha):
    row = lambda v: v.reshape(1, -1)
    wc = _regroup_w_in(p["w_in"])
    wuk_bd_t = jnp.einsum("hdc,hg->hcgd", p["w_uk"], jnp.eye(A_HEADS, dtype=F32)).reshape(
        A_HEADS * A_KV_RANK, A_HEADS * A_QK_DIM)
    qlat, qidx, ckv, ckvt, kidx, misc, misct, qkv, z = _in_proj(
        x2, wc, row(p["q_norm_g"]), p["w_uq"].T.astype(BF16), wuk_bd_t.astype(BF16), p["w_qidx"].T.astype(BF16),
        row(p["kv_norm_g"]),
        row(p["kidx_ln_g"]), row(p["kidx_ln_b"]), tm=min(512, S))
    a_out = _dsa(qidx, qlat, misct, kidx, ckv, ckvt, p["w_uv"].transpose(0, 2, 1).astype(BF16),
                 row(p["attn_out_g"]), B=B, S=S)
    d_out = _gdn(qkv, z, misc, misct, p["dn_conv_w"], p["dn_a_log"], p["dn_dt_bias"], p["dn_norm_g"],
                 B=B, S=S)
    w_out = p["w_out"].astype(BF16)
    h = _out_proj(a_out, d_out, x2, w_out[:A_OUT_W], w_out[A_OUT_W:], row(p["ln1_g"]), row(p["ln1_b"]),
                  alpha=alpha, tm=min(512, S))
    w_ffn = p["ffn_w_in"].astype(BF16)
    return _ffn(h, w_ffn[:, :D_FF], w_ffn[:, D_FF:], p["ffn_conv_w"], row(p["ffn_conv_b"]),
                p["ffn_w_down"].astype(BF16), row(p["ln2_g"]), row(p["ln2_b"]),
                alpha=alpha, S=S, tm=min(512, S), tf=D_FF // 2)


_PARAM_NAMES = ("w_in", "q_norm_g", "w_uq", "w_qidx", "kv_norm_g", "w_uk", "w_uv", "kidx_ln_g", "kidx_ln_b",
                "attn_out_g", "dn_conv_w", "dn_a_log", "dn_dt_bias", "dn_norm_g", "w_out", "ln1_g", "ln1_b",
                "ffn_w_in", "ffn_conv_w", "ffn_conv_b", "ffn_w_down", "ln2_g", "ln2_b")


def kernel(x, w_in, q_norm_g, w_uq, w_qidx, kv_norm_g, w_uk, w_uv, kidx_ln_g, kidx_ln_b, attn_out_g, dn_conv_w, dn_a_log, dn_dt_bias, dn_norm_g, w_out, ln1_g, ln1_b, ffn_w_in, ffn_conv_w, ffn_conv_b, ffn_w_down, ln2_g, ln2_b):
    params = (w_in, q_norm_g, w_uq, w_qidx, kv_norm_g, w_uk, w_uv, kidx_ln_g, kidx_ln_b, attn_out_g, dn_conv_w,
              dn_a_log, dn_dt_bias, dn_norm_g, w_out, ln1_g, ln1_b, ffn_w_in, ffn_conv_w, ffn_conv_b, ffn_w_down,
              ln2_g, ln2_b)
    B, S, D = x.shape
    depth = w_in.shape[0]
    alpha = (2 * depth) ** 0.25
    x2 = x.reshape(B * S, D)
    for l in range(depth):
        x2 = _layer(x2, {n: v[l] for n, v in zip(_PARAM_NAMES, params)}, B=B, S=S, alpha=alpha)
    return x2.reshape(B, S, D)
```

```python
import functools

import jax
import jax.numpy as jnp
import numpy as np
from jax import lax
from jax.experimental import pallas as pl
from jax.experimental.pallas import tpu as pltpu

F32 = jnp.float32
BF16 = jnp.bfloat16
I32 = jnp.int32
I16 = jnp.int16

D_MODEL = 1024
A_HEADS = 8
A_QK_DIM = 64
A_V_DIM = 64
A_Q_RANK = 256
A_KV_RANK = 128
IDX_HEADS = 8
IDX_DIM = 64
TOPK_MAX = 256
DN_HEADS = 8
DN_K_DIM = 64
DN_V_DIM = 64
DN_CONV = 4
D_FF = 2816
FFN_CONV = 3
RMS_EPS = 1e-6
LN_EPS = 1e-5

A_OUT_W = A_HEADS * A_V_DIM
DN_KEY_W = DN_HEADS * DN_K_DIM
DN_VAL_W = DN_HEADS * DN_V_DIM
DN_QKV_W = 2 * DN_KEY_W + DN_VAL_W
IN_SIZES = (A_Q_RANK, A_KV_RANK, IDX_DIM, IDX_HEADS, DN_QKV_W, DN_VAL_W, DN_HEADS, DN_HEADS)

MISC_W = 128
MISC_WIDX = IDX_DIM
MISC_B = IDX_DIM + IDX_HEADS
MISC_A = MISC_B + DN_HEADS
PROJ_W = A_Q_RANK + A_KV_RANK + MISC_W + DN_QKV_W + DN_VAL_W

LOG2E = 1.4426950408889634
SUBLANES = 8
PACK16 = 16
KVT_ROWS = A_KV_RANK + PACK16
HALF16 = 2 ** 15
INT_MIN = -(2 ** 31)
NEG_KEY = INT_MIN + 0x00800000
VMEM_LIMIT = 56 * 1024 * 1024

TIE_SCAN_MAX = 8
ROW_GROUP = 256
GDN_CHUNK = 64
GDN_SOLVE = 256
GDN_SOLVES_PER_STEP = 1


def _dot(a, b, precision=None):
    return jnp.dot(a, b, preferred_element_type=F32, precision=precision)


def _dot_nt(a, b, precision=None):
    return lax.dot_general(a, b, (((1,), (1,)), ((), ())), preferred_element_type=F32, precision=precision)


def _dot_tn(a, b, precision=None):
    return lax.dot_general(a, b, (((0,), (0,)), ((), ())), preferred_element_type=F32, precision=precision)


def _sigmoid(x):
    return 1.0 / (1.0 + jnp.exp(-x))


def _silu(x):
    return x * _sigmoid(x)


def _layer_norm_rows(v, g, b):
    mu = jnp.mean(v, axis=-1, keepdims=True)
    d = v - mu
    var = jnp.mean(d * d, axis=-1, keepdims=True)
    return d * lax.rsqrt(var + LN_EPS) * g + b


def _rms_norm_rows(v, g):
    return v * lax.rsqrt(jnp.mean(v * v, axis=-1, keepdims=True) + RMS_EPS) * g


def _in_proj_kernel(x_ref, w_ref, qg_ref, wuq_ref, wuk_ref, wqi_ref, kvg_ref, lng_ref, lnb_ref,
                    qlat_ref, qidx_ref, ckv_ref, ckvt_ref, kidx_ref, misc_ref, misct_ref, qkv_ref, z_ref):
    xb = x_ref[...].astype(BF16)
    proj = _dot(xb, w_ref[...])
    o = 0
    c_q = proj[:, o:o + A_Q_RANK]; o += A_Q_RANK
    c_kv = proj[:, o:o + A_KV_RANK]; o += A_KV_RANK
    misc = proj[:, o:o + MISC_W]; o += MISC_W
    qkv_ref[...] = proj[:, o:o + DN_QKV_W]; o += DN_QKV_W
    z_ref[...] = proj[:, o:o + DN_VAL_W]

    cqn_t = _rms_norm_rows(c_q, qg_ref[...]).T.astype(BF16)
    q_t = _dot(wuq_ref[...], cqn_t).astype(BF16)
    qlat_t = _dot(wuk_ref[...], q_t) * (A_QK_DIM ** -0.5 * LOG2E)
    qidx_t = _dot(wqi_ref[...], cqn_t)
    for h in range(A_HEADS):
        qlat_ref[h] = qlat_t[h * A_KV_RANK:(h + 1) * A_KV_RANK, :].astype(BF16)
        qidx_ref[h] = qidx_t[h * IDX_DIM:(h + 1) * IDX_DIM, :].astype(BF16)
    ckv = _rms_norm_rows(c_kv, kvg_ref[...])
    ckv_ref[...] = ckv.astype(BF16)
    ckvt_ref[0:A_KV_RANK, :] = ckv.T.astype(BF16)
    ckvt_ref[A_KV_RANK:KVT_ROWS, :] = jnp.ones((KVT_ROWS - A_KV_RANK, ckv.shape[0]), BF16)
    kidx_ref[...] = _layer_norm_rows(misc[:, :IDX_DIM], lng_ref[...], lnb_ref[...]).astype(BF16)
    misc_ref[...] = misc
    misct_ref[...] = misc.T


def _in_proj(x2, wc, qg, wuq, wuk, wqi, kvg, lng, lnb, *, tm):
    T = x2.shape[0]
    const = lambda *s: pl.BlockSpec(s, lambda i: (0,) * len(s))
    return pl.pallas_call(
        _in_proj_kernel,
        grid=(T // tm,),
        in_specs=[
            pl.BlockSpec((tm, D_MODEL), lambda i: (i, 0)),
            const(D_MODEL, PROJ_W), const(1, A_Q_RANK), const(A_HEADS * A_QK_DIM, A_Q_RANK),
            const(A_HEADS * A_KV_RANK, A_HEADS * A_QK_DIM), const(IDX_HEADS * IDX_DIM, A_Q_RANK),
            const(1, A_KV_RANK), const(1, IDX_DIM), const(1, IDX_DIM),
        ],
        out_specs=[
            pl.BlockSpec((A_HEADS, A_KV_RANK, tm), lambda i: (0, 0, i)),
            pl.BlockSpec((IDX_HEADS, IDX_DIM, tm), lambda i: (0, 0, i)),
            pl.BlockSpec((tm, A_KV_RANK), lambda i: (i, 0)),
            pl.BlockSpec((KVT_ROWS, tm), lambda i: (0, i)),
            pl.BlockSpec((tm, IDX_DIM), lambda i: (i, 0)),
            pl.BlockSpec((tm, MISC_W), lambda i: (i, 0)),
            pl.BlockSpec((MISC_W, tm), lambda i: (0, i)),
            pl.BlockSpec((tm, DN_QKV_W), lambda i: (i, 0)),
            pl.BlockSpec((tm, DN_VAL_W), lambda i: (i, 0)),
        ],
        out_shape=[
            jax.ShapeDtypeStruct((A_HEADS, A_KV_RANK, T), BF16),
            jax.ShapeDtypeStruct((IDX_HEADS, IDX_DIM, T), BF16),
            jax.ShapeDtypeStruct((T, A_KV_RANK), BF16),
            jax.ShapeDtypeStruct((KVT_ROWS, T), BF16),
            jax.ShapeDtypeStruct((T, IDX_DIM), BF16),
            jax.ShapeDtypeStruct((T, MISC_W), F32),
            jax.ShapeDtypeStruct((MISC_W, T), F32),
            jax.ShapeDtypeStruct((T, DN_QKV_W), F32),
            jax.ShapeDtypeStruct((T, DN_VAL_W), F32),
        ],
        compiler_params=pltpu.CompilerParams(dimension_semantics=("parallel",), vmem_limit_bytes=VMEM_LIMIT),
        name="in_proj",
    )(x2, wc, qg, wuq, wuk, wqi, kvg, lng, lnb)


def _dsa_kernel(qidx_ref, qlat_ref, wt_ref, kidx_ref, ckv_ref, ckvt_ref, wuvt_ref, g_ref, out_ref,
                keys_ref, hi_ref, lo_ref, acc_ref, stage_ref, bias_ref, *, qb, kb, rb, slab, topk, idx_bits):
    H = A_HEADS
    j = pl.program_id(1)
    q0 = j * qb
    nkeys = (j + 1) * qb
    nrblk = (nkeys + rb - 1) // rb
    ups = rb // kb
    nchunk = nrblk * ups
    w = wt_ref[...] * (IDX_HEADS ** -0.5 * IDX_DIM ** -0.5)
    row_minus_lane = (lax.broadcasted_iota(I32, (slab, qb), 0) - lax.broadcasted_iota(I32, (slab, qb), 1))

    def skewed(nparts, consume, produce, carry, on_tail=None):
        def chunk(c, k, carry):
            if ups > 1:
                for i in range(nparts):
                    carry = produce(c + 1, (k + 1) % ups, i, consume(c, k, i, carry))
                return carry
            for i in range(nparts):
                carry = consume(c, k, i, carry)
            for i in range(nparts):
                carry = produce(c + 1, k, i, carry)
            return carry

        def step(i, carry):
            for k in range(ups):
                carry = chunk(i * ups + k, k, carry)
            return carry

        for i in range(nparts):
            carry = produce(0, 0, i, carry)
        carry = lax.fori_loop(0, nrblk - 1, step, carry)
        for k in range(ups - 1):
            carry = chunk((nrblk - 1) * ups + k, k, carry)
        last = nchunk - 1

        @pl.when(last * kb < nkeys)
        def _():
            for i in range(nparts):
                consume(last, ups - 1, i, carry)

        if on_tail is not None:
            @pl.when(last * kb >= nkeys)
            def _():
                on_tail(last)
        return carry

    nslab = kb // slab
    hps = H // nslab

    def score_matmuls(c, slot, i, carry):
        kc = kidx_ref[pl.ds(pl.multiple_of(c * kb, kb), kb), :]
        for h in range(i * hps, (i + 1) * hps):
            stage_ref[slot, h] = _dot(kc, qidx_ref[h])
        return carry

    def score_keys(c, slot, i, carry):
        r0 = pl.multiple_of(c * kb, kb) + i * slab
        s = jnp.zeros((slab, qb), F32)
        for h in range(H):
            s = s + jnp.maximum(stage_ref[slot, h, i * slab:(i + 1) * slab, :], 0.0) * w[h:h + 1, :]
        bits = lax.bitcast_convert_type(s, I32)
        key = jnp.where(bits < 0, INT_MIN - bits, bits)
        key = jnp.where(row_minus_lane <= q0 - r0, key, NEG_KEY)
        keys_ref[pl.ds(r0, slab), :] = key
        hi_ref[pl.ds(r0, slab), :] = jnp.right_shift(key, 16).astype(I16)
        lo_ref[pl.ds(r0, slab), :] = ((key & 0xFFFF) - HALF16).astype(I16)
        return carry

    def score_tail(c):
        r0 = pl.multiple_of(c * kb, kb)
        keys_ref[pl.ds(r0, kb), :] = jnp.full((kb, qb), NEG_KEY, I32)
        hi_ref[pl.ds(r0, kb), :] = jnp.full((kb, qb), NEG_KEY >> 16, I16)
        lo_ref[pl.ds(r0, kb), :] = jnp.full((kb, qb), (NEG_KEY & 0xFFFF) - HALF16, I16)

    skewed(nslab, score_keys, score_matmuls, 0, score_tail)

    nacc = 4

    def count16(ref, cand, below=False):
        def body(r, accs):
            r0 = pl.multiple_of(r * rb, rb)
            k = ref[pl.ds(r0, rb), :]
            m = jnp.where(k < cand if below else k >= cand, jnp.int16(1), jnp.int16(0))
            accs = list(accs)
            for i in range(rb // PACK16):
                accs[i % nacc] = accs[i % nacc] + m[i * PACK16:(i + 1) * PACK16, :]
            return tuple(accs)
        accs = lax.fori_loop(0, nrblk, body, tuple(jnp.zeros((PACK16, qb), I16) for _ in range(nacc)))
        tot = accs[0].astype(I32)
        for a in accs[1:]:
            tot = tot + a.astype(I32)
        return tot.sum(axis=0, keepdims=True)

    def max16_below(ref, bound):
        def body(r, accs):
            r0 = pl.multiple_of(r * rb, rb)
            k = ref[pl.ds(r0, rb), :]
            m = jnp.where(k < bound, k, jnp.int16(-1))
            accs = list(accs)
            for i in range(rb // PACK16):
                tile = m[i * PACK16:(i + 1) * PACK16, :]
                accs[i % nacc] = jnp.where(tile > accs[i % nacc], tile, accs[i % nacc])
            return tuple(accs)
        accs = lax.fori_loop(0, nrblk, body, tuple(jnp.full((PACK16, qb), -1, I16) for _ in range(nacc)))
        top = accs[0].astype(I32)
        for a in accs[1:]:
            top = jnp.maximum(top, a.astype(I32))
        return top.max(axis=0, keepdims=True)

    def bisect16(ref, target, cnt_all):
        def body(p, carry):
            t_u, cnt_ok, cnt_rej = carry
            cand_u = t_u | jnp.left_shift(jnp.int32(1), 15 - p)
            cnt = count16(ref, (cand_u - HALF16).astype(I16))
            ok = cnt >= target
            return jnp.where(ok, cand_u, t_u), jnp.where(ok, cnt, cnt_ok), jnp.where(ok, cnt_rej, cnt)
        return lax.fori_loop(0, 16, body, (jnp.zeros((1, qb), I32), cnt_all, jnp.zeros((1, qb), I32)))

    hi_u, cnt_ge_hi, cnt_gt_hi = bisect16(hi_ref, topk, jnp.full((1, qb), nrblk * rb, I32))
    hi_t = (hi_u - HALF16).astype(I16)

    def mask_lo(r, carry):
        r0 = pl.multiple_of(r * rb, rb)
        lo_ref[pl.ds(r0, rb), :] = jnp.where(hi_ref[pl.ds(r0, rb), :] == hi_t, lo_ref[pl.ds(r0, rb), :],
                                             jnp.int16(-HALF16))
        return carry

    lax.fori_loop(0, nrblk, mask_lo, 0)
    lo_u, cnt_ge_lo, cnt_gt_lo = bisect16(lo_ref, topk - cnt_gt_hi, cnt_ge_hi - cnt_gt_hi)
    t = jnp.left_shift(hi_u - HALF16, 16) + lo_u
    cnt_t = cnt_gt_hi + cnt_ge_lo

    excess = jnp.where(t > NEG_KEY, cnt_t - topk, 0)

    worst = jnp.max(excess)

    @pl.when(worst > 0)
    def _():
        keep = topk - (cnt_gt_hi + cnt_gt_lo)
        lo_t = (lo_u - HALF16).astype(I16)

        def tie_rows(r, carry):
            r0 = pl.multiple_of(r * rb, rb)
            tie = (hi_ref[pl.ds(r0, rb), :] == hi_t) & (lo_ref[pl.ds(r0, rb), :] == lo_t)
            row = (r0 + lax.broadcasted_iota(I32, (rb, qb), 0)).astype(I16)
            lo_ref[pl.ds(r0, rb), :] = jnp.where(tie, row, jnp.int16(HALF16 - 1))
            return carry

        lax.fori_loop(0, nrblk, tie_rows, 0)

        def demote_from(first):
            def demote(r, carry):
                r0 = pl.multiple_of(r * rb, rb)
                k = keys_ref[pl.ds(r0, rb), :]
                row = r0 + lax.broadcasted_iota(I32, (rb, qb), 0)
                keys_ref[pl.ds(r0, rb), :] = jnp.where((k == t) & (row >= first), NEG_KEY, k)
                return carry
            lax.fori_loop(0, nrblk, demote, 0)

        @pl.when(worst <= TIE_SCAN_MAX)
        def _():
            def peel(c):
                first, left = c
                top = max16_below(lo_ref, first.astype(I16))
                return jnp.where(left > 0, top, first), jnp.maximum(left - 1, 0)
            first, _ = lax.while_loop(lambda c: jnp.max(c[1]) > 0, peel,
                                      (jnp.full((1, qb), HALF16 - 1, I32), jnp.maximum(excess, 0)))
            demote_from(first)

        @pl.when(worst > TIE_SCAN_MAX)
        def _():
            def idx_body(p, pos):
                cand = pos + jnp.left_shift(jnp.int32(1), idx_bits - 1 - p)
                return jnp.where(count16(lo_ref, cand.astype(I16), below=True) < keep, cand, pos)
            demote_from(lax.fori_loop(0, idx_bits, idx_body, jnp.zeros((1, qb), I32)) + 1)

    thr = jnp.maximum(t, NEG_KEY + 1)

    acc_ref[...] = jnp.zeros(acc_ref.shape, F32)

    def put(carry, h, item):
        return carry[:h] + (item,) + carry[h + 1:]

    def attn_logits(c, slot, h, carry):
        r0 = pl.multiple_of(c * kb, kb)
        if h == 0:
            bias_ref[slot] = jnp.where(keys_ref[pl.ds(r0, kb), :] >= thr, 0.0, -jnp.inf)
        m_prev = carry[h][1]
        lg = _dot(ckv_ref[pl.ds(r0, kb), :], qlat_ref[h]) + bias_ref[slot]
        stage_ref[slot, h] = lg
        return put(carry, h, (m_prev, jnp.maximum(m_prev, jnp.max(lg, axis=0, keepdims=True))))

    def attn_values(c, slot, h, carry):
        m_old, m_new = carry[h]
        kvt = ckvt_ref[:, pl.ds(pl.multiple_of(c * kb, kb), kb)]
        p = jnp.exp2(stage_ref[slot, h] - m_new).astype(BF16)
        acc_ref[h] = acc_ref[h] * jnp.exp2(m_old - m_new) + _dot(kvt, p)
        return carry

    m_init = jnp.full((1, qb), -1e30, F32)
    skewed(H, attn_values, attn_logits, ((m_init, m_init),) * H)

    out_t = jnp.concatenate(
        [_dot(wuvt_ref[h], (acc_ref[h, 0:A_KV_RANK, :] / acc_ref[h, A_KV_RANK:A_KV_RANK + 1, :]).astype(BF16))
         for h in range(H)], axis=0)
    out_ref[...] = _rms_norm_rows(out_t.T, g_ref[...])


def _dsa(qidx, qlat, misct, kidx, ckv, ckvt, wuvt, g, *, B, S):
    T = B * S
    qb = min(256, S)
    nq = S // qb
    kb = qb
    rb = 2 * qb if nq % 2 == 0 else qb
    topk = min(TOPK_MAX, S // 4)
    idx_bits = max(1, (S - 1).bit_length())
    assert S < HALF16, "tie-breaking keeps key positions in int16"
    slab = min(64, kb)
    kern = functools.partial(_dsa_kernel, qb=qb, kb=kb, rb=rb, slab=slab, topk=topk, idx_bits=idx_bits)
    return pl.pallas_call(
        kern,
        grid=(B, nq),
        in_specs=[
            pl.BlockSpec((IDX_HEADS, IDX_DIM, qb), lambda b, j: (0, 0, b * nq + j)),
            pl.BlockSpec((A_HEADS, A_KV_RANK, qb), lambda b, j: (0, 0, b * nq + j)),
            pl.BlockSpec((IDX_HEADS, qb), lambda b, j: (MISC_WIDX // IDX_HEADS, b * nq + j)),
            pl.BlockSpec((S, IDX_DIM), lambda b, j: (b, 0)),
            pl.BlockSpec((S, A_KV_RANK), lambda b, j: (b, 0)),
            pl.BlockSpec((KVT_ROWS, S), lambda b, j: (0, b)),
            pl.BlockSpec((A_HEADS, A_V_DIM, A_KV_RANK), lambda b, j: (0, 0, 0)),
            pl.BlockSpec((1, A_OUT_W), lambda b, j: (0, 0)),
        ],
        out_specs=pl.BlockSpec((qb, A_OUT_W), lambda b, j: (b * nq + j, 0)),
        out_shape=jax.ShapeDtypeStruct((T, A_OUT_W), F32),
        scratch_shapes=[
            pltpu.VMEM((S, qb), I32),
            pltpu.VMEM((S, qb), I16),
            pltpu.VMEM((S, qb), I16),
            pltpu.VMEM((A_HEADS, KVT_ROWS, qb), F32),
            pltpu.VMEM((rb // kb, A_HEADS, kb, qb), F32),
            pltpu.VMEM((rb // kb, kb, qb), F32),
        ],
        compiler_params=pltpu.CompilerParams(
            dimension_semantics=("parallel", "arbitrary"), vmem_limit_bytes=VMEM_LIMIT),
        name="dsa",
    )(qidx, qlat, misct, kidx, ckv, ckvt, wuvt, g)


def _gdn_constants(sb):
    r, c = np.arange(sb)[:, None], np.arange(sb)[None, :]
    same = (r // GDN_CHUNK) == (c // GDN_CHUNK)
    tri = same & (c <= r)
    diag = [((r // b) == (c // b)).astype(np.float32) for b in (8, 16, 32)] + [same.astype(np.float32)]
    eye = (r == c).astype(np.float32)
    mats = [diag[0]] + [diag[i] - diag[i - 1] for i in (1, 2, 3)] + [eye, 1.0 - eye, np.where(tri, 0.0, -np.inf)]
    mask_c = np.concatenate([tri, same], axis=0).astype(np.float32)
    mask_r = np.concatenate([same & (r <= c), same], axis=1).astype(np.float32)
    spread = (np.arange(DN_KEY_W)[None, :] // DN_K_DIM == np.arange(DN_HEADS)[:, None]).astype(np.float32)
    spread3 = np.zeros((3, 3, 3, DN_HEADS, DN_KEY_W), np.float32)
    for j in range(3):
        spread3[j, :, j] = spread
    return (jnp.asarray(np.stack(mats), F32), jnp.asarray(mask_c, BF16), jnp.asarray(mask_r, BF16),
            jnp.asarray(spread3.reshape(3, 9 * DN_HEADS, DN_KEY_W), BF16), jnp.asarray(np.tile(spread, (1, 2)), BF16))


def _gdn_kernel(qkv_ref, z_ref, misc_ref, at_ref, cw_ref, alog_c_ref, dtb_c_ref, alog_r_ref, dtb_r_ref,
                ng_ref, fm_ref, mc_ref, mr_ref, sr_ref, sc_ref, out_ref, xbuf_ref, state_ref, *, tb, sb):
    C = GDN_CHUNK
    nck = tb // C
    H = DN_HEADS
    step = pl.program_id(1)

    @pl.when(step == 0)
    def _():
        xbuf_ref[0:SUBLANES, :] = jnp.zeros((SUBLANES, DN_QKV_W), F32)
        state_ref[...] = jnp.zeros(state_ref.shape, F32)

    xbuf_ref[SUBLANES:SUBLANES + tb, :] = qkv_ref[...]
    xe = xbuf_ref[...]
    y = xe[SUBLANES:] * cw_ref[DN_CONV - 1:DN_CONV, :]
    for d in range(1, DN_CONV):
        y = y + pltpu.roll(xe, d, 0)[SUBLANES:] * cw_ref[DN_CONV - 1 - d:DN_CONV - d, :]
    xbuf_ref[0:SUBLANES, :] = xe[tb:tb + SUBLANES]
    y = _silu(y)

    def softplus(v):
        return jnp.maximum(v, 0.0) + jnp.log(1.0 + jnp.exp(-jnp.abs(v)))

    misc = misc_ref[...]
    beta_c = _sigmoid(misc[:, MISC_B:MISC_B + H])
    g_c = -jnp.exp(alog_r_ref[...]) * softplus(misc[:, MISC_A:MISC_A + H] + dtb_r_ref[...])
    g_r = -jnp.exp(alog_c_ref[...]) * softplus(at_ref[...] + dtb_c_ref[...])

    rows = [slice(b * sb, (b + 1) * sb) for b in range(tb // sb)]
    mask_c = mc_ref[...]
    mask_r = mr_ref[...]

    def pieces(v, axis, n=3):
        out, rest = [], v
        for _ in range(n - 1):
            out.append(rest.astype(BF16))
            rest = rest - out[-1].astype(F32)
        return jnp.concatenate(out + [rest.astype(BF16)], axis=axis)

    sums_c = [_dot(mask_c, pieces(g_c[r], 1)) for r in rows]
    sums_c = [s[:, 0:H] + s[:, H:2 * H] + s[:, 2 * H:3 * H] for s in sums_c]
    sums_r = [_dot(pieces(g_r[:, r], 0), mask_r) for r in rows]
    sums_r = [s[0:H] + s[H:2 * H] + s[2 * H:3 * H] for s in sums_r]
    gcum_c = [s[0:sb] for s in sums_c]
    glast_c = [s[sb:2 * sb] for s in sums_c]
    gcum_r = [s[:, 0:sb] for s in sums_r]
    glast_r = [s[:, sb:2 * sb] for s in sums_r]
    blk8, ring_f, eye, off_diag, tri_bias = fm_ref[0], [fm_ref[1], fm_ref[2], fm_ref[3]], fm_ref[4], fm_ref[5], fm_ref[6]

    hs = range(H)
    bh = [(b, h) for b in range(len(rows)) for h in hs]
    ch = range(len(bh))
    def expand(*vs):
        p = pieces(jnp.concatenate(list(vs) + [vs[-1]] * (3 - len(vs)), axis=1), 1)
        return [_dot(p, sr_ref[j]) for j in range(len(vs))]

    head_sums = lambda v: _dot_nt(pieces(v, 1, 2), sc_ref[...])
    yq, yk, yv = y[:, :DN_KEY_W], y[:, DN_KEY_W:2 * DN_KEY_W], y[:, 2 * DN_KEY_W:]
    gc_all = jnp.concatenate(gcum_c, axis=0)
    eg_c = jnp.exp(gc_all)
    rq_x, rk_x, beta_x = expand(lax.rsqrt(head_sums(yq * yq) + RMS_EPS) * (DN_K_DIM ** -0.5),
                                lax.rsqrt(head_sums(yk * yk) + RMS_EPS), beta_c)
    eg_x, egl_x = expand(eg_c, jnp.exp(jnp.concatenate(glast_c, axis=0) - gc_all))
    q_n, k_n = yq * rq_x, yk * rk_x
    kb_all = k_n * beta_x
    k_bf, kb_bf, q_bf = k_n.astype(BF16), kb_all.astype(BF16), q_n.astype(BF16)
    rhs_v, rhs_k = (yv * beta_x).astype(BF16), (kb_all * eg_x).astype(BF16)
    qd_all = (q_n * eg_x).astype(BF16)
    ke_all = (k_n * egl_x).astype(BF16)
    head = lambda a, i: a[rows[bh[i][0]], bh[i][1] * DN_K_DIM:(bh[i][1] + 1) * DN_K_DIM]
    gcs = [gcum_c[b][:, h:h + 1] for b, h in bh]
    kt_bf = [k_n[r].T.astype(BF16) for r in rows]
    head_t = lambda i: kt_bf[bh[i][0]][bh[i][1] * DN_K_DIM:(bh[i][1] + 1) * DN_K_DIM, :]
    kk = [_dot(head(kb_bf, i), head_t(i)) for i in ch]
    qk = [_dot(head(q_bf, i), head_t(i)) for i in ch]
    decay = [jnp.exp((gcs[i] - gcum_r[b][h:h + 1, :]) + tri_bias) for i, (b, h) in enumerate(bh)]
    lower = [kk[i] * (decay[i] * off_diag) for i in ch]
    attn = [(qk[i] * decay[i]).astype(BF16) for i in ch]
    pw = [lower[i] * blk8 for i in ch]
    x_inv = [eye - pw[i] for i in ch]
    for _ in range(2):
        pwb = [p.astype(BF16) for p in pw]
        pw = [_dot(p, p) for p in pwb]
        x_inv = [x_inv[i] + _dot(x_inv[i].astype(BF16), pw[i].astype(BF16)) for i in ch]
    for ring in ring_f:
        xb = [x.astype(BF16) for x in x_inv]
        t1 = [_dot(xb[i], (lower[i] * ring).astype(BF16)).astype(BF16) for i in ch]
        x_inv = [x_inv[i] - _dot(t1[i], xb[i]) for i in ch]
    uw = [_dot(x_inv[i].astype(BF16), jnp.concatenate([head(rhs_v, i), head(rhs_k, i)], axis=1)) for i in ch]
    us = [m[:, :DN_V_DIM] for m in uw]
    wkb = [m[:, DN_V_DIM:].astype(BF16) for m in uw]
    qd = [head(qd_all, i) for i in ch]
    ke = [head(ke_all, i) for i in ch]

    st = [state_ref[h] for h in hs]
    outs = [[] for _ in hs]
    for n in range(nck):
        b, r0 = divmod(n * C, sb)
        r = slice(r0, r0 + C)
        ws = [_dot(jnp.concatenate([wkb[b * H + h][r], qd[b * H + h][r]], axis=0), st[h].astype(BF16)) for h in hs]
        v_nb = [(us[b * H + h][r] - ws[h][0:C]).astype(BF16) for h in hs]
        for h in hs:
            outs[h].append(ws[h][C:2 * C] + _dot(attn[b * H + h][r, r], v_nb[h]))
        cd = [jnp.exp(glast_r[b][h:h + 1, r0:r0 + 1]) for h in hs]
        st = [st[h] * cd[h] + _dot_tn(ke[b * H + h][r], v_nb[h]) for h in hs]
    for h in hs:
        state_ref[h] = st[h]
    o = jnp.concatenate([jnp.concatenate(outs[h], axis=0) for h in hs], axis=1)
    r_o = lax.rsqrt(head_sums(o * o) * (1.0 / DN_V_DIM) + RMS_EPS)
    o = o * expand(r_o)[0] * ng_ref[...]
    out_ref[...] = (o * _silu(z_ref[...])).astype(out_ref.dtype)


def _gdn(qkv, z, misc, misct, conv_w, a_log, dt_bias, norm_g, *, B, S):
    T = B * S
    sb = min(GDN_SOLVE, S)
    tb = min(GDN_SOLVES_PER_STEP * sb, S)
    ns = S // tb
    H = DN_HEADS
    kern = functools.partial(_gdn_kernel, tb=tb, sb=sb)
    consts = _gdn_constants(sb)
    const = lambda *s: pl.BlockSpec(s, lambda b, i: (0,) * len(s))
    return pl.pallas_call(
        kern,
        grid=(B, ns),
        in_specs=[
            pl.BlockSpec((tb, DN_QKV_W), lambda b, i: (b * ns + i, 0)),
            pl.BlockSpec((tb, DN_VAL_W), lambda b, i: (b * ns + i, 0)),
            pl.BlockSpec((tb, MISC_W), lambda b, i: (b * ns + i, 0)),
            pl.BlockSpec((H, tb), lambda b, i: (MISC_A // H, b * ns + i)),
            const(DN_CONV, DN_QKV_W), const(H, 1), const(H, 1), const(1, H), const(1, H), const(1, DN_VAL_W),
        ] + [const(*c.shape) for c in consts],
        out_specs=pl.BlockSpec((tb, DN_VAL_W), lambda b, i: (b * ns + i, 0)),
        out_shape=jax.ShapeDtypeStruct((T, DN_VAL_W), BF16),
        scratch_shapes=[
            pltpu.VMEM((tb + SUBLANES, DN_QKV_W), F32),
            pltpu.VMEM((H, DN_K_DIM, DN_V_DIM), F32),
        ],
        compiler_params=pltpu.CompilerParams(
            dimension_semantics=("parallel", "arbitrary"), vmem_limit_bytes=VMEM_LIMIT),
        name="gdn",
    )(qkv, z, misc, misct, conv_w, a_log.reshape(H, 1), dt_bias.reshape(H, 1),
      a_log.reshape(1, H), dt_bias.reshape(1, H), jnp.tile(norm_g.reshape(1, DN_V_DIM), (1, H)), *consts)


def _out_proj_kernel(a_ref, d_ref, x_ref, wa_ref, wd_ref, g_ref, b_ref, h_ref, *, alpha):
    for r in range(0, a_ref.shape[0], ROW_GROUP):
        rows = slice(r, r + ROW_GROUP)
        mix = _dot(a_ref[rows, :].astype(BF16), wa_ref[...]) + _dot(d_ref[rows, :], wd_ref[...])
        h_ref[rows, :] = _layer_norm_rows(alpha * x_ref[rows, :] + mix, g_ref[...], b_ref[...])


def _out_proj(a, d, x2, wa, wd, g, b, *, alpha, tm):
    T = x2.shape[0]
    const = lambda *s: pl.BlockSpec(s, lambda i: (0,) * len(s))
    return pl.pallas_call(
        functools.partial(_out_proj_kernel, alpha=alpha),
        grid=(T // tm,),
        in_specs=[
            pl.BlockSpec((tm, A_OUT_W), lambda i: (i, 0)),
            pl.BlockSpec((tm, DN_VAL_W), lambda i: (i, 0)),
            pl.BlockSpec((tm, D_MODEL), lambda i: (i, 0)),
            const(A_OUT_W, D_MODEL), const(DN_VAL_W, D_MODEL), const(1, D_MODEL), const(1, D_MODEL),
        ],
        out_specs=pl.BlockSpec((tm, D_MODEL), lambda i: (i, 0)),
        out_shape=jax.ShapeDtypeStruct((T, D_MODEL), F32),
        compiler_params=pltpu.CompilerParams(dimension_semantics=("parallel",), vmem_limit_bytes=VMEM_LIMIT),
        name="out_proj",
    )(a, d, x2, wa, wd, g, b)


def _ffn_kernel(h_ref, halo_ref, wg_ref, wu_ref, cw_ref, cb_ref, wd_ref, g_ref, b_ref, out_ref,
                acc_ref, *, alpha, tiles_per_seq):
    i = pl.program_id(0)
    f = pl.program_id(1)
    nf = pl.num_programs(1)
    hb = h_ref[...].astype(BF16)
    wg = wg_ref[...]
    gate = _dot(hb, wg)
    up = _dot(hb, wu_ref[...])
    halo = _dot(halo_ref[...].astype(BF16), wg)
    halo = jnp.where(i % tiles_per_seq == 0, 0.0, halo)
    ge = jnp.concatenate([halo, gate], axis=0)
    conv = gate * cw_ref[FFN_CONV - 1:FFN_CONV, :]
    for d in range(1, FFN_CONV):
        conv = conv + pltpu.roll(ge, d, 0)[SUBLANES:] * cw_ref[FFN_CONV - 1 - d:FFN_CONV - d, :]
    act = (_silu(conv + cb_ref[...]) * up).astype(BF16)
    part = _dot(act, wd_ref[...])

    @pl.when(f == 0)
    def _():
        acc_ref[...] = part

    @pl.when(f > 0)
    def _():
        acc_ref[...] += part

    @pl.when(f == nf - 1)
    def _():
        out_ref[...] = _layer_norm_rows(alpha * h_ref[...] + acc_ref[...], g_ref[...], b_ref[...])


def _ffn(h, wg, wu, cw, cb, wd, g, b, *, alpha, S, tm, tf):
    T = h.shape[0]
    nf = D_FF // tf
    hs = tm // SUBLANES
    kern = functools.partial(_ffn_kernel, alpha=alpha, tiles_per_seq=S // tm)
    return pl.pallas_call(
        kern,
        grid=(T // tm, nf),
        in_specs=[
            pl.BlockSpec((tm, D_MODEL), lambda i, f: (i, 0)),
            pl.BlockSpec((SUBLANES, D_MODEL), lambda i, f: (jnp.maximum(i * hs - 1, 0), 0)),
            pl.BlockSpec((D_MODEL, tf), lambda i, f: (0, f)),
            pl.BlockSpec((D_MODEL, tf), lambda i, f: (0, f)),
            pl.BlockSpec((FFN_CONV, tf), lambda i, f: (0, f)),
            pl.BlockSpec((1, tf), lambda i, f: (0, f)),
            pl.BlockSpec((tf, D_MODEL), lambda i, f: (f, 0)),
            pl.BlockSpec((1, D_MODEL), lambda i, f: (0, 0)),
            pl.BlockSpec((1, D_MODEL), lambda i, f: (0, 0)),
        ],
        out_specs=pl.BlockSpec((tm, D_MODEL), lambda i, f: (i, 0)),
        out_shape=jax.ShapeDtypeStruct((T, D_MODEL), F32),
        scratch_shapes=[pltpu.VMEM((tm, D_MODEL), F32)],
        compiler_params=pltpu.CompilerParams(
            dimension_semantics=("parallel", "arbitrary"), vmem_limit_bytes=VMEM_LIMIT),
        name="ffn",
    )(h, h, wg, wu, cw, cb, wd, g, b)


def _regroup_w_in(w):
    offs = [0]
    for s in IN_SIZES:
        offs.append(offs[-1] + s)
    cq, ckv, kidx, widx, qkv, z, b, a = (w[:, offs[i]:offs[i + 1]] for i in range(8))
    pad = jnp.zeros((w.shape[0], MISC_W - (IDX_DIM + IDX_HEADS + 2 * DN_HEADS)), w.dtype)
    return jnp.concatenate([cq, ckv, kidx, widx, b, a, pad, qkv, z], axis=1).astype(BF16)


def _layer(x2, p, *, B, S, alpha):
    row = lambda v: v.reshape(1, -1)
    wc = _regroup_w_in(p["w_in"])
    wuk_bd_t = jnp.einsum("hdc,hg->hcgd", p["w_uk"], jnp.eye(A_HEADS, dtype=F32)).reshape(
        A_HEADS * A_KV_RANK, A_HEADS * A_QK_DIM)
    qlat, qidx, ckv, ckvt, kidx, misc, misct, qkv, z = _in_proj(
        x2, wc, row(p["q_norm_g"]), p["w_uq"].T.astype(BF16), wuk_bd_t.astype(BF16), p["w_qidx"].T.astype(BF16),
        row(p["kv_norm_g"]),
        row(p["kidx_ln_g"]), row(p["kidx_ln_b"]), tm=min(512, S))
    a_out = _dsa(qidx, qlat, misct, kidx, ckv, ckvt, p["w_uv"].transpose(0, 2, 1).astype(BF16),
                 row(p["attn_out_g"]), B=B, S=S)
    d_out = _gdn(qkv, z, misc, misct, p["dn_conv_w"], p["dn_a_log"], p["dn_dt_bias"], p["dn_norm_g"],
                 B=B, S=S)
    w_out = p["w_out"].astype(BF16)
    h = _out_proj(a_out, d_out, x2, w_out[:A_OUT_W], w_out[A_OUT_W:], row(p["ln1_g"]), row(p["ln1_b"]),
                  alpha=alpha, tm=min(512, S))
    w_ffn = p["ffn_w_in"].astype(BF16)
    return _ffn(h, w_ffn[:, :D_FF], w_ffn[:, D_FF:], p["ffn_conv_w"], row(p["ffn_conv_b"]),
                p["ffn_w_down"].astype(BF16), row(p["ln2_g"]), row(p["ln2_b"]),
                alpha=alpha, S=S, tm=min(512, S), tf=D_FF // 2)


_PARAM_NAMES = ("w_in", "q_norm_g", "w_uq", "w_qidx", "kv_norm_g", "w_uk", "w_uv", "kidx_ln_g", "kidx_ln_b",
                "attn_out_g", "dn_conv_w", "dn_a_log", "dn_dt_bias", "dn_norm_g", "w_out", "ln1_g", "ln1_b",
                "ffn_w_in", "ffn_conv_w", "ffn_conv_b", "ffn_w_down", "ln2_g", "ln2_b")


def kernel(x, w_in, q_norm_g, w_uq, w_qidx, kv_norm_g, w_uk, w_uv, kidx_ln_g, kidx_ln_b, attn_out_g, dn_conv_w, dn_a_log, dn_dt_bias, dn_norm_g, w_out, ln1_g, ln1_b, ffn_w_in, ffn_conv_w, ffn_conv_b, ffn_w_down, ln2_g, ln2_b):
    params = (w_in, q_norm_g, w_uq, w_qidx, kv_norm_g, w_uk, w_uv, kidx_ln_g, kidx_ln_b, attn_out_g, dn_conv_w,
              dn_a_log, dn_dt_bias, dn_norm_g, w_out, ln1_g, ln1_b, ffn_w_in, ffn_conv_w, ffn_conv_b, ffn_w_down,
              ln2_g, ln2_b)
    B, S, D = x.shape
    depth = w_in.shape[0]
    alpha = (2 * depth) ** 0.25
    x2 = x.reshape(B * S, D)
    for l in range(depth):
        x2 = _layer(x2, {n: v[l] for n, v in zip(_PARAM_NAMES, params)}, B=B, S=S, alpha=alpha)
    return x2.reshape(B, S, D)
```

```python
import functools

import jax
import jax.numpy as jnp
import numpy as np
from jax import lax
from jax.experimental import pallas as pl
from jax.experimental.pallas import tpu as pltpu

F32 = jnp.float32
BF16 = jnp.bfloat16
I32 = jnp.int32
I16 = jnp.int16

D_MODEL = 1024
A_HEADS = 8
A_QK_DIM = 64
A_V_DIM = 64
A_Q_RANK = 256
A_KV_RANK = 128
IDX_HEADS = 8
IDX_DIM = 64
TOPK_MAX = 256
DN_HEADS = 8
DN_K_DIM = 64
DN_V_DIM = 64
DN_CONV = 4
D_FF = 2816
FFN_CONV = 3
RMS_EPS = 1e-6
LN_EPS = 1e-5

A_OUT_W = A_HEADS * A_V_DIM
DN_KEY_W = DN_HEADS * DN_K_DIM
DN_VAL_W = DN_HEADS * DN_V_DIM
DN_QKV_W = 2 * DN_KEY_W + DN_VAL_W
IN_SIZES = (A_Q_RANK, A_KV_RANK, IDX_DIM, IDX_HEADS, DN_QKV_W, DN_VAL_W, DN_HEADS, DN_HEADS)

MISC_W = 128
MISC_WIDX = IDX_DIM
MISC_B = IDX_DIM + IDX_HEADS
MISC_A = MISC_B + DN_HEADS
PROJ_W = A_Q_RANK + A_KV_RANK + MISC_W + DN_QKV_W + DN_VAL_W

LOG2E = 1.4426950408889634
SUBLANES = 8
PACK16 = 16
KVT_ROWS = A_KV_RANK + PACK16
HALF16 = 2 ** 15
INT_MIN = -(2 ** 31)
NEG_KEY = INT_MIN + 0x00800000
VMEM_LIMIT = 56 * 1024 * 1024

TIE_SCAN_MAX = 8
ROW_GROUP = 256
GDN_CHUNK = 64
GDN_SOLVE = 256
GDN_SOLVES_PER_STEP = 1


def _dot(a, b):
    return jnp.dot(a, b, preferred_element_type=F32)


def _dot_nt(a, b):
    return lax.dot_general(a, b, (((1,), (1,)), ((), ())), preferred_element_type=F32)


def _dot_tn(a, b):
    return lax.dot_general(a, b, (((0,), (0,)), ((), ())), preferred_element_type=F32)


def _sigmoid(x):
    return 1.0 / (1.0 + jnp.exp(-x))


def _silu(x):
    return x * _sigmoid(x)


def _layer_norm_rows(v, g, b):
    mu = jnp.mean(v, axis=-1, keepdims=True)
    d = v - mu
    var = jnp.mean(d * d, axis=-1, keepdims=True)
    return d * lax.rsqrt(var + LN_EPS) * g + b


def _rms_norm_rows(v, g):
    return v * lax.rsqrt(jnp.mean(v * v, axis=-1, keepdims=True) + RMS_EPS) * g


def _in_proj_kernel(x_ref, w_ref, qg_ref, wuq_ref, wuk_ref, wqi_ref, kvg_ref, lng_ref, lnb_ref,
                    qlat_ref, qidx_ref, ckv_ref, ckvt_ref, kidx_ref, misc_ref, misct_ref, qkv_ref, z_ref):
    xb = x_ref[...].astype(BF16)
    proj = _dot(xb, w_ref[...])
    o = 0
    c_q = proj[:, o:o + A_Q_RANK]; o += A_Q_RANK
    c_kv = proj[:, o:o + A_KV_RANK]; o += A_KV_RANK
    misc = proj[:, o:o + MISC_W]; o += MISC_W
    qkv_ref[...] = proj[:, o:o + DN_QKV_W]; o += DN_QKV_W
    z_ref[...] = proj[:, o:o + DN_VAL_W]

    cqn_t = _rms_norm_rows(c_q, qg_ref[...]).T.astype(BF16)
    q_t = _dot(wuq_ref[...], cqn_t).astype(BF16)
    qlat_t = _dot(wuk_ref[...], q_t) * (A_QK_DIM ** -0.5 * LOG2E)
    qidx_t = _dot(wqi_ref[...], cqn_t)
    for h in range(A_HEADS):
        qlat_ref[h] = qlat_t[h * A_KV_RANK:(h + 1) * A_KV_RANK, :].astype(BF16)
        qidx_ref[h] = qidx_t[h * IDX_DIM:(h + 1) * IDX_DIM, :].astype(BF16)
    ckv = _rms_norm_rows(c_kv, kvg_ref[...])
    ckv_ref[...] = ckv.astype(BF16)
    ckvt_ref[0:A_KV_RANK, :] = ckv.T.astype(BF16)
    ckvt_ref[A_KV_RANK:KVT_ROWS, :] = jnp.ones((KVT_ROWS - A_KV_RANK, ckv.shape[0]), BF16)
    kidx_ref[...] = _layer_norm_rows(misc[:, :IDX_DIM], lng_ref[...], lnb_ref[...]).astype(BF16)
    misc_ref[...] = misc
    misct_ref[...] = misc.T


def _in_proj(x2, wc, qg, wuq, wuk, wqi, kvg, lng, lnb, *, tm):
    T = x2.shape[0]
    const = lambda *s: pl.BlockSpec(s, lambda i: (0,) * len(s))
    return pl.pallas_call(
        _in_proj_kernel,
        grid=(T // tm,),
        in_specs=[
            pl.BlockSpec((tm, D_MODEL), lambda i: (i, 0)),
            const(D_MODEL, PROJ_W), const(1, A_Q_RANK), const(A_HEADS * A_QK_DIM, A_Q_RANK),
            const(A_HEADS * A_KV_RANK, A_HEADS * A_QK_DIM), const(IDX_HEADS * IDX_DIM, A_Q_RANK),
            const(1, A_KV_RANK), const(1, IDX_DIM), const(1, IDX_DIM),
        ],
        out_specs=[
            pl.BlockSpec((A_HEADS, A_KV_RANK, tm), lambda i: (0, 0, i)),
            pl.BlockSpec((IDX_HEADS, IDX_DIM, tm), lambda i: (0, 0, i)),
            pl.BlockSpec((tm, A_KV_RANK), lambda i: (i, 0)),
            pl.BlockSpec((KVT_ROWS, tm), lambda i: (0, i)),
            pl.BlockSpec((tm, IDX_DIM), lambda i: (i, 0)),
            pl.BlockSpec((tm, MISC_W), lambda i: (i, 0)),
            pl.BlockSpec((MISC_W, tm), lambda i: (0, i)),
            pl.BlockSpec((tm, DN_QKV_W), lambda i: (i, 0)),
            pl.BlockSpec((tm, DN_VAL_W), lambda i: (i, 0)),
        ],
        out_shape=[
            jax.ShapeDtypeStruct((A_HEADS, A_KV_RANK, T), BF16),
            jax.ShapeDtypeStruct((IDX_HEADS, IDX_DIM, T), BF16),
            jax.ShapeDtypeStruct((T, A_KV_RANK), BF16),
            jax.ShapeDtypeStruct((KVT_ROWS, T), BF16),
            jax.ShapeDtypeStruct((T, IDX_DIM), BF16),
            jax.ShapeDtypeStruct((T, MISC_W), F32),
            jax.ShapeDtypeStruct((MISC_W, T), F32),
            jax.ShapeDtypeStruct((T, DN_QKV_W), F32),
            jax.ShapeDtypeStruct((T, DN_VAL_W), F32),
        ],
        compiler_params=pltpu.CompilerParams(dimension_semantics=("parallel",), vmem_limit_bytes=VMEM_LIMIT),
        name="in_proj",
    )(x2, wc, qg, wuq, wuk, wqi, kvg, lng, lnb)


def _dsa_kernel(qidx_ref, qlat_ref, wt_ref, kidx_ref, ckv_ref, ckvt_ref, wuvt_ref, g_ref, out_ref,
                keys_ref, hi_ref, lo_ref, acc_ref, stage_ref, bias_ref, *, qb, kb, rb, slab, topk, idx_bits):
    H = A_HEADS
    j = pl.program_id(1)
    q0 = j * qb
    nkeys = (j + 1) * qb
    nrblk = (nkeys + rb - 1) // rb
    ups = rb // kb
    nchunk = nrblk * ups
    w = wt_ref[...] * (IDX_HEADS ** -0.5 * IDX_DIM ** -0.5)
    row_minus_lane = (lax.broadcasted_iota(I32, (slab, qb), 0) - lax.broadcasted_iota(I32, (slab, qb), 1))

    def skewed(nparts, consume, produce, carry, on_tail=None):
        def chunk(c, k, carry):
            if ups > 1:
                for i in range(nparts):
                    carry = produce(c + 1, (k + 1) % ups, i, consume(c, k, i, carry))
                return carry
            for i in range(nparts):
                carry = consume(c, k, i, carry)
            for i in range(nparts):
                carry = produce(c + 1, k, i, carry)
            return carry

        def step(i, carry):
            for k in range(ups):
                carry = chunk(i * ups + k, k, carry)
            return carry

        for i in range(nparts):
            carry = produce(0, 0, i, carry)
        carry = lax.fori_loop(0, nrblk - 1, step, carry)
        for k in range(ups - 1):
            carry = chunk((nrblk - 1) * ups + k, k, carry)
        last = nchunk - 1

        @pl.when(last * kb < nkeys)
        def _():
            for i in range(nparts):
                consume(last, ups - 1, i, carry)

        if on_tail is not None:
            @pl.when(last * kb >= nkeys)
            def _():
                on_tail(last)
        return carry

    nslab = kb // slab
    hps = H // nslab

    def score_matmuls(c, slot, i, carry):
        kc = kidx_ref[pl.ds(pl.multiple_of(c * kb, kb), kb), :]
        for h in range(i * hps, (i + 1) * hps):
            stage_ref[slot, h] = _dot(kc, qidx_ref[h])
        return carry

    def score_keys(c, slot, i, carry):
        r0 = pl.multiple_of(c * kb, kb) + i * slab
        s = jnp.zeros((slab, qb), F32)
        for h in range(H):
            s = s + jnp.maximum(stage_ref[slot, h, i * slab:(i + 1) * slab, :], 0.0) * w[h:h + 1, :]
        bits = lax.bitcast_convert_type(s, I32)
        key = jnp.where(bits < 0, INT_MIN - bits, bits)
        key = jnp.where(row_minus_lane <= q0 - r0, key, NEG_KEY)
        keys_ref[pl.ds(r0, slab), :] = key
        hi_ref[pl.ds(r0, slab), :] = jnp.right_shift(key, 16).astype(I16)
        lo_ref[pl.ds(r0, slab), :] = ((key & 0xFFFF) - HALF16).astype(I16)
        return carry

    def score_tail(c):
        r0 = pl.multiple_of(c * kb, kb)
        keys_ref[pl.ds(r0, kb), :] = jnp.full((kb, qb), NEG_KEY, I32)
        hi_ref[pl.ds(r0, kb), :] = jnp.full((kb, qb), NEG_KEY >> 16, I16)
        lo_ref[pl.ds(r0, kb), :] = jnp.full((kb, qb), (NEG_KEY & 0xFFFF) - HALF16, I16)

    skewed(nslab, score_keys, score_matmuls, 0, score_tail)

    nacc = 4

    def count16(ref, cand, below=False):
        def body(r, accs):
            r0 = pl.multiple_of(r * rb, rb)
            k = ref[pl.ds(r0, rb), :]
            m = jnp.where(k < cand if below else k >= cand, jnp.int16(1), jnp.int16(0))
            accs = list(accs)
            for i in range(rb // PACK16):
                accs[i % nacc] = accs[i % nacc] + m[i * PACK16:(i + 1) * PACK16, :]
            return tuple(accs)
        accs = lax.fori_loop(0, nrblk, body, tuple(jnp.zeros((PACK16, qb), I16) for _ in range(nacc)))
        tot = accs[0].astype(I32)
        for a in accs[1:]:
            tot = tot + a.astype(I32)
        return tot.sum(axis=0, keepdims=True)

    def max16_below(ref, bound):
        def body(r, accs):
            r0 = pl.multiple_of(r * rb, rb)
            k = ref[pl.ds(r0, rb), :]
            m = jnp.where(k < bound, k, jnp.int16(-1))
            accs = list(accs)
            for i in range(rb // PACK16):
                tile = m[i * PACK16:(i + 1) * PACK16, :]
                accs[i % nacc] = jnp.where(tile > accs[i % nacc], tile, accs[i % nacc])
            return tuple(accs)
        accs = lax.fori_loop(0, nrblk, body, tuple(jnp.full((PACK16, qb), -1, I16) for _ in range(nacc)))
        top = accs[0].astype(I32)
        for a in accs[1:]:
            top = jnp.maximum(top, a.astype(I32))
        return top.max(axis=0, keepdims=True)

    def bisect16(ref, target, cnt_all):
        def body(p, carry):
            t_u, cnt_ok, cnt_rej = carry
            cand_u = t_u | jnp.left_shift(jnp.int32(1), 15 - p)
            cnt = count16(ref, (cand_u - HALF16).astype(I16))
            ok = cnt >= target
            return jnp.where(ok, cand_u, t_u), jnp.where(ok, cnt, cnt_ok), jnp.where(ok, cnt_rej, cnt)
        return lax.fori_loop(0, 16, body, (jnp.zeros((1, qb), I32), cnt_all, jnp.zeros((1, qb), I32)))

    hi_u, cnt_ge_hi, cnt_gt_hi = bisect16(hi_ref, topk, jnp.full((1, qb), nrblk * rb, I32))
    hi_t = (hi_u - HALF16).astype(I16)

    def mask_lo(r, carry):
        r0 = pl.multiple_of(r * rb, rb)
        lo_ref[pl.ds(r0, rb), :] = jnp.where(hi_ref[pl.ds(r0, rb), :] == hi_t, lo_ref[pl.ds(r0, rb), :],
                                             jnp.int16(-HALF16))
        return carry

    lax.fori_loop(0, nrblk, mask_lo, 0)
    lo_u, cnt_ge_lo, cnt_gt_lo = bisect16(lo_ref, topk - cnt_gt_hi, cnt_ge_hi - cnt_gt_hi)
    t = jnp.left_shift(hi_u - HALF16, 16) + lo_u
    cnt_t = cnt_gt_hi + cnt_ge_lo

    excess = jnp.where(t > NEG_KEY, cnt_t - topk, 0)

    worst = jnp.max(excess)

    @pl.when(worst > 0)
    def _():
        keep = topk - (cnt_gt_hi + cnt_gt_lo)
        lo_t = (lo_u - HALF16).astype(I16)

        def tie_rows(r, carry):
            r0 = pl.multiple_of(r * rb, rb)
            tie = (hi_ref[pl.ds(r0, rb), :] == hi_t) & (lo_ref[pl.ds(r0, rb), :] == lo_t)
            row = (r0 + lax.broadcasted_iota(I32, (rb, qb), 0)).astype(I16)
            lo_ref[pl.ds(r0, rb), :] = jnp.where(tie, row, jnp.int16(HALF16 - 1))
            return carry

        lax.fori_loop(0, nrblk, tie_rows, 0)

        def demote_from(first):
            def demote(r, carry):
                r0 = pl.multiple_of(r * rb, rb)
                k = keys_ref[pl.ds(r0, rb), :]
                row = r0 + lax.broadcasted_iota(I32, (rb, qb), 0)
                keys_ref[pl.ds(r0, rb), :] = jnp.where((k == t) & (row >= first), NEG_KEY, k)
                return carry
            lax.fori_loop(0, nrblk, demote, 0)

        @pl.when(worst <= TIE_SCAN_MAX)
        def _():
            def peel(c):
                first, left = c
                top = max16_below(lo_ref, first.astype(I16))
                return jnp.where(left > 0, top, first), jnp.maximum(left - 1, 0)
            first, _ = lax.while_loop(lambda c: jnp.max(c[1]) > 0, peel,
                                      (jnp.full((1, qb), HALF16 - 1, I32), jnp.maximum(excess, 0)))
            demote_from(first)

        @pl.when(worst > TIE_SCAN_MAX)
        def _():
            def idx_body(p, pos):
                cand = pos + jnp.left_shift(jnp.int32(1), idx_bits - 1 - p)
                return jnp.where(count16(lo_ref, cand.astype(I16), below=True) < keep, cand, pos)
            demote_from(lax.fori_loop(0, idx_bits, idx_body, jnp.zeros((1, qb), I32)) + 1)

    thr = jnp.maximum(t, NEG_KEY + 1)

    acc_ref[...] = jnp.zeros(acc_ref.shape, F32)

    def put(carry, h, item):
        return carry[:h] + (item,) + carry[h + 1:]

    def attn_logits(c, slot, h, carry):
        r0 = pl.multiple_of(c * kb, kb)
        if h == 0:
            bias_ref[slot] = jnp.where(keys_ref[pl.ds(r0, kb), :] >= thr, 0.0, -jnp.inf)
        m_prev = carry[h][1]
        lg = _dot(ckv_ref[pl.ds(r0, kb), :], qlat_ref[h]) + bias_ref[slot]
        stage_ref[slot, h] = lg
        return put(carry, h, (m_prev, jnp.maximum(m_prev, jnp.max(lg, axis=0, keepdims=True))))

    def attn_values(c, slot, h, carry):
        m_old, m_new = carry[h]
        kvt = ckvt_ref[:, pl.ds(pl.multiple_of(c * kb, kb), kb)]
        p = jnp.exp2(stage_ref[slot, h] - m_new).astype(BF16)
        acc_ref[h] = acc_ref[h] * jnp.exp2(m_old - m_new) + _dot(kvt, p)
        return carry

    m_init = jnp.full((1, qb), -1e30, F32)
    skewed(H, attn_values, attn_logits, ((m_init, m_init),) * H)

    out_t = jnp.concatenate(
        [_dot(wuvt_ref[h], (acc_ref[h, 0:A_KV_RANK, :] / acc_ref[h, A_KV_RANK:A_KV_RANK + 1, :]).astype(BF16))
         for h in range(H)], axis=0)
    out_ref[...] = _rms_norm_rows(out_t.T, g_ref[...])


def _dsa(qidx, qlat, misct, kidx, ckv, ckvt, wuvt, g, *, B, S):
    T = B * S
    qb = min(256, S)
    nq = S // qb
    kb = qb
    rb = 2 * qb if nq % 2 == 0 else qb
    topk = min(TOPK_MAX, S // 4)
    idx_bits = max(1, (S - 1).bit_length())
    assert S < HALF16, "tie-breaking keeps key positions in int16"
    slab = min(64, kb)
    kern = functools.partial(_dsa_kernel, qb=qb, kb=kb, rb=rb, slab=slab, topk=topk, idx_bits=idx_bits)
    return pl.pallas_call(
        kern,
        grid=(B, nq),
        in_specs=[
            pl.BlockSpec((IDX_HEADS, IDX_DIM, qb), lambda b, j: (0, 0, b * nq + j)),
            pl.BlockSpec((A_HEADS, A_KV_RANK, qb), lambda b, j: (0, 0, b * nq + j)),
            pl.BlockSpec((IDX_HEADS, qb), lambda b, j: (MISC_WIDX // IDX_HEADS, b * nq + j)),
            pl.BlockSpec((S, IDX_DIM), lambda b, j: (b, 0)),
            pl.BlockSpec((S, A_KV_RANK), lambda b, j: (b, 0)),
            pl.BlockSpec((KVT_ROWS, S), lambda b, j: (0, b)),
            pl.BlockSpec((A_HEADS, A_V_DIM, A_KV_RANK), lambda b, j: (0, 0, 0)),
            pl.BlockSpec((1, A_OUT_W), lambda b, j: (0, 0)),
        ],
        out_specs=pl.BlockSpec((qb, A_OUT_W), lambda b, j: (b * nq + j, 0)),
        out_shape=jax.ShapeDtypeStruct((T, A_OUT_W), F32),
        scratch_shapes=[
            pltpu.VMEM((S, qb), I32),
            pltpu.VMEM((S, qb), I16),
            pltpu.VMEM((S, qb), I16),
            pltpu.VMEM((A_HEADS, KVT_ROWS, qb), F32),
            pltpu.VMEM((rb // kb, A_HEADS, kb, qb), F32),
            pltpu.VMEM((rb // kb, kb, qb), F32),
        ],
        compiler_params=pltpu.CompilerParams(
            dimension_semantics=("parallel", "arbitrary"), vmem_limit_bytes=VMEM_LIMIT),
        name="dsa",
    )(qidx, qlat, misct, kidx, ckv, ckvt, wuvt, g)


def _gdn_constants(sb):
    r, c = np.arange(sb)[:, None], np.arange(sb)[None, :]
    same = (r // GDN_CHUNK) == (c // GDN_CHUNK)
    tri = same & (c <= r)
    diag = [((r // b) == (c // b)).astype(np.float32) for b in (8, 16, 32)] + [same.astype(np.float32)]
    eye = (r == c).astype(np.float32)
    mats = [diag[0]] + [diag[i] - diag[i - 1] for i in (1, 2, 3)] + [eye, 1.0 - eye, np.where(tri, 0.0, -np.inf)]
    mask_c = np.concatenate([tri, same], axis=0).astype(np.float32)
    mask_r = np.concatenate([same & (r <= c), same], axis=1).astype(np.float32)
    spread = (np.arange(DN_KEY_W)[None, :] // DN_K_DIM == np.arange(DN_HEADS)[:, None]).astype(np.float32)
    spread3 = np.zeros((3, 3, 3, DN_HEADS, DN_KEY_W), np.float32)
    for j in range(3):
        spread3[j, :, j] = spread
    return (jnp.asarray(np.stack(mats), F32), jnp.asarray(mask_c, BF16), jnp.asarray(mask_r, BF16),
            jnp.asarray(spread3.reshape(3, 9 * DN_HEADS, DN_KEY_W), BF16), jnp.asarray(np.tile(spread, (1, 2)), BF16))


def _gdn_kernel(qkv_ref, z_ref, misc_ref, at_ref, cw_ref, alog_c_ref, dtb_c_ref, alog_r_ref, dtb_r_ref,
                ng_ref, fm_ref, mc_ref, mr_ref, sr_ref, sc_ref, out_ref, xbuf_ref, state_ref, *, tb, sb):
    C = GDN_CHUNK
    nck = tb // C
    H = DN_HEADS
    step = pl.program_id(1)

    @pl.when(step == 0)
    def _():
        xbuf_ref[0:SUBLANES, :] = jnp.zeros((SUBLANES, DN_QKV_W), F32)
        state_ref[...] = jnp.zeros(state_ref.shape, F32)

    xbuf_ref[SUBLANES:SUBLANES + tb, :] = qkv_ref[...]
    xe = xbuf_ref[...]
    y = xe[SUBLANES:] * cw_ref[DN_CONV - 1:DN_CONV, :]
    for d in range(1, DN_CONV):
        y = y + pltpu.roll(xe, d, 0)[SUBLANES:] * cw_ref[DN_CONV - 1 - d:DN_CONV - d, :]
    xbuf_ref[0:SUBLANES, :] = xe[tb:tb + SUBLANES]
    y = _silu(y)

    def softplus(v):
        return jnp.maximum(v, 0.0) + jnp.log(1.0 + jnp.exp(-jnp.abs(v)))

    misc = misc_ref[...]
    beta_c = _sigmoid(misc[:, MISC_B:MISC_B + H])
    g_c = -jnp.exp(alog_r_ref[...]) * softplus(misc[:, MISC_A:MISC_A + H] + dtb_r_ref[...])
    g_r = -jnp.exp(alog_c_ref[...]) * softplus(at_ref[...] + dtb_c_ref[...])

    rows = [slice(b * sb, (b + 1) * sb) for b in range(tb // sb)]
    mask_c = mc_ref[...]
    mask_r = mr_ref[...]

    def pieces(v, axis, n=3):
        out, rest = [], v
        for _ in range(n - 1):
            out.append(rest.astype(BF16))
            rest = rest - out[-1].astype(F32)
        return jnp.concatenate(out + [rest.astype(BF16)], axis=axis)

    sums_c = [_dot(mask_c, pieces(g_c[r], 1)) for r in rows]
    sums_c = [s[:, 0:H] + s[:, H:2 * H] + s[:, 2 * H:3 * H] for s in sums_c]
    sums_r = [_dot(pieces(g_r[:, r], 0), mask_r) for r in rows]
    sums_r = [s[0:H] + s[H:2 * H] + s[2 * H:3 * H] for s in sums_r]
    gcum_c = [s[0:sb] for s in sums_c]
    glast_c = [s[sb:2 * sb] for s in sums_c]
    gcum_r = [s[:, 0:sb] for s in sums_r]
    glast_r = [s[:, sb:2 * sb] for s in sums_r]
    blk8, ring_f, eye, off_diag, tri_bias = fm_ref[0], [fm_ref[1], fm_ref[2], fm_ref[3]], fm_ref[4], fm_ref[5], fm_ref[6]

    hs = range(H)
    bh = [(b, h) for b in range(len(rows)) for h in hs]
    ch = range(len(bh))
    def expand(*vs):
        p = pieces(jnp.concatenate(list(vs) + [vs[-1]] * (3 - len(vs)), axis=1), 1)
        return [_dot(p, sr_ref[j]) for j in range(len(vs))]

    head_sums = lambda v: _dot_nt(pieces(v, 1, 2), sc_ref[...])
    yq, yk, yv = y[:, :DN_KEY_W], y[:, DN_KEY_W:2 * DN_KEY_W], y[:, 2 * DN_KEY_W:]
    gc_all = jnp.concatenate(gcum_c, axis=0)
    eg_c = jnp.exp(gc_all)
    rq_x, rk_x, beta_x = expand(lax.rsqrt(head_sums(yq * yq) + RMS_EPS) * (DN_K_DIM ** -0.5),
                                lax.rsqrt(head_sums(yk * yk) + RMS_EPS), beta_c)
    eg_x, egl_x = expand(eg_c, jnp.exp(jnp.concatenate(glast_c, axis=0) - gc_all))
    q_n, k_n = yq * rq_x, yk * rk_x
    kb_all = k_n * beta_x
    k_bf, kb_bf, q_bf = k_n.astype(BF16), kb_all.astype(BF16), q_n.astype(BF16)
    rhs_v, rhs_k = (yv * beta_x).astype(BF16), (kb_all * eg_x).astype(BF16)
    qd_all = (q_n * eg_x).astype(BF16)
    ke_all = (k_n * egl_x).astype(BF16)
    head = lambda a, i: a[rows[bh[i][0]], bh[i][1] * DN_K_DIM:(bh[i][1] + 1) * DN_K_DIM]
    gcs = [gcum_c[b][:, h:h + 1] for b, h in bh]
    kk = [_dot_nt(head(kb_bf, i), head(k_bf, i)) for i in ch]
    qk = [_dot_nt(head(q_bf, i), head(k_bf, i)) for i in ch]
    decay = [jnp.exp((gcs[i] - gcum_r[b][h:h + 1, :]) + tri_bias) for i, (b, h) in enumerate(bh)]
    lower = [kk[i] * (decay[i] * off_diag) for i in ch]
    attn = [(qk[i] * decay[i]).astype(BF16) for i in ch]
    pw = [lower[i] * blk8 for i in ch]
    x_inv = [eye - pw[i] for i in ch]
    for _ in range(2):
        pwb = [p.astype(BF16) for p in pw]
        pw = [_dot(p, p) for p in pwb]
        x_inv = [x_inv[i] + _dot(x_inv[i].astype(BF16), pw[i].astype(BF16)) for i in ch]
    for ring in ring_f:
        xb = [x.astype(BF16) for x in x_inv]
        t1 = [_dot(xb[i], (lower[i] * ring).astype(BF16)).astype(BF16) for i in ch]
        x_inv = [x_inv[i] - _dot(t1[i], xb[i]) for i in ch]
    uw = [_dot(x_inv[i].astype(BF16), jnp.concatenate([head(rhs_v, i), head(rhs_k, i)], axis=1)) for i in ch]
    us = [m[:, :DN_V_DIM] for m in uw]
    wkb = [m[:, DN_V_DIM:].astype(BF16) for m in uw]
    qd = [head(qd_all, i) for i in ch]
    ke = [head(ke_all, i) for i in ch]

    st = [state_ref[h] for h in hs]
    outs = [[] for _ in hs]
    for n in range(nck):
        b, r0 = divmod(n * C, sb)
        r = slice(r0, r0 + C)
        ws = [_dot(jnp.concatenate([wkb[b * H + h][r], qd[b * H + h][r]], axis=0), st[h].astype(BF16)) for h in hs]
        v_nb = [(us[b * H + h][r] - ws[h][0:C]).astype(BF16) for h in hs]
        for h in hs:
            outs[h].append(ws[h][C:2 * C] + _dot(attn[b * H + h][r, r], v_nb[h]))
        cd = [jnp.exp(glast_r[b][h:h + 1, r0:r0 + 1]) for h in hs]
        st = [st[h] * cd[h] + _dot_tn(ke[b * H + h][r], v_nb[h]) for h in hs]
    for h in hs:
        state_ref[h] = st[h]
    o = jnp.concatenate([jnp.concatenate(outs[h], axis=0) for h in hs], axis=1)
    r_o = lax.rsqrt(head_sums(o * o) * (1.0 / DN_V_DIM) + RMS_EPS)
    o = o * expand(r_o)[0] * ng_ref[...]
    out_ref[...] = (o * _silu(z_ref[...])).astype(out_ref.dtype)


def _gdn(qkv, z, misc, misct, conv_w, a_log, dt_bias, norm_g, *, B, S):
    T = B * S
    sb = min(GDN_SOLVE, S)
    tb = min(GDN_SOLVES_PER_STEP * sb, S)
    ns = S // tb
    H = DN_HEADS
    kern = functools.partial(_gdn_kernel, tb=tb, sb=sb)
    consts = _gdn_constants(sb)
    const = lambda *s: pl.BlockSpec(s, lambda b, i: (0,) * len(s))
    return pl.pallas_call(
        kern,
        grid=(B, ns),
        in_specs=[
            pl.BlockSpec((tb, DN_QKV_W), lambda b, i: (b * ns + i, 0)),
            pl.BlockSpec((tb, DN_VAL_W), lambda b, i: (b * ns + i, 0)),
            pl.BlockSpec((tb, MISC_W), lambda b, i: (b * ns + i, 0)),
            pl.BlockSpec((H, tb), lambda b, i: (MISC_A // H, b * ns + i)),
            const(DN_CONV, DN_QKV_W), const(H, 1), const(H, 1), const(1, H), const(1, H), const(1, DN_VAL_W),
        ] + [const(*c.shape) for c in consts],
        out_specs=pl.BlockSpec((tb, DN_VAL_W), lambda b, i: (b * ns + i, 0)),
        out_shape=jax.ShapeDtypeStruct((T, DN_VAL_W), BF16),
        scratch_shapes=[
            pltpu.VMEM((tb + SUBLANES, DN_QKV_W), F32),
            pltpu.VMEM((H, DN_K_DIM, DN_V_DIM), F32),
        ],
        compiler_params=pltpu.CompilerParams(
            dimension_semantics=("parallel", "arbitrary"), vmem_limit_bytes=VMEM_LIMIT),
        name="gdn",
    )(qkv, z, misc, misct, conv_w, a_log.reshape(H, 1), dt_bias.reshape(H, 1),
      a_log.reshape(1, H), dt_bias.reshape(1, H), jnp.tile(norm_g.reshape(1, DN_V_DIM), (1, H)), *consts)


def _out_proj_kernel(a_ref, d_ref, x_ref, wa_ref, wd_ref, g_ref, b_ref, h_ref, *, alpha):
    for r in range(0, a_ref.shape[0], ROW_GROUP):
        rows = slice(r, r + ROW_GROUP)
        mix = _dot(a_ref[rows, :].astype(BF16), wa_ref[...]) + _dot(d_ref[rows, :], wd_ref[...])
        h_ref[rows, :] = _layer_norm_rows(alpha * x_ref[rows, :] + mix, g_ref[...], b_ref[...])


def _out_proj(a, d, x2, wa, wd, g, b, *, alpha, tm):
    T = x2.shape[0]
    const = lambda *s: pl.BlockSpec(s, lambda i: (0,) * len(s))
    return pl.pallas_call(
        functools.partial(_out_proj_kernel, alpha=alpha),
        grid=(T // tm,),
        in_specs=[
            pl.BlockSpec((tm, A_OUT_W), lambda i: (i, 0)),
            pl.BlockSpec((tm, DN_VAL_W), lambda i: (i, 0)),
            pl.BlockSpec((tm, D_MODEL), lambda i: (i, 0)),
            const(A_OUT_W, D_MODEL), const(DN_VAL_W, D_MODEL), const(1, D_MODEL), const(1, D_MODEL),
        ],
        out_specs=pl.BlockSpec((tm, D_MODEL), lambda i: (i, 0)),
        out_shape=jax.ShapeDtypeStruct((T, D_MODEL), F32),
        compiler_params=pltpu.CompilerParams(dimension_semantics=("parallel",), vmem_limit_bytes=VMEM_LIMIT),
        name="out_proj",
    )(a, d, x2, wa, wd, g, b)


def _ffn_kernel(h_ref, halo_ref, wg_ref, wu_ref, cw_ref, cb_ref, wd_ref, g_ref, b_ref, out_ref,
                acc_ref, *, alpha, tiles_per_seq):
    i = pl.program_id(0)
    f = pl.program_id(1)
    nf = pl.num_programs(1)
    hb = h_ref[...].astype(BF16)
    wg = wg_ref[...]
    gate = _dot(hb, wg)
    up = _dot(hb, wu_ref[...])
    halo = _dot(halo_ref[...].astype(BF16), wg)
    halo = jnp.where(i % tiles_per_seq == 0, 0.0, halo)
    ge = jnp.concatenate([halo, gate], axis=0)
    conv = gate * cw_ref[FFN_CONV - 1:FFN_CONV, :]
    for d in range(1, FFN_CONV):
        conv = conv + pltpu.roll(ge, d, 0)[SUBLANES:] * cw_ref[FFN_CONV - 1 - d:FFN_CONV - d, :]
    act = (_silu(conv + cb_ref[...]) * up).astype(BF16)
    part = _dot(act, wd_ref[...])

    @pl.when(f == 0)
    def _():
        acc_ref[...] = part

    @pl.when(f > 0)
    def _():
        acc_ref[...] += part

    @pl.when(f == nf - 1)
    def _():
        out_ref[...] = _layer_norm_rows(alpha * h_ref[...] + acc_ref[...], g_ref[...], b_ref[...])


def _ffn(h, wg, wu, cw, cb, wd, g, b, *, alpha, S, tm, tf):
    T = h.shape[0]
    nf = D_FF // tf
    hs = tm // SUBLANES
    kern = functools.partial(_ffn_kernel, alpha=alpha, tiles_per_seq=S // tm)
    return pl.pallas_call(
        kern,
        grid=(T // tm, nf),
        in_specs=[
            pl.BlockSpec((tm, D_MODEL), lambda i, f: (i, 0)),
            pl.BlockSpec((SUBLANES, D_MODEL), lambda i, f: (jnp.maximum(i * hs - 1, 0), 0)),
            pl.BlockSpec((D_MODEL, tf), lambda i, f: (0, f)),
            pl.BlockSpec((D_MODEL, tf), lambda i, f: (0, f)),
            pl.BlockSpec((FFN_CONV, tf), lambda i, f: (0, f)),
            pl.BlockSpec((1, tf), lambda i, f: (0, f)),
            pl.BlockSpec((tf, D_MODEL), lambda i, f: (f, 0)),
            pl.BlockSpec((1, D_MODEL), lambda i, f: (0, 0)),
            pl.BlockSpec((1, D_MODEL), lambda i, f: (0, 0)),
        ],
        out_specs=pl.BlockSpec((tm, D_MODEL), lambda i, f: (i, 0)),
        out_shape=jax.ShapeDtypeStruct((T, D_MODEL), F32),
        scratch_shapes=[pltpu.VMEM((tm, D_MODEL), F32)],
        compiler_params=pltpu.CompilerParams(
            dimension_semantics=("parallel", "arbitrary"), vmem_limit_bytes=VMEM_LIMIT),
        name="ffn",
    )(h, h, wg, wu, cw, cb, wd, g, b)


def _regroup_w_in(w):
    offs = [0]
    for s in IN_SIZES:
        offs.append(offs[-1] + s)
    cq, ckv, kidx, widx, qkv, z, b, a = (w[:, offs[i]:offs[i + 1]] for i in range(8))
    pad = jnp.zeros((w.shape[0], MISC_W - (IDX_DIM + IDX_HEADS + 2 * DN_HEADS)), w.dtype)
    return jnp.concatenate([cq, ckv, kidx, widx, b, a, pad, qkv, z], axis=1).astype(BF16)


def _layer(x2, p, *, B, S, alpha):
    row = lambda v: v.reshape(1, -1)
    wc = _regroup_w_in(p["w_in"])
    wuk_bd_t = jnp.einsum("hdc,hg->hcgd", p["w_uk"], jnp.eye(A_HEADS, dtype=F32)).reshape(
        A_HEADS * A_KV_RANK, A_HEADS * A_QK_DIM)
    qlat, qidx, ckv, ckvt, kidx, misc, misct, qkv, z = _in_proj(
        x2, wc, row(p["q_norm_g"]), p["w_uq"].T.astype(BF16), wuk_bd_t.astype(BF16), p["w_qidx"].T.astype(BF16),
        row(p["kv_norm_g"]),
        row(p["kidx_ln_g"]), row(p["kidx_ln_b"]), tm=min(512, S))
    a_out = _dsa(qidx, qlat, misct, kidx, ckv, ckvt, p["w_uv"].transpose(0, 2, 1).astype(BF16),
                 row(p["attn_out_g"]), B=B, S=S)
    d_out = _gdn(qkv, z, misc, misct, p["dn_conv_w"], p["dn_a_log"], p["dn_dt_bias"], p["dn_norm_g"],
                 B=B, S=S)
    w_out = p["w_out"].astype(BF16)
    h = _out_proj(a_out, d_out, x2, w_out[:A_OUT_W], w_out[A_OUT_W:], row(p["ln1_g"]), row(p["ln1_b"]),
                  alpha=alpha, tm=min(512, S))
    w_ffn = p["ffn_w_in"].astype(BF16)
    return _ffn(h, w_ffn[:, :D_FF], w_ffn[:, D_FF:], p["ffn_conv_w"], row(p["ffn_conv_b"]),
                p["ffn_w_down"].astype(BF16), row(p["ln2_g"]), row(p["ln2_b"]),
                alpha=alpha, S=S, tm=min(512, S), tf=D_FF // 2)


_PARAM_NAMES = ("w_in", "q_norm_g", "w_uq", "w_qidx", "kv_norm_g", "w_uk", "w_uv", "kidx_ln_g", "kidx_ln_b",
                "attn_out_g", "dn_conv_w", "dn_a_log", "dn_dt_bias", "dn_norm_g", "w_out", "ln1_g", "ln1_b",
                "ffn_w_in", "ffn_conv_w", "ffn_conv_b", "ffn_w_down", "ln2_g", "ln2_b")


def kernel(x, w_in, q_norm_g, w_uq, w_qidx, kv_norm_g, w_uk, w_uv, kidx_ln_g, kidx_ln_b, attn_out_g, dn_conv_w, dn_a_log, dn_dt_bias, dn_norm_g, w_out, ln1_g, ln1_b, ffn_w_in, ffn_conv_w, ffn_conv_b, ffn_w_down, ln2_g, ln2_b):
    params = (w_in, q_norm_g, w_uq, w_qidx, kv_norm_g, w_uk, w_uv, kidx_ln_g, kidx_ln_b, attn_out_g, dn_conv_w,
              dn_a_log, dn_dt_bias, dn_norm_g, w_out, ln1_g, ln1_b, ffn_w_in, ffn_conv_w, ffn_conv_b, ffn_w_down,
              ln2_g, ln2_b)
    B, S, D = x.shape
    depth = w_in.shape[0]
    alpha = (2 * depth) ** 0.25
    x2 = x.reshape(B * S, D)
    for l in range(depth):
        x2 = _layer(x2, {n: v[l] for n, v in zip(_PARAM_NAMES, params)}, B=B, S=S, alpha=alpha)
    return x2.reshape(B, S, D)
```

```python
import functools

import jax
import jax.numpy as jnp
import numpy as np
from jax import lax
from jax.experimental import pallas as pl
from jax.experimental.pallas import tpu as pltpu

F32 = jnp.float32
BF16 = jnp.bfloat16
I32 = jnp.int32
I16 = jnp.int16

D_MODEL = 1024
A_HEADS = 8
A_QK_DIM = 64
A_V_DIM = 64
A_Q_RANK = 256
A_KV_RANK = 128
IDX_HEADS = 8
IDX_DIM = 64
TOPK_MAX = 256
DN_HEADS = 8
DN_K_DIM = 64
DN_V_DIM = 64
DN_CONV = 4
D_FF = 2816
FFN_CONV = 3
RMS_EPS = 1e-6
LN_EPS = 1e-5

A_OUT_W = A_HEADS * A_V_DIM
DN_KEY_W = DN_HEADS * DN_K_DIM
DN_VAL_W = DN_HEADS * DN_V_DIM
DN_QKV_W = 2 * DN_KEY_W + DN_VAL_W
IN_SIZES = (A_Q_RANK, A_KV_RANK, IDX_DIM, IDX_HEADS, DN_QKV_W, DN_VAL_W, DN_HEADS, DN_HEADS)

MISC_W = 128
MISC_WIDX = IDX_DIM
MISC_B = IDX_DIM + IDX_HEADS
MISC_A = MISC_B + DN_HEADS
PROJ_W = A_Q_RANK + A_KV_RANK + MISC_W + DN_QKV_W + DN_VAL_W

LOG2E = 1.4426950408889634
SUBLANES = 8
PACK16 = 16
KVT_ROWS = A_KV_RANK + PACK16
HALF16 = 2 ** 15
INT_MIN = -(2 ** 31)
NEG_KEY = INT_MIN + 0x00800000
VMEM_LIMIT = 56 * 1024 * 1024

TIE_SCAN_MAX = 8
ROW_GROUP = 256
GDN_CHUNK = 64
GDN_SOLVE = 256
GDN_SOLVES_PER_STEP = 1


def _dot(a, b):
    return jnp.dot(a, b, preferred_element_type=F32)


def _dot_nt(a, b):
    return lax.dot_general(a, b, (((1,), (1,)), ((), ())), preferred_element_type=F32)


def _dot_tn(a, b):
    return lax.dot_general(a, b, (((0,), (0,)), ((), ())), preferred_element_type=F32)


def _sigmoid(x):
    return 1.0 / (1.0 + jnp.exp(-x))


def _silu(x):
    return x * _sigmoid(x)


def _layer_norm_rows(v, g, b):
    mu = jnp.mean(v, axis=-1, keepdims=True)
    d = v - mu
    var = jnp.mean(d * d, axis=-1, keepdims=True)
    return d * lax.rsqrt(var + LN_EPS) * g + b


def _rms_norm_rows(v, g):
    return v * lax.rsqrt(jnp.mean(v * v, axis=-1, keepdims=True) + RMS_EPS) * g


def _in_proj_kernel(x_ref, w_ref, qg_ref, wuq_ref, wuk_ref, wqi_ref, kvg_ref, lng_ref, lnb_ref,
                    qlat_ref, qidx_ref, ckv_ref, ckvt_ref, kidx_ref, misc_ref, misct_ref, qkv_ref, z_ref):
    xb = x_ref[...].astype(BF16)
    proj = _dot(xb, w_ref[...])
    o = 0
    c_q = proj[:, o:o + A_Q_RANK]; o += A_Q_RANK
    c_kv = proj[:, o:o + A_KV_RANK]; o += A_KV_RANK
    misc = proj[:, o:o + MISC_W]; o += MISC_W
    qkv_ref[...] = proj[:, o:o + DN_QKV_W]; o += DN_QKV_W
    z_ref[...] = proj[:, o:o + DN_VAL_W]

    cqn_t = _rms_norm_rows(c_q, qg_ref[...]).T.astype(BF16)
    q_t = _dot(wuq_ref[...], cqn_t).astype(BF16)
    qlat_t = _dot(wuk_ref[...], q_t) * (A_QK_DIM ** -0.5 * LOG2E)
    qidx_t = _dot(wqi_ref[...], cqn_t)
    for h in range(A_HEADS):
        qlat_ref[h] = qlat_t[h * A_KV_RANK:(h + 1) * A_KV_RANK, :].astype(BF16)
        qidx_ref[h] = qidx_t[h * IDX_DIM:(h + 1) * IDX_DIM, :].astype(BF16)
    ckv = _rms_norm_rows(c_kv, kvg_ref[...])
    ckv_ref[...] = ckv.astype(BF16)
    ckvt_ref[0:A_KV_RANK, :] = ckv.T.astype(BF16)
    ckvt_ref[A_KV_RANK:KVT_ROWS, :] = jnp.ones((KVT_ROWS - A_KV_RANK, ckv.shape[0]), BF16)
    kidx_ref[...] = _layer_norm_rows(misc[:, :IDX_DIM], lng_ref[...], lnb_ref[...]).astype(BF16)
    misc_ref[...] = misc
    misct_ref[...] = misc.T


def _in_proj(x2, wc, qg, wuq, wuk, wqi, kvg, lng, lnb, *, tm):
    T = x2.shape[0]
    const = lambda *s: pl.BlockSpec(s, lambda i: (0,) * len(s))
    return pl.pallas_call(
        _in_proj_kernel,
        grid=(T // tm,),
        in_specs=[
            pl.BlockSpec((tm, D_MODEL), lambda i: (i, 0)),
            const(D_MODEL, PROJ_W), const(1, A_Q_RANK), const(A_HEADS * A_QK_DIM, A_Q_RANK),
            const(A_HEADS * A_KV_RANK, A_HEADS * A_QK_DIM), const(IDX_HEADS * IDX_DIM, A_Q_RANK),
            const(1, A_KV_RANK), const(1, IDX_DIM), const(1, IDX_DIM),
        ],
        out_specs=[
            pl.BlockSpec((A_HEADS, A_KV_RANK, tm), lambda i: (0, 0, i)),
            pl.BlockSpec((IDX_HEADS, IDX_DIM, tm), lambda i: (0, 0, i)),
            pl.BlockSpec((tm, A_KV_RANK), lambda i: (i, 0)),
            pl.BlockSpec((KVT_ROWS, tm), lambda i: (0, i)),
            pl.BlockSpec((tm, IDX_DIM), lambda i: (i, 0)),
            pl.BlockSpec((tm, MISC_W), lambda i: (i, 0)),
            pl.BlockSpec((MISC_W, tm), lambda i: (0, i)),
            pl.BlockSpec((tm, DN_QKV_W), lambda i: (i, 0)),
            pl.BlockSpec((tm, DN_VAL_W), lambda i: (i, 0)),
        ],
        out_shape=[
            jax.ShapeDtypeStruct((A_HEADS, A_KV_RANK, T), BF16),
            jax.ShapeDtypeStruct((IDX_HEADS, IDX_DIM, T), BF16),
            jax.ShapeDtypeStruct((T, A_KV_RANK), BF16),
            jax.ShapeDtypeStruct((KVT_ROWS, T), BF16),
            jax.ShapeDtypeStruct((T, IDX_DIM), BF16),
            jax.ShapeDtypeStruct((T, MISC_W), F32),
            jax.ShapeDtypeStruct((MISC_W, T), F32),
            jax.ShapeDtypeStruct((T, DN_QKV_W), F32),
            jax.ShapeDtypeStruct((T, DN_VAL_W), F32),
        ],
        compiler_params=pltpu.CompilerParams(dimension_semantics=("parallel",), vmem_limit_bytes=VMEM_LIMIT),
        name="in_proj",
    )(x2, wc, qg, wuq, wuk, wqi, kvg, lng, lnb)


def _dsa_kernel(qidx_ref, qlat_ref, wt_ref, kidx_ref, ckv_ref, ckvt_ref, wuvt_ref, g_ref, out_ref,
                keys_ref, hi_ref, lo_ref, acc_ref, stage_ref, bias_ref, *, qb, kb, rb, slab, topk, idx_bits):
    H = A_HEADS
    j = pl.program_id(1)
    q0 = j * qb
    nkeys = (j + 1) * qb
    nrblk = (nkeys + rb - 1) // rb
    ups = rb // kb
    nchunk = nrblk * ups
    w = wt_ref[...] * (IDX_HEADS ** -0.5 * IDX_DIM ** -0.5)
    row_minus_lane = (lax.broadcasted_iota(I32, (slab, qb), 0) - lax.broadcasted_iota(I32, (slab, qb), 1))

    def skewed(nparts, consume, produce, carry, on_tail=None):
        def chunk(c, k, carry):
            if ups > 1:
                for i in range(nparts):
                    carry = produce(c + 1, (k + 1) % ups, i, consume(c, k, i, carry))
                return carry
            for i in range(nparts):
                carry = consume(c, k, i, carry)
            for i in range(nparts):
                carry = produce(c + 1, k, i, carry)
            return carry

        def step(i, carry):
            for k in range(ups):
                carry = chunk(i * ups + k, k, carry)
            return carry

        for i in range(nparts):
            carry = produce(0, 0, i, carry)
        carry = lax.fori_loop(0, nrblk - 1, step, carry)
        for k in range(ups - 1):
            carry = chunk((nrblk - 1) * ups + k, k, carry)
        last = nchunk - 1

        @pl.when(last * kb < nkeys)
        def _():
            for i in range(nparts):
                consume(last, ups - 1, i, carry)

        if on_tail is not None:
            @pl.when(last * kb >= nkeys)
            def _():
                on_tail(last)
        return carry

    nslab = kb // slab
    hps = H // nslab

    def score_matmuls(c, slot, i, carry):
        kc = kidx_ref[pl.ds(pl.multiple_of(c * kb, kb), kb), :]
        for h in range(i * hps, (i + 1) * hps):
            stage_ref[slot, h] = _dot(kc, qidx_ref[h])
        return carry

    def score_keys(c, slot, i, carry):
        r0 = pl.multiple_of(c * kb, kb) + i * slab
        s = jnp.zeros((slab, qb), F32)
        for h in range(H):
            s = s + jnp.maximum(stage_ref[slot, h, i * slab:(i + 1) * slab, :], 0.0) * w[h:h + 1, :]
        bits = lax.bitcast_convert_type(s, I32)
        key = jnp.where(bits < 0, INT_MIN - bits, bits)
        key = jnp.where(row_minus_lane <= q0 - r0, key, NEG_KEY)
        keys_ref[pl.ds(r0, slab), :] = key
        hi_ref[pl.ds(r0, slab), :] = jnp.right_shift(key, 16).astype(I16)
        lo_ref[pl.ds(r0, slab), :] = ((key & 0xFFFF) - HALF16).astype(I16)
        return carry

    def score_tail(c):
        r0 = pl.multiple_of(c * kb, kb)
        keys_ref[pl.ds(r0, kb), :] = jnp.full((kb, qb), NEG_KEY, I32)
        hi_ref[pl.ds(r0, kb), :] = jnp.full((kb, qb), NEG_KEY >> 16, I16)
        lo_ref[pl.ds(r0, kb), :] = jnp.full((kb, qb), (NEG_KEY & 0xFFFF) - HALF16, I16)

    skewed(nslab, score_keys, score_matmuls, 0, score_tail)

    nacc = 4

    def count16(ref, cand, below=False):
        def body(r, accs):
            r0 = pl.multiple_of(r * rb, rb)
            k = ref[pl.ds(r0, rb), :]
            m = jnp.where(k < cand if below else k >= cand, jnp.int16(1), jnp.int16(0))
            accs = list(accs)
            for i in range(rb // PACK16):
                accs[i % nacc] = accs[i % nacc] + m[i * PACK16:(i + 1) * PACK16, :]
            return tuple(accs)
        accs = lax.fori_loop(0, nrblk, body, tuple(jnp.zeros((PACK16, qb), I16) for _ in range(nacc)))
        tot = accs[0]
        for a in accs[1:]:
            tot = tot + a
        return tot.astype(I32).sum(axis=0, keepdims=True)

    def max16_below(ref, bound):
        def body(r, accs):
            r0 = pl.multiple_of(r * rb, rb)
            k = ref[pl.ds(r0, rb), :]
            m = jnp.where(k < bound, k, jnp.int16(-1))
            accs = list(accs)
            for i in range(rb // PACK16):
                tile = m[i * PACK16:(i + 1) * PACK16, :]
                accs[i % nacc] = jnp.where(tile > accs[i % nacc], tile, accs[i % nacc])
            return tuple(accs)
        accs = lax.fori_loop(0, nrblk, body, tuple(jnp.full((PACK16, qb), -1, I16) for _ in range(nacc)))
        top = accs[0].astype(I32)
        for a in accs[1:]:
            top = jnp.maximum(top, a.astype(I32))
        return top.max(axis=0, keepdims=True)

    def bisect16(ref, target, cnt_all):
        def body(p, carry):
            t_u, cnt_ok, cnt_rej = carry
            cand_u = t_u | jnp.left_shift(jnp.int32(1), 15 - p)
            cnt = count16(ref, (cand_u - HALF16).astype(I16))
            ok = cnt >= target
            return jnp.where(ok, cand_u, t_u), jnp.where(ok, cnt, cnt_ok), jnp.where(ok, cnt_rej, cnt)
        return lax.fori_loop(0, 16, body, (jnp.zeros((1, qb), I32), cnt_all, jnp.zeros((1, qb), I32)))

    hi_u, cnt_ge_hi, cnt_gt_hi = bisect16(hi_ref, topk, jnp.full((1, qb), nrblk * rb, I32))
    hi_t = (hi_u - HALF16).astype(I16)

    def mask_lo(r, carry):
        r0 = pl.multiple_of(r * rb, rb)
        lo_ref[pl.ds(r0, rb), :] = jnp.where(hi_ref[pl.ds(r0, rb), :] == hi_t, lo_ref[pl.ds(r0, rb), :],
                                             jnp.int16(-HALF16))
        return carry

    lax.fori_loop(0, nrblk, mask_lo, 0)
    lo_u, cnt_ge_lo, cnt_gt_lo = bisect16(lo_ref, topk - cnt_gt_hi, cnt_ge_hi - cnt_gt_hi)
    t = jnp.left_shift(hi_u - HALF16, 16) + lo_u
    cnt_t = cnt_gt_hi + cnt_ge_lo

    excess = jnp.where(t > NEG_KEY, cnt_t - topk, 0)

    worst = jnp.max(excess)

    @pl.when(worst > 0)
    def _():
        keep = topk - (cnt_gt_hi + cnt_gt_lo)
        lo_t = (lo_u - HALF16).astype(I16)

        def tie_rows(r, carry):
            r0 = pl.multiple_of(r * rb, rb)
            tie = (hi_ref[pl.ds(r0, rb), :] == hi_t) & (lo_ref[pl.ds(r0, rb), :] == lo_t)
            row = (r0 + lax.broadcasted_iota(I32, (rb, qb), 0)).astype(I16)
            lo_ref[pl.ds(r0, rb), :] = jnp.where(tie, row, jnp.int16(HALF16 - 1))
            return carry

        lax.fori_loop(0, nrblk, tie_rows, 0)

        def demote_from(first):
            def demote(r, carry):
                r0 = pl.multiple_of(r * rb, rb)
                k = keys_ref[pl.ds(r0, rb), :]
                row = r0 + lax.broadcasted_iota(I32, (rb, qb), 0)
                keys_ref[pl.ds(r0, rb), :] = jnp.where((k == t) & (row >= first), NEG_KEY, k)
                return carry
            lax.fori_loop(0, nrblk, demote, 0)

        @pl.when(worst <= TIE_SCAN_MAX)
        def _():
            def peel(c):
                first, left = c
                top = max16_below(lo_ref, first.astype(I16))
                return jnp.where(left > 0, top, first), jnp.maximum(left - 1, 0)
            first, _ = lax.while_loop(lambda c: jnp.max(c[1]) > 0, peel,
                                      (jnp.full((1, qb), HALF16 - 1, I32), jnp.maximum(excess, 0)))
            demote_from(first)

        @pl.when(worst > TIE_SCAN_MAX)
        def _():
            def idx_body(p, pos):
                cand = pos + jnp.left_shift(jnp.int32(1), idx_bits - 1 - p)
                return jnp.where(count16(lo_ref, cand.astype(I16), below=True) < keep, cand, pos)
            demote_from(lax.fori_loop(0, idx_bits, idx_body, jnp.zeros((1, qb), I32)) + 1)

    thr = jnp.maximum(t, NEG_KEY + 1)

    acc_ref[...] = jnp.zeros(acc_ref.shape, F32)

    def put(carry, h, item):
        return carry[:h] + (item,) + carry[h + 1:]

    def attn_logits(c, slot, h, carry):
        r0 = pl.multiple_of(c * kb, kb)
        if h == 0:
            bias_ref[slot] = jnp.where(keys_ref[pl.ds(r0, kb), :] >= thr, 0.0, -jnp.inf)
        m_prev = carry[h][1]
        lg = _dot(ckv_ref[pl.ds(r0, kb), :], qlat_ref[h]) + bias_ref[slot]
        stage_ref[slot, h] = lg
        return put(carry, h, (m_prev, jnp.maximum(m_prev, jnp.max(lg, axis=0, keepdims=True))))

    def attn_values(c, slot, h, carry):
        m_old, m_new = carry[h]
        kvt = ckvt_ref[:, pl.ds(pl.multiple_of(c * kb, kb), kb)]
        p = jnp.exp2(stage_ref[slot, h] - m_new).astype(BF16)
        acc_ref[h] = acc_ref[h] * jnp.exp2(m_old - m_new) + _dot(kvt, p)
        return carry

    m_init = jnp.full((1, qb), -1e30, F32)
    skewed(H, attn_values, attn_logits, ((m_init, m_init),) * H)

    out_t = jnp.concatenate(
        [_dot(wuvt_ref[h], (acc_ref[h, 0:A_KV_RANK, :] / acc_ref[h, A_KV_RANK:A_KV_RANK + 1, :]).astype(BF16))
         for h in range(H)], axis=0)
    out_ref[...] = _rms_norm_rows(out_t.T, g_ref[...])


def _dsa(qidx, qlat, misct, kidx, ckv, ckvt, wuvt, g, *, B, S):
    T = B * S
    qb = min(256, S)
    nq = S // qb
    kb = qb
    rb = 2 * qb if nq % 2 == 0 else qb
    topk = min(TOPK_MAX, S // 4)
    idx_bits = max(1, (S - 1).bit_length())
    assert S < HALF16, "tie-breaking keeps key positions in int16"
    slab = min(64, kb)
    kern = functools.partial(_dsa_kernel, qb=qb, kb=kb, rb=rb, slab=slab, topk=topk, idx_bits=idx_bits)
    return pl.pallas_call(
        kern,
        grid=(B, nq),
        in_specs=[
            pl.BlockSpec((IDX_HEADS, IDX_DIM, qb), lambda b, j: (0, 0, b * nq + j)),
            pl.BlockSpec((A_HEADS, A_KV_RANK, qb), lambda b, j: (0, 0, b * nq + j)),
            pl.BlockSpec((IDX_HEADS, qb), lambda b, j: (MISC_WIDX // IDX_HEADS, b * nq + j)),
            pl.BlockSpec((S, IDX_DIM), lambda b, j: (b, 0)),
            pl.BlockSpec((S, A_KV_RANK), lambda b, j: (b, 0)),
            pl.BlockSpec((KVT_ROWS, S), lambda b, j: (0, b)),
            pl.BlockSpec((A_HEADS, A_V_DIM, A_KV_RANK), lambda b, j: (0, 0, 0)),
            pl.BlockSpec((1, A_OUT_W), lambda b, j: (0, 0)),
        ],
        out_specs=pl.BlockSpec((qb, A_OUT_W), lambda b, j: (b * nq + j, 0)),
        out_shape=jax.ShapeDtypeStruct((T, A_OUT_W), F32),
        scratch_shapes=[
            pltpu.VMEM((S, qb), I32),
            pltpu.VMEM((S, qb), I16),
            pltpu.VMEM((S, qb), I16),
            pltpu.VMEM((A_HEADS, KVT_ROWS, qb), F32),
            pltpu.VMEM((rb // kb, A_HEADS, kb, qb), F32),
            pltpu.VMEM((rb // kb, kb, qb), F32),
        ],
        compiler_params=pltpu.CompilerParams(
            dimension_semantics=("parallel", "arbitrary"), vmem_limit_bytes=VMEM_LIMIT),
        name="dsa",
    )(qidx, qlat, misct, kidx, ckv, ckvt, wuvt, g)


def _gdn_constants(sb):
    r, c = np.arange(sb)[:, None], np.arange(sb)[None, :]
    same = (r // GDN_CHUNK) == (c // GDN_CHUNK)
    tri = same & (c <= r)
    diag = [((r // b) == (c // b)).astype(np.float32) for b in (8, 16, 32)] + [same.astype(np.float32)]
    eye = (r == c).astype(np.float32)
    mats = [diag[0]] + [diag[i] - diag[i - 1] for i in (1, 2, 3)] + [eye, 1.0 - eye, np.where(tri, 0.0, -np.inf)]
    mask_c = np.concatenate([tri, same], axis=0).astype(np.float32)
    mask_r = np.concatenate([same & (r <= c), same], axis=1).astype(np.float32)
    spread = (np.arange(DN_KEY_W)[None, :] // DN_K_DIM == np.arange(DN_HEADS)[:, None]).astype(np.float32)
    spread3 = np.zeros((3, 3, 3, DN_HEADS, DN_KEY_W), np.float32)
    for j in range(3):
        spread3[j, :, j] = spread
    return (jnp.asarray(np.stack(mats), F32), jnp.asarray(mask_c, BF16), jnp.asarray(mask_r, BF16),
            jnp.asarray(spread3.reshape(3, 9 * DN_HEADS, DN_KEY_W), BF16), jnp.asarray(np.tile(spread, (1, 2)), BF16))


def _gdn_kernel(qkv_ref, z_ref, misc_ref, at_ref, cw_ref, alog_c_ref, dtb_c_ref, alog_r_ref, dtb_r_ref,
                ng_ref, fm_ref, mc_ref, mr_ref, sr_ref, sc_ref, out_ref, xbuf_ref, state_ref, *, tb, sb):
    C = GDN_CHUNK
    nck = tb // C
    H = DN_HEADS
    step = pl.program_id(1)

    @pl.when(step == 0)
    def _():
        xbuf_ref[0:SUBLANES, :] = jnp.zeros((SUBLANES, DN_QKV_W), F32)
        state_ref[...] = jnp.zeros(state_ref.shape, F32)

    xbuf_ref[SUBLANES:SUBLANES + tb, :] = qkv_ref[...]
    xe = xbuf_ref[...]
    y = xe[SUBLANES:] * cw_ref[DN_CONV - 1:DN_CONV, :]
    for d in range(1, DN_CONV):
        y = y + pltpu.roll(xe, d, 0)[SUBLANES:] * cw_ref[DN_CONV - 1 - d:DN_CONV - d, :]
    xbuf_ref[0:SUBLANES, :] = xe[tb:tb + SUBLANES]
    y = _silu(y)

    def softplus(v):
        return jnp.maximum(v, 0.0) + jnp.log(1.0 + jnp.exp(-jnp.abs(v)))

    misc = misc_ref[...]
    beta_c = _sigmoid(misc[:, MISC_B:MISC_B + H])
    g_c = -jnp.exp(alog_r_ref[...]) * softplus(misc[:, MISC_A:MISC_A + H] + dtb_r_ref[...])
    g_r = -jnp.exp(alog_c_ref[...]) * softplus(at_ref[...] + dtb_c_ref[...])

    rows = [slice(b * sb, (b + 1) * sb) for b in range(tb // sb)]
    mask_c = mc_ref[...]
    mask_r = mr_ref[...]

    def pieces(v, axis, n=3):
        out, rest = [], v
        for _ in range(n - 1):
            out.append(rest.astype(BF16))
            rest = rest - out[-1].astype(F32)
        return jnp.concatenate(out + [rest.astype(BF16)], axis=axis)

    sums_c = [_dot(mask_c, pieces(g_c[r], 1)) for r in rows]
    sums_c = [s[:, 0:H] + s[:, H:2 * H] + s[:, 2 * H:3 * H] for s in sums_c]
    sums_r = [_dot(pieces(g_r[:, r], 0), mask_r) for r in rows]
    sums_r = [s[0:H] + s[H:2 * H] + s[2 * H:3 * H] for s in sums_r]
    gcum_c = [s[0:sb] for s in sums_c]
    glast_c = [s[sb:2 * sb] for s in sums_c]
    gcum_r = [s[:, 0:sb] for s in sums_r]
    glast_r = [s[:, sb:2 * sb] for s in sums_r]
    blk8, ring_f, eye, off_diag, tri_bias = fm_ref[0], [fm_ref[1], fm_ref[2], fm_ref[3]], fm_ref[4], fm_ref[5], fm_ref[6]

    hs = range(H)
    bh = [(b, h) for b in range(len(rows)) for h in hs]
    ch = range(len(bh))
    def expand(*vs):
        p = pieces(jnp.concatenate(list(vs) + [vs[-1]] * (3 - len(vs)), axis=1), 1)
        return [_dot(p, sr_ref[j]) for j in range(len(vs))]

    head_sums = lambda v: _dot_nt(pieces(v, 1, 2), sc_ref[...])
    yq, yk, yv = y[:, :DN_KEY_W], y[:, DN_KEY_W:2 * DN_KEY_W], y[:, 2 * DN_KEY_W:]
    gc_all = jnp.concatenate(gcum_c, axis=0)
    eg_c = jnp.exp(gc_all)
    rq_x, rk_x, beta_x = expand(lax.rsqrt(head_sums(yq * yq) + RMS_EPS) * (DN_K_DIM ** -0.5),
                                lax.rsqrt(head_sums(yk * yk) + RMS_EPS), beta_c)
    eg_x, egl_x = expand(eg_c, jnp.exp(jnp.concatenate(glast_c, axis=0) - gc_all))
    q_n, k_n = yq * rq_x, yk * rk_x
    kb_all = k_n * beta_x
    k_bf, kb_bf, q_bf = k_n.astype(BF16), kb_all.astype(BF16), q_n.astype(BF16)
    rhs_v, rhs_k = (yv * beta_x).astype(BF16), (kb_all * eg_x).astype(BF16)
    qd_all = (q_n * eg_x).astype(BF16)
    ke_all = (k_n * egl_x).astype(BF16)
    head = lambda a, i: a[rows[bh[i][0]], bh[i][1] * DN_K_DIM:(bh[i][1] + 1) * DN_K_DIM]
    gcs = [gcum_c[b][:, h:h + 1] for b, h in bh]
    kk = [_dot_nt(head(kb_bf, i), head(k_bf, i)) for i in ch]
    qk = [_dot_nt(head(q_bf, i), head(k_bf, i)) for i in ch]
    decay = [jnp.exp((gcs[i] - gcum_r[b][h:h + 1, :]) + tri_bias) for i, (b, h) in enumerate(bh)]
    lower = [kk[i] * (decay[i] * off_diag) for i in ch]
    attn = [(qk[i] * decay[i]).astype(BF16) for i in ch]
    pw = [lower[i] * blk8 for i in ch]
    x_inv = [eye - pw[i] for i in ch]
    for _ in range(2):
        pwb = [p.astype(BF16) for p in pw]
        pw = [_dot(p, p) for p in pwb]
        x_inv = [x_inv[i] + _dot(x_inv[i].astype(BF16), pw[i].astype(BF16)) for i in ch]
    for ring in ring_f:
        xb = [x.astype(BF16) for x in x_inv]
        t1 = [_dot(xb[i], (lower[i] * ring).astype(BF16)).astype(BF16) for i in ch]
        x_inv = [x_inv[i] - _dot(t1[i], xb[i]) for i in ch]
    uw = [_dot(x_inv[i].astype(BF16), jnp.concatenate([head(rhs_v, i), head(rhs_k, i)], axis=1)) for i in ch]
    us = [m[:, :DN_V_DIM] for m in uw]
    wkb = [m[:, DN_V_DIM:].astype(BF16) for m in uw]
    qd = [head(qd_all, i) for i in ch]
    ke = [head(ke_all, i) for i in ch]

    st = [state_ref[h] for h in hs]
    outs = [[] for _ in hs]
    for n in range(nck):
        b, r0 = divmod(n * C, sb)
        r = slice(r0, r0 + C)
        ws = [_dot(jnp.concatenate([wkb[b * H + h][r], qd[b * H + h][r]], axis=0), st[h].astype(BF16)) for h in hs]
        v_nb = [(us[b * H + h][r] - ws[h][0:C]).astype(BF16) for h in hs]
        for h in hs:
            outs[h].append(ws[h][C:2 * C] + _dot(attn[b * H + h][r, r], v_nb[h]))
        cd = [jnp.exp(glast_r[b][h:h + 1, r0:r0 + 1]) for h in hs]
        st = [st[h] * cd[h] + _dot_tn(ke[b * H + h][r], v_nb[h]) for h in hs]
    for h in hs:
        state_ref[h] = st[h]
    o = jnp.concatenate([jnp.concatenate(outs[h], axis=0) for h in hs], axis=1)
    r_o = lax.rsqrt(head_sums(o * o) * (1.0 / DN_V_DIM) + RMS_EPS)
    o = o * expand(r_o)[0] * ng_ref[...]
    out_ref[...] = (o * _silu(z_ref[...])).astype(out_ref.dtype)


def _gdn(qkv, z, misc, misct, conv_w, a_log, dt_bias, norm_g, *, B, S):
    T = B * S
    sb = min(GDN_SOLVE, S)
    tb = min(GDN_SOLVES_PER_STEP * sb, S)
    ns = S // tb
    H = DN_HEADS
    kern = functools.partial(_gdn_kernel, tb=tb, sb=sb)
    consts = _gdn_constants(sb)
    const = lambda *s: pl.BlockSpec(s, lambda b, i: (0,) * len(s))
    return pl.pallas_call(
        kern,
        grid=(B, ns),
        in_specs=[
            pl.BlockSpec((tb, DN_QKV_W), lambda b, i: (b * ns + i, 0)),
            pl.BlockSpec((tb, DN_VAL_W), lambda b, i: (b * ns + i, 0)),
            pl.BlockSpec((tb, MISC_W), lambda b, i: (b * ns + i, 0)),
            pl.BlockSpec((H, tb), lambda b, i: (MISC_A // H, b * ns + i)),
            const(DN_CONV, DN_QKV_W), const(H, 1), const(H, 1), const(1, H), const(1, H), const(1, DN_VAL_W),
        ] + [const(*c.shape) for c in consts],
        out_specs=pl.BlockSpec((tb, DN_VAL_W), lambda b, i: (b * ns + i, 0)),
        out_shape=jax.ShapeDtypeStruct((T, DN_VAL_W), BF16),
        scratch_shapes=[
            pltpu.VMEM((tb + SUBLANES, DN_QKV_W), F32),
            pltpu.VMEM((H, DN_K_DIM, DN_V_DIM), F32),
        ],
        compiler_params=pltpu.CompilerParams(
            dimension_semantics=("parallel", "arbitrary"), vmem_limit_bytes=VMEM_LIMIT),
        name="gdn",
    )(qkv, z, misc, misct, conv_w, a_log.reshape(H, 1), dt_bias.reshape(H, 1),
      a_log.reshape(1, H), dt_bias.reshape(1, H), jnp.tile(norm_g.reshape(1, DN_V_DIM), (1, H)), *consts)


def _out_proj_kernel(a_ref, d_ref, x_ref, wa_ref, wd_ref, g_ref, b_ref, h_ref, *, alpha):
    for r in range(0, a_ref.shape[0], ROW_GROUP):
        rows = slice(r, r + ROW_GROUP)
        mix = _dot(a_ref[rows, :].astype(BF16), wa_ref[...]) + _dot(d_ref[rows, :], wd_ref[...])
        h_ref[rows, :] = _layer_norm_rows(alpha * x_ref[rows, :] + mix, g_ref[...], b_ref[...])


def _out_proj(a, d, x2, wa, wd, g, b, *, alpha, tm):
    T = x2.shape[0]
    const = lambda *s: pl.BlockSpec(s, lambda i: (0,) * len(s))
    return pl.pallas_call(
        functools.partial(_out_proj_kernel, alpha=alpha),
        grid=(T // tm,),
        in_specs=[
            pl.BlockSpec((tm, A_OUT_W), lambda i: (i, 0)),
            pl.BlockSpec((tm, DN_VAL_W), lambda i: (i, 0)),
            pl.BlockSpec((tm, D_MODEL), lambda i: (i, 0)),
            const(A_OUT_W, D_MODEL), const(DN_VAL_W, D_MODEL), const(1, D_MODEL), const(1, D_MODEL),
        ],
        out_specs=pl.BlockSpec((tm, D_MODEL), lambda i: (i, 0)),
        out_shape=jax.ShapeDtypeStruct((T, D_MODEL), F32),
        compiler_params=pltpu.CompilerParams(dimension_semantics=("parallel",), vmem_limit_bytes=VMEM_LIMIT),
        name="out_proj",
    )(a, d, x2, wa, wd, g, b)


def _ffn_kernel(h_ref, halo_ref, wg_ref, wu_ref, cw_ref, cb_ref, wd_ref, g_ref, b_ref, out_ref,
                acc_ref, *, alpha, tiles_per_seq):
    i = pl.program_id(0)
    f = pl.program_id(1)
    nf = pl.num_programs(1)
    hb = h_ref[...].astype(BF16)
    wg = wg_ref[...]
    gate = _dot(hb, wg)
    up = _dot(hb, wu_ref[...])
    halo = _dot(halo_ref[...].astype(BF16), wg)
    halo = jnp.where(i % tiles_per_seq == 0, 0.0, halo)
    ge = jnp.concatenate([halo, gate], axis=0)
    conv = gate * cw_ref[FFN_CONV - 1:FFN_CONV, :]
    for d in range(1, FFN_CONV):
        conv = conv + pltpu.roll(ge, d, 0)[SUBLANES:] * cw_ref[FFN_CONV - 1 - d:FFN_CONV - d, :]
    act = (_silu(conv + cb_ref[...]) * up).astype(BF16)
    part = _dot(act, wd_ref[...])

    @pl.when(f == 0)
    def _():
        acc_ref[...] = part

    @pl.when(f > 0)
    def _():
        acc_ref[...] += part

    @pl.when(f == nf - 1)
    def _():
        out_ref[...] = _layer_norm_rows(alpha * h_ref[...] + acc_ref[...], g_ref[...], b_ref[...])


def _ffn(h, wg, wu, cw, cb, wd, g, b, *, alpha, S, tm, tf):
    T = h.shape[0]
    nf = D_FF // tf
    hs = tm // SUBLANES
    kern = functools.partial(_ffn_kernel, alpha=alpha, tiles_per_seq=S // tm)
    return pl.pallas_call(
        kern,
        grid=(T // tm, nf),
        in_specs=[
            pl.BlockSpec((tm, D_MODEL), lambda i, f: (i, 0)),
            pl.BlockSpec((SUBLANES, D_MODEL), lambda i, f: (jnp.maximum(i * hs - 1, 0), 0)),
            pl.BlockSpec((D_MODEL, tf), lambda i, f: (0, f)),
            pl.BlockSpec((D_MODEL, tf), lambda i, f: (0, f)),
            pl.BlockSpec((FFN_CONV, tf), lambda i, f: (0, f)),
            pl.BlockSpec((1, tf), lambda i, f: (0, f)),
            pl.BlockSpec((tf, D_MODEL), lambda i, f: (f, 0)),
            pl.BlockSpec((1, D_MODEL), lambda i, f: (0, 0)),
            pl.BlockSpec((1, D_MODEL), lambda i, f: (0, 0)),
        ],
        out_specs=pl.BlockSpec((tm, D_MODEL), lambda i, f: (i, 0)),
        out_shape=jax.ShapeDtypeStruct((T, D_MODEL), F32),
        scratch_shapes=[pltpu.VMEM((tm, D_MODEL), F32)],
        compiler_params=pltpu.CompilerParams(
            dimension_semantics=("parallel", "arbitrary"), vmem_limit_bytes=VMEM_LIMIT),
        name="ffn",
    )(h, h, wg, wu, cw, cb, wd, g, b)


def _regroup_w_in(w):
    offs = [0]
    for s in IN_SIZES:
        offs.append(offs[-1] + s)
    cq, ckv, kidx, widx, qkv, z, b, a = (w[:, offs[i]:offs[i + 1]] for i in range(8))
    pad = jnp.zeros((w.shape[0], MISC_W - (IDX_DIM + IDX_HEADS + 2 * DN_HEADS)), w.dtype)
    return jnp.concatenate([cq, ckv, kidx, widx, b, a, pad, qkv, z], axis=1).astype(BF16)


def _layer(x2, p, *, B, S, alpha):
    row = lambda v: v.reshape(1, -1)
    wc = _regroup_w_in(p["w_in"])
    wuk_bd_t = jnp.einsum("hdc,hg->hcgd", p["w_uk"], jnp.eye(A_HEADS, dtype=F32)).reshape(
        A_HEADS * A_KV_RANK, A_HEADS * A_QK_DIM)
    qlat, qidx, ckv, ckvt, kidx, misc, misct, qkv, z = _in_proj(
        x2, wc, row(p["q_norm_g"]), p["w_uq"].T.astype(BF16), wuk_bd_t.astype(BF16), p["w_qidx"].T.astype(BF16),
        row(p["kv_norm_g"]),
        row(p["kidx_ln_g"]), row(p["kidx_ln_b"]), tm=min(512, S))
    a_out = _dsa(qidx, qlat, misct, kidx, ckv, ckvt, p["w_uv"].transpose(0, 2, 1).astype(BF16),
                 row(p["attn_out_g"]), B=B, S=S)
    d_out = _gdn(qkv, z, misc, misct, p["dn_conv_w"], p["dn_a_log"], p["dn_dt_bias"], p["dn_norm_g"],
                 B=B, S=S)
    w_out = p["w_out"].astype(BF16)
    h = _out_proj(a_out, d_out, x2, w_out[:A_OUT_W], w_out[A_OUT_W:], row(p["ln1_g"]), row(p["ln1_b"]),
                  alpha=alpha, tm=min(512, S))
    w_ffn = p["ffn_w_in"].astype(BF16)
    return _ffn(h, w_ffn[:, :D_FF], w_ffn[:, D_FF:], p["ffn_conv_w"], row(p["ffn_conv_b"]),
                p["ffn_w_down"].astype(BF16), row(p["ln2_g"]), row(p["ln2_b"]),
                alpha=alpha, S=S, tm=min(512, S), tf=D_FF // 2)


_PARAM_NAMES = ("w_in", "q_norm_g", "w_uq", "w_qidx", "kv_norm_g", "w_uk", "w_uv", "kidx_ln_g", "kidx_ln_b",
                "attn_out_g", "dn_conv_w", "dn_a_log", "dn_dt_bias", "dn_norm_g", "w_out", "ln1_g", "ln1_b",
                "ffn_w_in", "ffn_conv_w", "ffn_conv_b", "ffn_w_down", "ln2_g", "ln2_b")


def kernel(x, w_in, q_norm_g, w_uq, w_qidx, kv_norm_g, w_uk, w_uv, kidx_ln_g, kidx_ln_b, attn_out_g, dn_conv_w, dn_a_log, dn_dt_bias, dn_norm_g, w_out, ln1_g, ln1_b, ffn_w_in, ffn_conv_w, ffn_conv_b, ffn_w_down, ln2_g, ln2_b):
    params = (w_in, q_norm_g, w_uq, w_qidx, kv_norm_g, w_uk, w_uv, kidx_ln_g, kidx_ln_b, attn_out_g, dn_conv_w,
              dn_a_log, dn_dt_bias, dn_norm_g, w_out, ln1_g, ln1_b, ffn_w_in, ffn_conv_w, ffn_conv_b, ffn_w_down,
              ln2_g, ln2_b)
    B, S, D = x.shape
    depth = w_in.shape[0]
    alpha = (2 * depth) ** 0.25
    x2 = x.reshape(B * S, D)
    for l in range(depth):
        x2 = _layer(x2, {n: v[l] for n, v in zip(_PARAM_NAMES, params)}, B=B, S=S, alpha=alpha)
    return x2.reshape(B, S, D)
```

```python
import functools

import jax
import jax.numpy as jnp
import numpy as np
from jax import lax
from jax.experimental import pallas as pl
from jax.experimental.pallas import tpu as pltpu

F32 = jnp.float32
BF16 = jnp.bfloat16
I32 = jnp.int32
I16 = jnp.int16

D_MODEL = 1024
A_HEADS = 8
A_QK_DIM = 64
A_V_DIM = 64
A_Q_RANK = 256
A_KV_RANK = 128
IDX_HEADS = 8
IDX_DIM = 64
TOPK_MAX = 256
DN_HEADS = 8
DN_K_DIM = 64
DN_V_DIM = 64
DN_CONV = 4
D_FF = 2816
FFN_CONV = 3
RMS_EPS = 1e-6
LN_EPS = 1e-5

A_OUT_W = A_HEADS * A_V_DIM
DN_KEY_W = DN_HEADS * DN_K_DIM
DN_VAL_W = DN_HEADS * DN_V_DIM
DN_QKV_W = 2 * DN_KEY_W + DN_VAL_W
IN_SIZES = (A_Q_RANK, A_KV_RANK, IDX_DIM, IDX_HEADS, DN_QKV_W, DN_VAL_W, DN_HEADS, DN_HEADS)

MISC_W = 128
MISC_WIDX = IDX_DIM
MISC_B = IDX_DIM + IDX_HEADS
MISC_A = MISC_B + DN_HEADS
PROJ_W = A_Q_RANK + A_KV_RANK + MISC_W + DN_QKV_W + DN_VAL_W

LOG2E = 1.4426950408889634
SUBLANES = 8
PACK16 = 16
KVT_ROWS = A_KV_RANK + PACK16
HALF16 = 2 ** 15
INT_MIN = -(2 ** 31)
NEG_KEY = INT_MIN + 0x00800000
VMEM_LIMIT = 56 * 1024 * 1024

TIE_SCAN_MAX = 8
ROW_GROUP = 256
GDN_CHUNK = 64
GDN_SOLVE = 256
GDN_SOLVES_PER_STEP = 1


def _dot(a, b):
    return jnp.dot(a, b, preferred_element_type=F32)


def _dot_nt(a, b):
    return lax.dot_general(a, b, (((1,), (1,)), ((), ())), preferred_element_type=F32)


def _dot_tn(a, b):
    return lax.dot_general(a, b, (((0,), (0,)), ((), ())), preferred_element_type=F32)


def _sigmoid(x):
    return 1.0 / (1.0 + jnp.exp(-x))


def _silu(x):
    return x * _sigmoid(x)


def _layer_norm_rows(v, g, b):
    mu = jnp.mean(v, axis=-1, keepdims=True)
    d = v - mu
    var = jnp.mean(d * d, axis=-1, keepdims=True)
    return d * lax.rsqrt(var + LN_EPS) * g + b


def _rms_norm_rows(v, g):
    return v * lax.rsqrt(jnp.mean(v * v, axis=-1, keepdims=True) + RMS_EPS) * g


def _in_proj_kernel(x_ref, w_ref, qg_ref, wuq_ref, wuk_ref, wqi_ref, kvg_ref, lng_ref, lnb_ref,
                    qlat_ref, qidx_ref, ckv_ref, ckvt_ref, kidx_ref, misc_ref, misct_ref, qkv_ref, z_ref):
    xb = x_ref[...].astype(BF16)
    proj = _dot(xb, w_ref[...])
    o = 0
    c_q = proj[:, o:o + A_Q_RANK]; o += A_Q_RANK
    c_kv = proj[:, o:o + A_KV_RANK]; o += A_KV_RANK
    misc = proj[:, o:o + MISC_W]; o += MISC_W
    qkv_ref[...] = proj[:, o:o + DN_QKV_W]; o += DN_QKV_W
    z_ref[...] = proj[:, o:o + DN_VAL_W]

    cqn_t = _rms_norm_rows(c_q, qg_ref[...]).T.astype(BF16)
    q_t = _dot(wuq_ref[...], cqn_t).astype(BF16)
    qlat_t = _dot(wuk_ref[...], q_t) * (A_QK_DIM ** -0.5 * LOG2E)
    qidx_t = _dot(wqi_ref[...], cqn_t)
    for h in range(A_HEADS):
        qlat_ref[h] = qlat_t[h * A_KV_RANK:(h + 1) * A_KV_RANK, :].astype(BF16)
        qidx_ref[h] = qidx_t[h * IDX_DIM:(h + 1) * IDX_DIM, :].astype(BF16)
    ckv = _rms_norm_rows(c_kv, kvg_ref[...])
    ckv_ref[...] = ckv.astype(BF16)
    ckvt_ref[0:A_KV_RANK, :] = ckv.T.astype(BF16)
    ckvt_ref[A_KV_RANK:KVT_ROWS, :] = jnp.ones((KVT_ROWS - A_KV_RANK, ckv.shape[0]), BF16)
    kidx_ref[...] = _layer_norm_rows(misc[:, :IDX_DIM], lng_ref[...], lnb_ref[...]).astype(BF16)
    misc_ref[...] = misc
    misct_ref[...] = misc.T


def _in_proj(x2, wc, qg, wuq, wuk, wqi, kvg, lng, lnb, *, tm):
    T = x2.shape[0]
    const = lambda *s: pl.BlockSpec(s, lambda i: (0,) * len(s))
    return pl.pallas_call(
        _in_proj_kernel,
        grid=(T // tm,),
        in_specs=[
            pl.BlockSpec((tm, D_MODEL), lambda i: (i, 0)),
            const(D_MODEL, PROJ_W), const(1, A_Q_RANK), const(A_HEADS * A_QK_DIM, A_Q_RANK),
            const(A_HEADS * A_KV_RANK, A_HEADS * A_QK_DIM), const(IDX_HEADS * IDX_DIM, A_Q_RANK),
            const(1, A_KV_RANK), const(1, IDX_DIM), const(1, IDX_DIM),
        ],
        out_specs=[
            pl.BlockSpec((A_HEADS, A_KV_RANK, tm), lambda i: (0, 0, i)),
            pl.BlockSpec((IDX_HEADS, IDX_DIM, tm), lambda i: (0, 0, i)),
            pl.BlockSpec((tm, A_KV_RANK), lambda i: (i, 0)),
            pl.BlockSpec((KVT_ROWS, tm), lambda i: (0, i)),
            pl.BlockSpec((tm, IDX_DIM), lambda i: (i, 0)),
            pl.BlockSpec((tm, MISC_W), lambda i: (i, 0)),
            pl.BlockSpec((MISC_W, tm), lambda i: (0, i)),
            pl.BlockSpec((tm, DN_QKV_W), lambda i: (i, 0)),
            pl.BlockSpec((tm, DN_VAL_W), lambda i: (i, 0)),
        ],
        out_shape=[
            jax.ShapeDtypeStruct((A_HEADS, A_KV_RANK, T), BF16),
            jax.ShapeDtypeStruct((IDX_HEADS, IDX_DIM, T), BF16),
            jax.ShapeDtypeStruct((T, A_KV_RANK), BF16),
            jax.ShapeDtypeStruct((KVT_ROWS, T), BF16),
            jax.ShapeDtypeStruct((T, IDX_DIM), BF16),
            jax.ShapeDtypeStruct((T, MISC_W), F32),
            jax.ShapeDtypeStruct((MISC_W, T), F32),
            jax.ShapeDtypeStruct((T, DN_QKV_W), F32),
            jax.ShapeDtypeStruct((T, DN_VAL_W), F32),
        ],
        compiler_params=pltpu.CompilerParams(dimension_semantics=("parallel",), vmem_limit_bytes=VMEM_LIMIT),
        name="in_proj",
    )(x2, wc, qg, wuq, wuk, wqi, kvg, lng, lnb)


def _dsa_kernel(qidx_ref, qlat_ref, wt_ref, kidx_ref, ckv_ref, ckvt_ref, wuvt_ref, g_ref, out_ref,
                keys_ref, hi_ref, lo_ref, acc_ref, stage_ref, bias_ref, *, qb, kb, rb, slab, topk, idx_bits):
    H = A_HEADS
    j = pl.program_id(1)
    q0 = j * qb
    nkeys = (j + 1) * qb
    nrblk = (nkeys + rb - 1) // rb
    ups = rb // kb
    nchunk = nrblk * ups
    w = wt_ref[...] * (IDX_HEADS ** -0.5 * IDX_DIM ** -0.5)
    row_minus_lane = (lax.broadcasted_iota(I32, (slab, qb), 0) - lax.broadcasted_iota(I32, (slab, qb), 1))

    def skewed(nparts, consume, produce, carry, on_tail=None):
        def chunk(c, k, carry):
            if ups > 1:
                for i in range(nparts):
                    carry = produce(c + 1, (k + 1) % ups, i, consume(c, k, i, carry))
                return carry
            for i in range(nparts):
                carry = consume(c, k, i, carry)
            for i in range(nparts):
                carry = produce(c + 1, k, i, carry)
            return carry

        def step(i, carry):
            for k in range(ups):
                carry = chunk(i * ups + k, k, carry)
            return carry

        for i in range(nparts):
            carry = produce(0, 0, i, carry)
        carry = lax.fori_loop(0, nrblk - 1, step, carry)
        for k in range(ups - 1):
            carry = chunk((nrblk - 1) * ups + k, k, carry)
        last = nchunk - 1

        @pl.when(last * kb < nkeys)
        def _():
            for i in range(nparts):
                consume(last, ups - 1, i, carry)

        if on_tail is not None:
            @pl.when(last * kb >= nkeys)
            def _():
                on_tail(last)
        return carry

    nslab = kb // slab
    hps = H // nslab

    def score_matmuls(c, slot, i, carry):
        kc = kidx_ref[pl.ds(pl.multiple_of(c * kb, kb), kb), :]
        for h in range(i * hps, (i + 1) * hps):
            stage_ref[slot, h] = _dot(kc, qidx_ref[h])
        return carry

    def score_keys(c, slot, i, carry):
        r0 = pl.multiple_of(c * kb, kb) + i * slab
        s = jnp.zeros((slab, qb), F32)
        for h in range(H):
            s = s + jnp.maximum(stage_ref[slot, h, i * slab:(i + 1) * slab, :], 0.0) * w[h:h + 1, :]
        bits = lax.bitcast_convert_type(s, I32)
        key = jnp.where(bits < 0, INT_MIN - bits, bits)
        key = jnp.where(row_minus_lane <= q0 - r0, key, NEG_KEY)
        keys_ref[pl.ds(r0, slab), :] = key
        hi_ref[pl.ds(r0, slab), :] = jnp.right_shift(key, 16).astype(I16)
        lo_ref[pl.ds(r0, slab), :] = ((key & 0xFFFF) - HALF16).astype(I16)
        return carry

    def score_tail(c):
        r0 = pl.multiple_of(c * kb, kb)
        keys_ref[pl.ds(r0, kb), :] = jnp.full((kb, qb), NEG_KEY, I32)
        hi_ref[pl.ds(r0, kb), :] = jnp.full((kb, qb), NEG_KEY >> 16, I16)
        lo_ref[pl.ds(r0, kb), :] = jnp.full((kb, qb), (NEG_KEY & 0xFFFF) - HALF16, I16)

    skewed(nslab, score_keys, score_matmuls, 0, score_tail)

    nacc = 4

    def count16(ref, cand, below=False):
        def body(r, accs):
            r0 = pl.multiple_of(r * rb, rb)
            k = ref[pl.ds(r0, rb), :]
            m = jnp.where(k < cand if below else k >= cand, jnp.int16(1), jnp.int16(0))
            accs = list(accs)
            for i in range(rb // PACK16):
                accs[i % nacc] = accs[i % nacc] + m[i * PACK16:(i + 1) * PACK16, :]
            return tuple(accs)
        accs = lax.fori_loop(0, nrblk, body, tuple(jnp.zeros((PACK16, qb), I16) for _ in range(nacc)))
        tot = accs[0].astype(I32)
        for a in accs[1:]:
            tot = tot + a.astype(I32)
        return tot.sum(axis=0, keepdims=True)

    def max16_below(ref, bound):
        def body(r, accs):
            r0 = pl.multiple_of(r * rb, rb)
            k = ref[pl.ds(r0, rb), :]
            m = jnp.where(k < bound, k, jnp.int16(-1))
            accs = list(accs)
            for i in range(rb // PACK16):
                tile = m[i * PACK16:(i + 1) * PACK16, :]
                accs[i % nacc] = jnp.where(tile > accs[i % nacc], tile, accs[i % nacc])
            return tuple(accs)
        accs = lax.fori_loop(0, nrblk, body, tuple(jnp.full((PACK16, qb), -1, I16) for _ in range(nacc)))
        top = accs[0].astype(I32)
        for a in accs[1:]:
            top = jnp.maximum(top, a.astype(I32))
        return top.max(axis=0, keepdims=True)

    def bisect16(ref, target, cnt_all):
        def body(p, carry):
            t_u, cnt_ok, cnt_rej = carry
            cand_u = t_u | jnp.left_shift(jnp.int32(1), 15 - p)
            cnt = count16(ref, (cand_u - HALF16).astype(I16))
            ok = cnt >= target
            return jnp.where(ok, cand_u, t_u), jnp.where(ok, cnt, cnt_ok), jnp.where(ok, cnt_rej, cnt)
        return lax.fori_loop(0, 16, body, (jnp.zeros((1, qb), I32), cnt_all, jnp.zeros((1, qb), I32)))

    hi_u, cnt_ge_hi, cnt_gt_hi = bisect16(hi_ref, topk, jnp.full((1, qb), nrblk * rb, I32))
    hi_t = (hi_u - HALF16).astype(I16)

    def mask_lo(r, carry):
        r0 = pl.multiple_of(r * rb, rb)
        lo_ref[pl.ds(r0, rb), :] = jnp.where(hi_ref[pl.ds(r0, rb), :] == hi_t, lo_ref[pl.ds(r0, rb), :],
                                             jnp.int16(-HALF16))
        return carry

    lax.fori_loop(0, nrblk, mask_lo, 0)
    lo_u, cnt_ge_lo, cnt_gt_lo = bisect16(lo_ref, topk - cnt_gt_hi, cnt_ge_hi - cnt_gt_hi)
    t = jnp.left_shift(hi_u - HALF16, 16) + lo_u
    cnt_t = cnt_gt_hi + cnt_ge_lo

    excess = jnp.where(t > NEG_KEY, cnt_t - topk, 0)

    worst = jnp.max(excess)

    @pl.when(worst > 0)
    def _():
        keep = topk - (cnt_gt_hi + cnt_gt_lo)
        lo_t = (lo_u - HALF16).astype(I16)

        def tie_rows(r, carry):
            r0 = pl.multiple_of(r * rb, rb)
            tie = (hi_ref[pl.ds(r0, rb), :] == hi_t) & (lo_ref[pl.ds(r0, rb), :] == lo_t)
            row = (r0 + lax.broadcasted_iota(I32, (rb, qb), 0)).astype(I16)
            lo_ref[pl.ds(r0, rb), :] = jnp.where(tie, row, jnp.int16(HALF16 - 1))
            return carry

        lax.fori_loop(0, nrblk, tie_rows, 0)

        def demote_from(first):
            def demote(r, carry):
                r0 = pl.multiple_of(r * rb, rb)
                k = keys_ref[pl.ds(r0, rb), :]
                row = r0 + lax.broadcasted_iota(I32, (rb, qb), 0)
                keys_ref[pl.ds(r0, rb), :] = jnp.where((k == t) & (row >= first), NEG_KEY, k)
                return carry
            lax.fori_loop(0, nrblk, demote, 0)

        @pl.when(worst <= TIE_SCAN_MAX)
        def _():
            def peel(c):
                first, left = c
                top = max16_below(lo_ref, first.astype(I16))
                return jnp.where(left > 0, top, first), jnp.maximum(left - 1, 0)
            first, _ = lax.while_loop(lambda c: jnp.max(c[1]) > 0, peel,
                                      (jnp.full((1, qb), HALF16 - 1, I32), jnp.maximum(excess, 0)))
            demote_from(first)

        @pl.when(worst > TIE_SCAN_MAX)
        def _():
            def idx_body(p, pos):
                cand = pos + jnp.left_shift(jnp.int32(1), idx_bits - 1 - p)
                return jnp.where(count16(lo_ref, cand.astype(I16), below=True) < keep, cand, pos)
            demote_from(lax.fori_loop(0, idx_bits, idx_body, jnp.zeros((1, qb), I32)) + 1)

    thr = jnp.maximum(t, NEG_KEY + 1)

    acc_ref[...] = jnp.zeros(acc_ref.shape, F32)

    def put(carry, h, item):
        return carry[:h] + (item,) + carry[h + 1:]

    def attn_logits(c, slot, h, carry):
        r0 = pl.multiple_of(c * kb, kb)
        if h == 0:
            bias_ref[slot] = jnp.where(keys_ref[pl.ds(r0, kb), :] >= thr, 0.0, -jnp.inf)
        m_prev = carry[h][1]
        lg = _dot(ckv_ref[pl.ds(r0, kb), :], qlat_ref[h]) + bias_ref[slot]
        stage_ref[slot, h] = lg
        return put(carry, h, (m_prev, jnp.maximum(m_prev, jnp.max(lg, axis=0, keepdims=True))))

    def attn_values(c, slot, h, carry):
        m_old, m_new = carry[h]
        kvt = ckvt_ref[:, pl.ds(pl.multiple_of(c * kb, kb), kb)]
        p = jnp.exp2(stage_ref[slot, h] - m_new).astype(BF16)
        acc_ref[h] = acc_ref[h] * jnp.exp2(m_old - m_new) + _dot(kvt, p)
        return carry

    m_init = jnp.full((1, qb), -1e30, F32)
    skewed(H, attn_values, attn_logits, ((m_init, m_init),) * H)

    out_t = jnp.concatenate(
        [_dot(wuvt_ref[h], (acc_ref[h, 0:A_KV_RANK, :] / acc_ref[h, A_KV_RANK:A_KV_RANK + 1, :]).astype(BF16))
         for h in range(H)], axis=0)
    out_ref[...] = _rms_norm_rows(out_t.T, g_ref[...])


def _dsa(qidx, qlat, misct, kidx, ckv, ckvt, wuvt, g, *, B, S):
    T = B * S
    qb = min(256, S)
    nq = S // qb
    kb = qb
    rb = 2 * qb if nq % 2 == 0 else qb
    topk = min(TOPK_MAX, S // 4)
    idx_bits = max(1, (S - 1).bit_length())
    assert S < HALF16, "tie-breaking keeps key positions in int16"
    slab = min(64, kb)
    kern = functools.partial(_dsa_kernel, qb=qb, kb=kb, rb=rb, slab=slab, topk=topk, idx_bits=idx_bits)
    return pl.pallas_call(
        kern,
        grid=(B, nq),
        in_specs=[
            pl.BlockSpec((IDX_HEADS, IDX_DIM, qb), lambda b, j: (0, 0, b * nq + j)),
            pl.BlockSpec((A_HEADS, A_KV_RANK, qb), lambda b, j: (0, 0, b * nq + j)),
            pl.BlockSpec((IDX_HEADS, qb), lambda b, j: (MISC_WIDX // IDX_HEADS, b * nq + j)),
            pl.BlockSpec((S, IDX_DIM), lambda b, j: (b, 0)),
            pl.BlockSpec((S, A_KV_RANK), lambda b, j: (b, 0)),
            pl.BlockSpec((KVT_ROWS, S), lambda b, j: (0, b)),
            pl.BlockSpec((A_HEADS, A_V_DIM, A_KV_RANK), lambda b, j: (0, 0, 0)),
            pl.BlockSpec((1, A_OUT_W), lambda b, j: (0, 0)),
        ],
        out_specs=pl.BlockSpec((qb, A_OUT_W), lambda b, j: (b * nq + j, 0)),
        out_shape=jax.ShapeDtypeStruct((T, A_OUT_W), F32),
        scratch_shapes=[
            pltpu.VMEM((S, qb), I32),
            pltpu.VMEM((S, qb), I16),
            pltpu.VMEM((S, qb), I16),
            pltpu.VMEM((A_HEADS, KVT_ROWS, qb), F32),
            pltpu.VMEM((rb // kb, A_HEADS, kb, qb), F32),
            pltpu.VMEM((rb // kb, kb, qb), F32),
        ],
        compiler_params=pltpu.CompilerParams(
            dimension_semantics=("parallel", "arbitrary"), vmem_limit_bytes=VMEM_LIMIT),
        name="dsa",
    )(qidx, qlat, misct, kidx, ckv, ckvt, wuvt, g)


def _gdn_constants(sb):
    r, c = np.arange(sb)[:, None], np.arange(sb)[None, :]
    same = (r // GDN_CHUNK) == (c // GDN_CHUNK)
    tri = same & (c <= r)
    diag = [((r // b) == (c // b)).astype(np.float32) for b in (8, 16, 32)] + [same.astype(np.float32)]
    eye = (r == c).astype(np.float32)
    mats = [diag[0]] + [diag[i] - diag[i - 1] for i in (1, 2, 3)] + [eye, 1.0 - eye, np.where(tri, 0.0, -np.inf)]
    mask_c = np.concatenate([tri, same], axis=0).astype(np.float32)
    mask_r = np.concatenate([same & (r <= c), same], axis=1).astype(np.float32)
    spread = (np.arange(DN_KEY_W)[None, :] // DN_K_DIM == np.arange(DN_HEADS)[:, None]).astype(np.float32)
    spread3 = np.zeros((3, 3, 3, DN_HEADS, DN_KEY_W), np.float32)
    for j in range(3):
        spread3[j, :, j] = spread
    return (jnp.asarray(np.stack(mats), F32), jnp.asarray(mask_c, BF16), jnp.asarray(mask_r, BF16),
            jnp.asarray(spread3.reshape(3, 9 * DN_HEADS, DN_KEY_W), BF16), jnp.asarray(np.tile(spread, (1, 2)), BF16))


def _gdn_kernel(qkv_ref, z_ref, misc_ref, at_ref, cw_ref, alog_c_ref, dtb_c_ref, alog_r_ref, dtb_r_ref,
                ng_ref, fm_ref, mc_ref, mr_ref, sr_ref, sc_ref, out_ref, xbuf_ref, state_ref, *, tb, sb):
    C = GDN_CHUNK
    nck = tb // C
    H = DN_HEADS
    step = pl.program_id(1)

    @pl.when(step == 0)
    def _():
        xbuf_ref[0:SUBLANES, :] = jnp.zeros((SUBLANES, DN_QKV_W), F32)
        state_ref[...] = jnp.zeros(state_ref.shape, F32)

    xbuf_ref[SUBLANES:SUBLANES + tb, :] = qkv_ref[...]
    xe = xbuf_ref[...]
    y = xe[SUBLANES:] * cw_ref[DN_CONV - 1:DN_CONV, :]
    for d in range(1, DN_CONV):
        y = y + pltpu.roll(xe, d, 0)[SUBLANES:] * cw_ref[DN_CONV - 1 - d:DN_CONV - d, :]
    xbuf_ref[0:SUBLANES, :] = xe[tb:tb + SUBLANES]
    y = _silu(y)

    def softplus(v):
        return jnp.maximum(v, 0.0) + jnp.log(1.0 + jnp.exp(-jnp.abs(v)))

    misc = misc_ref[...]
    beta_c = _sigmoid(misc[:, MISC_B:MISC_B + H])
    g_c = -jnp.exp(alog_r_ref[...]) * softplus(misc[:, MISC_A:MISC_A + H] + dtb_r_ref[...])
    g_r = -jnp.exp(alog_c_ref[...]) * softplus(at_ref[...] + dtb_c_ref[...])

    rows = [slice(b * sb, (b + 1) * sb) for b in range(tb // sb)]
    mask_c = mc_ref[...]
    mask_r = mr_ref[...]

    def pieces(v, axis, n=3):
        out, rest = [], v
        for _ in range(n - 1):
            out.append(rest.astype(BF16))
            rest = rest - out[-1].astype(F32)
        return jnp.concatenate(out + [rest.astype(BF16)], axis=axis)

    sums_c = [_dot(mask_c, pieces(g_c[r], 1)) for r in rows]
    sums_c = [s[:, 0:H] + s[:, H:2 * H] + s[:, 2 * H:3 * H] for s in sums_c]
    sums_r = [_dot(pieces(g_r[:, r], 0), mask_r) for r in rows]
    sums_r = [s[0:H] + s[H:2 * H] + s[2 * H:3 * H] for s in sums_r]
    gcum_c = [s[0:sb] for s in sums_c]
    glast_c = [s[sb:2 * sb] for s in sums_c]
    gcum_r = [s[:, 0:sb] for s in sums_r]
    glast_r = [s[:, sb:2 * sb] for s in sums_r]
    blk8, ring_f, eye, off_diag, tri_bias = fm_ref[0], [fm_ref[1], fm_ref[2], fm_ref[3]], fm_ref[4], fm_ref[5], fm_ref[6]

    hs = range(H)
    bh = [(b, h) for b in range(len(rows)) for h in hs]
    ch = range(len(bh))
    def expand(*vs):
        p = pieces(jnp.concatenate(list(vs) + [vs[-1]] * (3 - len(vs)), axis=1), 1)
        return [_dot(p, sr_ref[j]) for j in range(len(vs))]

    head_sums = lambda v: _dot_nt(pieces(v, 1, 2), sc_ref[...])
    yq, yk, yv = y[:, :DN_KEY_W], y[:, DN_KEY_W:2 * DN_KEY_W], y[:, 2 * DN_KEY_W:]
    gc_all = jnp.concatenate(gcum_c, axis=0)
    eg_c = jnp.exp(gc_all)
    rq_x, rk_x, beta_x = expand(lax.rsqrt(head_sums(yq * yq) + RMS_EPS) * (DN_K_DIM ** -0.5),
                                lax.rsqrt(head_sums(yk * yk) + RMS_EPS), beta_c)
    eg_x, egl_x = expand(eg_c, jnp.exp(jnp.concatenate(glast_c, axis=0) - gc_all))
    q_n, k_n = yq * rq_x, yk * rk_x
    kb_all = k_n * beta_x
    k_bf, kb_bf, q_bf = k_n.astype(BF16), kb_all.astype(BF16), q_n.astype(BF16)
    rhs_v, rhs_k = (yv * beta_x).astype(BF16), (kb_all * eg_x).astype(BF16)
    qd_all = (q_n * eg_x).astype(BF16)
    ke_all = (k_n * egl_x).astype(BF16)
    head = lambda a, i: a[rows[bh[i][0]], bh[i][1] * DN_K_DIM:(bh[i][1] + 1) * DN_K_DIM]
    gcs = [gcum_c[b][:, h:h + 1] for b, h in bh]
    kk = [_dot_nt(head(kb_bf, i), head(k_bf, i)) for i in ch]
    qk = [_dot_nt(head(q_bf, i), head(k_bf, i)) for i in ch]
    decay = [jnp.exp((gcs[i] - gcum_r[b][h:h + 1, :]) + tri_bias) for i, (b, h) in enumerate(bh)]
    lower = [kk[i] * (decay[i] * off_diag) for i in ch]
    attn = [(qk[i] * decay[i]).astype(BF16) for i in ch]
    pw = [lower[i] * blk8 for i in ch]
    x_inv = [eye - pw[i] for i in ch]
    for _ in range(2):
        pwb = [p.astype(BF16) for p in pw]
        pw = [_dot(p, p) for p in pwb]
        x_inv = [x_inv[i] + _dot(x_inv[i].astype(BF16), pw[i].astype(BF16)) for i in ch]
    for ring in ring_f:
        xb = [x.astype(BF16) for x in x_inv]
        t1 = [_dot(xb[i], (lower[i] * ring).astype(BF16)).astype(BF16) for i in ch]
        x_inv = [x_inv[i] - _dot(t1[i], xb[i]) for i in ch]
    uw = [_dot(x_inv[i].astype(BF16), jnp.concatenate([head(rhs_v, i), head(rhs_k, i)], axis=1)) for i in ch]
    us = [m[:, :DN_V_DIM] for m in uw]
    wkb = [m[:, DN_V_DIM:].astype(BF16) for m in uw]
    qd = [head(qd_all, i) for i in ch]
    ke = [head(ke_all, i) for i in ch]

    st = [state_ref[h] for h in hs]
    outs = [[] for _ in hs]
    for n in range(nck):
        b, r0 = divmod(n * C, sb)
        r = slice(r0, r0 + C)
        ws = [_dot(jnp.concatenate([wkb[b * H + h][r], qd[b * H + h][r]], axis=0), st[h].astype(BF16)) for h in hs]
        v_nb = [(us[b * H + h][r] - ws[h][0:C]).astype(BF16) for h in hs]
        for h in hs:
            outs[h].append(ws[h][C:2 * C] + _dot(attn[b * H + h][r, r], v_nb[h]))
        cd = [jnp.exp(glast_r[b][h:h + 1, r0:r0 + 1]) for h in hs]
        st = [st[h] * cd[h] + _dot_tn(ke[b * H + h][r], v_nb[h]) for h in hs]
    for h in hs:
        state_ref[h] = st[h]
    o = jnp.concatenate([jnp.concatenate(outs[h], axis=0) for h in hs], axis=1)
    r_o = lax.rsqrt(head_sums(o * o) * (1.0 / DN_V_DIM) + RMS_EPS)
    o = o * expand(r_o)[0] * ng_ref[...]
    out_ref[...] = (o * _silu(z_ref[...])).astype(out_ref.dtype)


def _gdn(qkv, z, misc, misct, conv_w, a_log, dt_bias, norm_g, *, B, S):
    T = B * S
    sb = min(GDN_SOLVE, S)
    tb = min(GDN_SOLVES_PER_STEP * sb, S)
    ns = S // tb
    H = DN_HEADS
    kern = functools.partial(_gdn_kernel, tb=tb, sb=sb)
    consts = _gdn_constants(sb)
    const = lambda *s: pl.BlockSpec(s, lambda b, i: (0,) * len(s))
    return pl.pallas_call(
        kern,
        grid=(B, ns),
        in_specs=[
            pl.BlockSpec((tb, DN_QKV_W), lambda b, i: (b * ns + i, 0)),
            pl.BlockSpec((tb, DN_VAL_W), lambda b, i: (b * ns + i, 0)),
            pl.BlockSpec((tb, MISC_W), lambda b, i: (b * ns + i, 0)),
            pl.BlockSpec((H, tb), lambda b, i: (MISC_A // H, b * ns + i)),
            const(DN_CONV, DN_QKV_W), const(H, 1), const(H, 1), const(1, H), const(1, H), const(1, DN_VAL_W),
        ] + [const(*c.shape) for c in consts],
        out_specs=pl.BlockSpec((tb, DN_VAL_W), lambda b, i: (b * ns + i, 0)),
        out_shape=jax.ShapeDtypeStruct((T, DN_VAL_W), BF16),
        scratch_shapes=[
            pltpu.VMEM((tb + SUBLANES, DN_QKV_W), F32),
            pltpu.VMEM((H, DN_K_DIM, DN_V_DIM), F32),
        ],
        compiler_params=pltpu.CompilerParams(
            dimension_semantics=("parallel", "arbitrary"), vmem_limit_bytes=VMEM_LIMIT),
        name="gdn",
    )(qkv, z, misc, misct, conv_w, a_log.reshape(H, 1), dt_bias.reshape(H, 1),
      a_log.reshape(1, H), dt_bias.reshape(1, H), jnp.tile(norm_g.reshape(1, DN_V_DIM), (1, H)), *consts)


def _out_proj_kernel(a_ref, d_ref, x_ref, wa_ref, wd_ref, g_ref, b_ref, h_ref, *, alpha):
    for r in range(0, a_ref.shape[0], ROW_GROUP):
        rows = slice(r, r + ROW_GROUP)
        mix = _dot(a_ref[rows, :].astype(BF16), wa_ref[...]) + _dot(d_ref[rows, :], wd_ref[...])
        h_ref[rows, :] = _layer_norm_rows(alpha * x_ref[rows, :] + mix, g_ref[...], b_ref[...])


def _out_proj(a, d, x2, wa, wd, g, b, *, alpha, tm):
    T = x2.shape[0]
    const = lambda *s: pl.BlockSpec(s, lambda i: (0,) * len(s))
    return pl.pallas_call(
        functools.partial(_out_proj_kernel, alpha=alpha),
        grid=(T // tm,),
        in_specs=[
            pl.BlockSpec((tm, A_OUT_W), lambda i: (i, 0)),
            pl.BlockSpec((tm, DN_VAL_W), lambda i: (i, 0)),
            pl.BlockSpec((tm, D_MODEL), lambda i: (i, 0)),
            const(A_OUT_W, D_MODEL), const(DN_VAL_W, D_MODEL), const(1, D_MODEL), const(1, D_MODEL),
        ],
        out_specs=pl.BlockSpec((tm, D_MODEL), lambda i: (i, 0)),
        out_shape=jax.ShapeDtypeStruct((T, D_MODEL), F32),
        compiler_params=pltpu.CompilerParams(dimension_semantics=("parallel",), vmem_limit_bytes=VMEM_LIMIT),
        name="out_proj",
    )(a, d, x2, wa, wd, g, b)


def _ffn_kernel(a_ref, d_ref, x_ref, ah_ref, dh_ref, xh_ref, wa_ref, wo_ref, g1_ref, b1_ref,
                wg_ref, wu_ref, cw_ref, cb_ref, wd_ref, g_ref, b_ref, out_ref,
                acc_ref, h_ref, *, alpha, tiles_per_seq):
    i = pl.program_id(0)
    f = pl.program_id(1)
    nf = pl.num_programs(1)
    tm = a_ref.shape[0]

    @pl.when(f == 0)
    def _():
        def h_rows(a, d, x):
            mix = _dot(a.astype(BF16), wa_ref[...]) + _dot(d, wo_ref[...])
            return _layer_norm_rows(alpha * x + mix, g1_ref[...], b1_ref[...])
        h_ref[0:PACK16, :] = h_rows(ah_ref[...], dh_ref[...], xh_ref[...])
        h_ref[PACK16:PACK16 + tm, :] = h_rows(a_ref[...], d_ref[...], x_ref[...])

    hb = h_ref[PACK16:PACK16 + tm, :].astype(BF16)
    wg = wg_ref[...]
    gate = _dot(hb, wg)
    up = _dot(hb, wu_ref[...])
    halo = _dot(h_ref[PACK16 - SUBLANES:PACK16, :].astype(BF16), wg)
    halo = jnp.where(i % tiles_per_seq == 0, 0.0, halo)
    ge = jnp.concatenate([halo, gate], axis=0)
    conv = gate * cw_ref[FFN_CONV - 1:FFN_CONV, :]
    for d in range(1, FFN_CONV):
        conv = conv + pltpu.roll(ge, d, 0)[SUBLANES:] * cw_ref[FFN_CONV - 1 - d:FFN_CONV - d, :]
    act = (_silu(conv + cb_ref[...]) * up).astype(BF16)
    part = _dot(act, wd_ref[...])

    @pl.when(f == 0)
    def _():
        acc_ref[...] = part

    @pl.when(f > 0)
    def _():
        acc_ref[...] += part

    @pl.when(f == nf - 1)
    def _():
        out_ref[...] = _layer_norm_rows(alpha * h_ref[PACK16:PACK16 + tm, :] + acc_ref[...], g_ref[...], b_ref[...])


def _ffn(a, d, x2, wa, wo, g1, b1, wg, wu, cw, cb, wd, g, b, *, alpha, S, tm, tf):
    T = x2.shape[0]
    nf = D_FF // tf
    hs = tm // PACK16
    kern = functools.partial(_ffn_kernel, alpha=alpha, tiles_per_seq=S // tm)
    tile = lambda w: pl.BlockSpec((tm, w), lambda i, f: (i, 0))
    halo = lambda w: pl.BlockSpec((PACK16, w), lambda i, f: (jnp.maximum(i * hs - 1, 0), 0))
    const = lambda *s: pl.BlockSpec(s, lambda i, f: (0,) * len(s))
    return pl.pallas_call(
        kern,
        grid=(T // tm, nf),
        in_specs=[
            tile(A_OUT_W), tile(DN_VAL_W), tile(D_MODEL), halo(A_OUT_W), halo(DN_VAL_W), halo(D_MODEL),
            const(A_OUT_W, D_MODEL), const(DN_VAL_W, D_MODEL), const(1, D_MODEL), const(1, D_MODEL),
            pl.BlockSpec((D_MODEL, tf), lambda i, f: (0, f)),
            pl.BlockSpec((D_MODEL, tf), lambda i, f: (0, f)),
            pl.BlockSpec((FFN_CONV, tf), lambda i, f: (0, f)),
            pl.BlockSpec((1, tf), lambda i, f: (0, f)),
            pl.BlockSpec((tf, D_MODEL), lambda i, f: (f, 0)),
            pl.BlockSpec((1, D_MODEL), lambda i, f: (0, 0)),
            pl.BlockSpec((1, D_MODEL), lambda i, f: (0, 0)),
        ],
        out_specs=pl.BlockSpec((tm, D_MODEL), lambda i, f: (i, 0)),
        out_shape=jax.ShapeDtypeStruct((T, D_MODEL), F32),
        scratch_shapes=[pltpu.VMEM((tm, D_MODEL), F32), pltpu.VMEM((PACK16 + tm, D_MODEL), F32)],
        compiler_params=pltpu.CompilerParams(
            dimension_semantics=("parallel", "arbitrary"), vmem_limit_bytes=VMEM_LIMIT),
        name="ffn",
    )(a, d, x2, a, d, x2, wa, wo, g1, b1, wg, wu, cw, cb, wd, g, b)


def _regroup_w_in(w):
    offs = [0]
    for s in IN_SIZES:
        offs.append(offs[-1] + s)
    cq, ckv, kidx, widx, qkv, z, b, a = (w[:, offs[i]:offs[i + 1]] for i in range(8))
    pad = jnp.zeros((w.shape[0], MISC_W - (IDX_DIM + IDX_HEADS + 2 * DN_HEADS)), w.dtype)
    return jnp.concatenate([cq, ckv, kidx, widx, b, a, pad, qkv, z], axis=1).astype(BF16)


def _layer(x2, p, *, B, S, alpha):
    row = lambda v: v.reshape(1, -1)
    wc = _regroup_w_in(p["w_in"])
    wuk_bd_t = jnp.einsum("hdc,hg->hcgd", p["w_uk"], jnp.eye(A_HEADS, dtype=F32)).reshape(
        A_HEADS * A_KV_RANK, A_HEADS * A_QK_DIM)
    qlat, qidx, ckv, ckvt, kidx, misc, misct, qkv, z = _in_proj(
        x2, wc, row(p["q_norm_g"]), p["w_uq"].T.astype(BF16), wuk_bd_t.astype(BF16), p["w_qidx"].T.astype(BF16),
        row(p["kv_norm_g"]),
        row(p["kidx_ln_g"]), row(p["kidx_ln_b"]), tm=min(512, S))
    a_out = _dsa(qidx, qlat, misct, kidx, ckv, ckvt, p["w_uv"].transpose(0, 2, 1).astype(BF16),
                 row(p["attn_out_g"]), B=B, S=S)
    d_out = _gdn(qkv, z, misc, misct, p["dn_conv_w"], p["dn_a_log"], p["dn_dt_bias"], p["dn_norm_g"],
                 B=B, S=S)
    w_out = p["w_out"].astype(BF16)
    w_ffn = p["ffn_w_in"].astype(BF16)
    return _ffn(a_out, d_out, x2, w_out[:A_OUT_W], w_out[A_OUT_W:], row(p["ln1_g"]), row(p["ln1_b"]),
                w_ffn[:, :D_FF], w_ffn[:, D_FF:], p["ffn_conv_w"], row(p["ffn_conv_b"]),
                p["ffn_w_down"].astype(BF16), row(p["ln2_g"]), row(p["ln2_b"]),
                alpha=alpha, S=S, tm=min(512, S), tf=D_FF // 2)


_PARAM_NAMES = ("w_in", "q_norm_g", "w_uq", "w_qidx", "kv_norm_g", "w_uk", "w_uv", "kidx_ln_g", "kidx_ln_b",
                "attn_out_g", "dn_conv_w", "dn_a_log", "dn_dt_bias", "dn_norm_g", "w_out", "ln1_g", "ln1_b",
                "ffn_w_in", "ffn_conv_w", "ffn_conv_b", "ffn_w_down", "ln2_g", "ln2_b")


def kernel(x, w_in, q_norm_g, w_uq, w_qidx, kv_norm_g, w_uk, w_uv, kidx_ln_g, kidx_ln_b, attn_out_g, dn_conv_w, dn_a_log, dn_dt_bias, dn_norm_g, w_out, ln1_g, ln1_b, ffn_w_in, ffn_conv_w, ffn_conv_b, ffn_w_down, ln2_g, ln2_b):
    params = (w_in, q_norm_g, w_uq, w_qidx, kv_norm_g, w_uk, w_uv, kidx_ln_g, kidx_ln_b, attn_out_g, dn_conv_w,
              dn_a_log, dn_dt_bias, dn_norm_g, w_out, ln1_g, ln1_b, ffn_w_in, ffn_conv_w, ffn_conv_b, ffn_w_down,
              ln2_g, ln2_b)
    B, S, D = x.shape
    depth = w_in.shape[0]
    alpha = (2 * depth) ** 0.25
    x2 = x.reshape(B * S, D)
    for l in range(depth):
        x2 = _layer(x2, {n: v[l] for n, v in zip(_PARAM_NAMES, params)}, B=B, S=S, alpha=alpha)
    return x2.reshape(B, S, D)
```

```python
import functools

import jax
import jax.numpy as jnp
import numpy as np
from jax import lax
from jax.experimental import pallas as pl
from jax.experimental.pallas import tpu as pltpu

F32 = jnp.float32
BF16 = jnp.bfloat16
I32 = jnp.int32
I16 = jnp.int16

D_MODEL = 1024
A_HEADS = 8
A_QK_DIM = 64
A_V_DIM = 64
A_Q_RANK = 256
A_KV_RANK = 128
IDX_HEADS = 8
IDX_DIM = 64
TOPK_MAX = 256
DN_HEADS = 8
DN_K_DIM = 64
DN_V_DIM = 64
DN_CONV = 4
D_FF = 2816
FFN_CONV = 3
RMS_EPS = 1e-6
LN_EPS = 1e-5

A_OUT_W = A_HEADS * A_V_DIM
DN_KEY_W = DN_HEADS * DN_K_DIM
DN_VAL_W = DN_HEADS * DN_V_DIM
DN_QKV_W = 2 * DN_KEY_W + DN_VAL_W
IN_SIZES = (A_Q_RANK, A_KV_RANK, IDX_DIM, IDX_HEADS, DN_QKV_W, DN_VAL_W, DN_HEADS, DN_HEADS)

MISC_W = 128
MISC_WIDX = IDX_DIM
MISC_B = IDX_DIM + IDX_HEADS
MISC_A = MISC_B + DN_HEADS
PROJ_W = A_Q_RANK + A_KV_RANK + MISC_W + DN_QKV_W + DN_VAL_W

LOG2E = 1.4426950408889634
SUBLANES = 8
PACK16 = 16
KVT_ROWS = A_KV_RANK + PACK16
HALF16 = 2 ** 15
INT_MIN = -(2 ** 31)
NEG_KEY = INT_MIN + 0x00800000
VMEM_LIMIT = 56 * 1024 * 1024

TIE_SCAN_MAX = 8
ROW_GROUP = 256
GDN_CHUNK = 64
GDN_SOLVE = 256
GDN_SOLVES_PER_STEP = 1


def _dot(a, b):
    return jnp.dot(a, b, preferred_element_type=F32)


def _dot_nt(a, b):
    return lax.dot_general(a, b, (((1,), (1,)), ((), ())), preferred_element_type=F32)


def _dot_tn(a, b):
    return lax.dot_general(a, b, (((0,), (0,)), ((), ())), preferred_element_type=F32)


def _sigmoid(x):
    return 1.0 / (1.0 + jnp.exp(-x))


def _silu(x):
    return x * _sigmoid(x)


def _layer_norm_rows(v, g, b):
    mu = jnp.mean(v, axis=-1, keepdims=True)
    d = v - mu
    var = jnp.mean(d * d, axis=-1, keepdims=True)
    return d * lax.rsqrt(var + LN_EPS) * g + b


def _rms_norm_rows(v, g):
    return v * lax.rsqrt(jnp.mean(v * v, axis=-1, keepdims=True) + RMS_EPS) * g


def _in_proj_kernel(x_ref, w_ref, qg_ref, wuq_ref, wuk_ref, wqi_ref, kvg_ref, lng_ref, lnb_ref,
                    qlat_ref, qidx_ref, ckv_ref, ckvt_ref, kidx_ref, misc_ref, misct_ref, qkv_ref, z_ref):
    xb = x_ref[...].astype(BF16)
    proj = _dot(xb, w_ref[...])
    o = 0
    c_q = proj[:, o:o + A_Q_RANK]; o += A_Q_RANK
    c_kv = proj[:, o:o + A_KV_RANK]; o += A_KV_RANK
    misc = proj[:, o:o + MISC_W]; o += MISC_W
    qkv_ref[...] = proj[:, o:o + DN_QKV_W]; o += DN_QKV_W
    z_ref[...] = proj[:, o:o + DN_VAL_W]

    cqn_t = _rms_norm_rows(c_q, qg_ref[...]).T.astype(BF16)
    q_t = _dot(wuq_ref[...], cqn_t).astype(BF16)
    qlat_t = _dot(wuk_ref[...], q_t) * (A_QK_DIM ** -0.5 * LOG2E)
    qidx_t = _dot(wqi_ref[...], cqn_t)
    for h in range(A_HEADS):
        qlat_ref[h] = qlat_t[h * A_KV_RANK:(h + 1) * A_KV_RANK, :].astype(BF16)
        qidx_ref[h] = qidx_t[h * IDX_DIM:(h + 1) * IDX_DIM, :].astype(BF16)
    ckv = _rms_norm_rows(c_kv, kvg_ref[...])
    ckv_ref[...] = ckv.astype(BF16)
    ckvt_ref[0:A_KV_RANK, :] = ckv.T.astype(BF16)
    ckvt_ref[A_KV_RANK:KVT_ROWS, :] = jnp.ones((KVT_ROWS - A_KV_RANK, ckv.shape[0]), BF16)
    kidx_ref[...] = _layer_norm_rows(misc[:, :IDX_DIM], lng_ref[...], lnb_ref[...]).astype(BF16)
    misc_ref[...] = misc
    misct_ref[...] = misc.T


def _in_proj(x2, wc, qg, wuq, wuk, wqi, kvg, lng, lnb, *, tm):
    T = x2.shape[0]
    const = lambda *s: pl.BlockSpec(s, lambda i: (0,) * len(s))
    return pl.pallas_call(
        _in_proj_kernel,
        grid=(T // tm,),
        in_specs=[
            pl.BlockSpec((tm, D_MODEL), lambda i: (i, 0)),
            const(D_MODEL, PROJ_W), const(1, A_Q_RANK), const(A_HEADS * A_QK_DIM, A_Q_RANK),
            const(A_HEADS * A_KV_RANK, A_HEADS * A_QK_DIM), const(IDX_HEADS * IDX_DIM, A_Q_RANK),
            const(1, A_KV_RANK), const(1, IDX_DIM), const(1, IDX_DIM),
        ],
        out_specs=[
            pl.BlockSpec((A_HEADS, A_KV_RANK, tm), lambda i: (0, 0, i)),
            pl.BlockSpec((IDX_HEADS, IDX_DIM, tm), lambda i: (0, 0, i)),
            pl.BlockSpec((tm, A_KV_RANK), lambda i: (i, 0)),
            pl.BlockSpec((KVT_ROWS, tm), lambda i: (0, i)),
            pl.BlockSpec((tm, IDX_DIM), lambda i: (i, 0)),
            pl.BlockSpec((tm, MISC_W), lambda i: (i, 0)),
            pl.BlockSpec((MISC_W, tm), lambda i: (0, i)),
            pl.BlockSpec((tm, DN_QKV_W), lambda i: (i, 0)),
            pl.BlockSpec((tm, DN_VAL_W), lambda i: (i, 0)),
        ],
        out_shape=[
            jax.ShapeDtypeStruct((A_HEADS, A_KV_RANK, T), BF16),
            jax.ShapeDtypeStruct((IDX_HEADS, IDX_DIM, T), BF16),
            jax.ShapeDtypeStruct((T, A_KV_RANK), BF16),
            jax.ShapeDtypeStruct((KVT_ROWS, T), BF16),
            jax.ShapeDtypeStruct((T, IDX_DIM), BF16),
            jax.ShapeDtypeStruct((T, MISC_W), F32),
            jax.ShapeDtypeStruct((MISC_W, T), F32),
            jax.ShapeDtypeStruct((T, DN_QKV_W), F32),
            jax.ShapeDtypeStruct((T, DN_VAL_W), F32),
        ],
        compiler_params=pltpu.CompilerParams(dimension_semantics=("parallel",), vmem_limit_bytes=VMEM_LIMIT),
        name="in_proj",
    )(x2, wc, qg, wuq, wuk, wqi, kvg, lng, lnb)


def _dsa_kernel(qidx_ref, qlat_ref, wt_ref, kidx_ref, ckv_ref, ckvt_ref, wuvt_ref, g_ref, out_ref,
                keys_ref, hi_ref, lo_ref, acc_ref, stage_ref, bias_ref, *, qb, kb, rb, slab, topk, idx_bits):
    H = A_HEADS
    j = pl.program_id(1)
    q0 = j * qb
    nkeys = (j + 1) * qb
    nrblk = (nkeys + rb - 1) // rb
    ups = rb // kb
    nchunk = nrblk * ups
    w = wt_ref[...] * (IDX_HEADS ** -0.5 * IDX_DIM ** -0.5)
    row_minus_lane = (lax.broadcasted_iota(I32, (slab, qb), 0) - lax.broadcasted_iota(I32, (slab, qb), 1))

    def skewed(nparts, consume, produce, carry, on_tail=None):
        def chunk(c, k, carry):
            if ups > 1:
                for i in range(nparts):
                    carry = produce(c + 1, (k + 1) % ups, i, consume(c, k, i, carry))
                return carry
            for i in range(nparts):
                carry = consume(c, k, i, carry)
            for i in range(nparts):
                carry = produce(c + 1, k, i, carry)
            return carry

        def step(i, carry):
            for k in range(ups):
                carry = chunk(i * ups + k, k, carry)
            return carry

        for i in range(nparts):
            carry = produce(0, 0, i, carry)
        carry = lax.fori_loop(0, nrblk - 1, step, carry)
        for k in range(ups - 1):
            carry = chunk((nrblk - 1) * ups + k, k, carry)
        last = nchunk - 1

        @pl.when(last * kb < nkeys)
        def _():
            for i in range(nparts):
                consume(last, ups - 1, i, carry)

        if on_tail is not None:
            @pl.when(last * kb >= nkeys)
            def _():
                on_tail(last)
        return carry

    nslab = kb // slab
    hps = H // nslab

    def score_matmuls(c, slot, i, carry):
        kc = kidx_ref[pl.ds(pl.multiple_of(c * kb, kb), kb), :]
        for h in range(i * hps, (i + 1) * hps):
            stage_ref[slot, h] = _dot(kc, qidx_ref[h])
        return carry

    def score_keys(c, slot, i, carry):
        r0 = pl.multiple_of(c * kb, kb) + i * slab
        s = jnp.zeros((slab, qb), F32)
        for h in range(H):
            s = s + jnp.maximum(stage_ref[slot, h, i * slab:(i + 1) * slab, :], 0.0) * w[h:h + 1, :]
        bits = lax.bitcast_convert_type(s, I32)
        key = jnp.where(bits < 0, INT_MIN - bits, bits)
        key = jnp.where(row_minus_lane <= q0 - r0, key, NEG_KEY)
        keys_ref[pl.ds(r0, slab), :] = key
        hi_ref[pl.ds(r0, slab), :] = jnp.right_shift(key, 16).astype(I16)
        lo_ref[pl.ds(r0, slab), :] = ((key & 0xFFFF) - HALF16).astype(I16)
        return carry

    def score_tail(c):
        r0 = pl.multiple_of(c * kb, kb)
        keys_ref[pl.ds(r0, kb), :] = jnp.full((kb, qb), NEG_KEY, I32)
        hi_ref[pl.ds(r0, kb), :] = jnp.full((kb, qb), NEG_KEY >> 16, I16)
        lo_ref[pl.ds(r0, kb), :] = jnp.full((kb, qb), (NEG_KEY & 0xFFFF) - HALF16, I16)

    skewed(nslab, score_keys, score_matmuls, 0, score_tail)

    nacc = 4

    def count16(ref, cand, below=False):
        def body(r, accs):
            r0 = pl.multiple_of(r * rb, rb)
            k = ref[pl.ds(r0, rb), :]
            m = jnp.where(k < cand if below else k >= cand, jnp.int16(1), jnp.int16(0))
            accs = list(accs)
            for i in range(rb // PACK16):
                accs[i % nacc] = accs[i % nacc] + m[i * PACK16:(i + 1) * PACK16, :]
            return tuple(accs)
        accs = lax.fori_loop(0, nrblk, body, tuple(jnp.zeros((PACK16, qb), I16) for _ in range(nacc)))
        tot = accs[0].astype(I32)
        for a in accs[1:]:
            tot = tot + a.astype(I32)
        return tot.sum(axis=0, keepdims=True)

    def max16_below(ref, bound):
        def body(r, accs):
            r0 = pl.multiple_of(r * rb, rb)
            k = ref[pl.ds(r0, rb), :]
            m = jnp.where(k < bound, k, jnp.int16(-1))
            accs = list(accs)
            for i in range(rb // PACK16):
                tile = m[i * PACK16:(i + 1) * PACK16, :]
                accs[i % nacc] = jnp.where(tile > accs[i % nacc], tile, accs[i % nacc])
            return tuple(accs)
        accs = lax.fori_loop(0, nrblk, body, tuple(jnp.full((PACK16, qb), -1, I16) for _ in range(nacc)))
        top = accs[0].astype(I32)
        for a in accs[1:]:
            top = jnp.maximum(top, a.astype(I32))
        return top.max(axis=0, keepdims=True)

    def bisect16(ref, target, cnt_all):
        def body(p, carry):
            t_u, cnt_ok, cnt_rej = carry
            cand_u = t_u | jnp.left_shift(jnp.int32(1), 15 - p)
            cnt = count16(ref, (cand_u - HALF16).astype(I16))
            ok = cnt >= target
            return jnp.where(ok, cand_u, t_u), jnp.where(ok, cnt, cnt_ok), jnp.where(ok, cnt_rej, cnt)
        return lax.fori_loop(0, 16, body, (jnp.zeros((1, qb), I32), cnt_all, jnp.zeros((1, qb), I32)))

    hi_u, cnt_ge_hi, cnt_gt_hi = bisect16(hi_ref, topk, jnp.full((1, qb), nrblk * rb, I32))
    hi_t = (hi_u - HALF16).astype(I16)

    def mask_lo(r, carry):
        r0 = pl.multiple_of(r * rb, rb)
        lo_ref[pl.ds(r0, rb), :] = jnp.where(hi_ref[pl.ds(r0, rb), :] == hi_t, lo_ref[pl.ds(r0, rb), :],
                                             jnp.int16(-HALF16))
        return carry

    lax.fori_loop(0, nrblk, mask_lo, 0)
    lo_u, cnt_ge_lo, cnt_gt_lo = bisect16(lo_ref, topk - cnt_gt_hi, cnt_ge_hi - cnt_gt_hi)
    t = jnp.left_shift(hi_u - HALF16, 16) + lo_u
    cnt_t = cnt_gt_hi + cnt_ge_lo

    excess = jnp.where(t > NEG_KEY, cnt_t - topk, 0)

    worst = jnp.max(excess)

    @pl.when(worst > 0)
    def _():
        keep = topk - (cnt_gt_hi + cnt_gt_lo)
        lo_t = (lo_u - HALF16).astype(I16)

        def tie_rows(r, carry):
            r0 = pl.multiple_of(r * rb, rb)
            tie = (hi_ref[pl.ds(r0, rb), :] == hi_t) & (lo_ref[pl.ds(r0, rb), :] == lo_t)
            row = (r0 + lax.broadcasted_iota(I32, (rb, qb), 0)).astype(I16)
            lo_ref[pl.ds(r0, rb), :] = jnp.where(tie, row, jnp.int16(HALF16 - 1))
            return carry

        lax.fori_loop(0, nrblk, tie_rows, 0)

        def demote_from(first):
            def demote(r, carry):
                r0 = pl.multiple_of(r * rb, rb)
                k = keys_ref[pl.ds(r0, rb), :]
                row = r0 + lax.broadcasted_iota(I32, (rb, qb), 0)
                keys_ref[pl.ds(r0, rb), :] = jnp.where((k == t) & (row >= first), NEG_KEY, k)
                return carry
            lax.fori_loop(0, nrblk, demote, 0)

        @pl.when(worst <= TIE_SCAN_MAX)
        def _():
            def peel(c):
                first, left = c
                top = max16_below(lo_ref, first.astype(I16))
                return jnp.where(left > 0, top, first), jnp.maximum(left - 1, 0)
            first, _ = lax.while_loop(lambda c: jnp.max(c[1]) > 0, peel,
                                      (jnp.full((1, qb), HALF16 - 1, I32), jnp.maximum(excess, 0)))
            demote_from(first)

        @pl.when(worst > TIE_SCAN_MAX)
        def _():
            def idx_body(p, pos):
                cand = pos + jnp.left_shift(jnp.int32(1), idx_bits - 1 - p)
                return jnp.where(count16(lo_ref, cand.astype(I16), below=True) < keep, cand, pos)
            demote_from(lax.fori_loop(0, idx_bits, idx_body, jnp.zeros((1, qb), I32)) + 1)

    thr = jnp.maximum(t, NEG_KEY + 1)

    acc_ref[...] = jnp.zeros(acc_ref.shape, F32)

    def put(carry, h, item):
        return carry[:h] + (item,) + carry[h + 1:]

    def attn_logits(c, slot, h, carry):
        r0 = pl.multiple_of(c * kb, kb)
        if h == 0:
            bias_ref[slot] = jnp.where(keys_ref[pl.ds(r0, kb), :] >= thr, 0.0, -jnp.inf)
        m_prev = carry[h][1]
        lg = _dot(ckv_ref[pl.ds(r0, kb), :], qlat_ref[h]) + bias_ref[slot]
        stage_ref[slot, h] = lg
        return put(carry, h, (m_prev, jnp.maximum(m_prev, jnp.max(lg, axis=0, keepdims=True))))

    def attn_values(c, slot, h, carry):
        m_old, m_new = carry[h]
        kvt = ckvt_ref[:, pl.ds(pl.multiple_of(c * kb, kb), kb)]
        p = jnp.exp2(stage_ref[slot, h] - m_new).astype(BF16)
        acc_ref[h] = acc_ref[h] * jnp.exp2(m_old - m_new) + _dot(kvt, p)
        return carry

    m_init = jnp.full((1, qb), -1e30, F32)
    skewed(H, attn_values, attn_logits, ((m_init, m_init),) * H)

    out_t = jnp.concatenate(
        [_dot(wuvt_ref[h], (acc_ref[h, 0:A_KV_RANK, :] / acc_ref[h, A_KV_RANK:A_KV_RANK + 1, :]).astype(BF16))
         for h in range(H)], axis=0)
    out_ref[...] = _rms_norm_rows(out_t.T, g_ref[...]).astype(out_ref.dtype)


def _dsa(qidx, qlat, misct, kidx, ckv, ckvt, wuvt, g, *, B, S):
    T = B * S
    qb = min(256, S)
    nq = S // qb
    kb = qb
    rb = 2 * qb if nq % 2 == 0 else qb
    topk = min(TOPK_MAX, S // 4)
    idx_bits = max(1, (S - 1).bit_length())
    assert S < HALF16, "tie-breaking keeps key positions in int16"
    slab = min(64, kb)
    kern = functools.partial(_dsa_kernel, qb=qb, kb=kb, rb=rb, slab=slab, topk=topk, idx_bits=idx_bits)
    return pl.pallas_call(
        kern,
        grid=(B, nq),
        in_specs=[
            pl.BlockSpec((IDX_HEADS, IDX_DIM, qb), lambda b, j: (0, 0, b * nq + j)),
            pl.BlockSpec((A_HEADS, A_KV_RANK, qb), lambda b, j: (0, 0, b * nq + j)),
            pl.BlockSpec((IDX_HEADS, qb), lambda b, j: (MISC_WIDX // IDX_HEADS, b * nq + j)),
            pl.BlockSpec((S, IDX_DIM), lambda b, j: (b, 0)),
            pl.BlockSpec((S, A_KV_RANK), lambda b, j: (b, 0)),
            pl.BlockSpec((KVT_ROWS, S), lambda b, j: (0, b)),
            pl.BlockSpec((A_HEADS, A_V_DIM, A_KV_RANK), lambda b, j: (0, 0, 0)),
            pl.BlockSpec((1, A_OUT_W), lambda b, j: (0, 0)),
        ],
        out_specs=pl.BlockSpec((qb, A_OUT_W), lambda b, j: (b * nq + j, 0)),
        out_shape=jax.ShapeDtypeStruct((T, A_OUT_W), BF16),
        scratch_shapes=[
            pltpu.VMEM((S, qb), I32),
            pltpu.VMEM((S, qb), I16),
            pltpu.VMEM((S, qb), I16),
            pltpu.VMEM((A_HEADS, KVT_ROWS, qb), F32),
            pltpu.VMEM((rb // kb, A_HEADS, kb, qb), F32),
            pltpu.VMEM((rb // kb, kb, qb), F32),
        ],
        compiler_params=pltpu.CompilerParams(
            dimension_semantics=("parallel", "arbitrary"), vmem_limit_bytes=VMEM_LIMIT),
        name="dsa",
    )(qidx, qlat, misct, kidx, ckv, ckvt, wuvt, g)


def _gdn_constants(sb):
    r, c = np.arange(sb)[:, None], np.arange(sb)[None, :]
    same = (r // GDN_CHUNK) == (c // GDN_CHUNK)
    tri = same & (c <= r)
    diag = [((r // b) == (c // b)).astype(np.float32) for b in (8, 16, 32)] + [same.astype(np.float32)]
    eye = (r == c).astype(np.float32)
    mats = [diag[0]] + [diag[i] - diag[i - 1] for i in (1, 2, 3)] + [eye, 1.0 - eye, np.where(tri, 0.0, -np.inf)]
    mask_c = np.concatenate([tri, same], axis=0).astype(np.float32)
    mask_r = np.concatenate([same & (r <= c), same], axis=1).astype(np.float32)
    spread = (np.arange(DN_KEY_W)[None, :] // DN_K_DIM == np.arange(DN_HEADS)[:, None]).astype(np.float32)
    spread3 = np.zeros((3, 3, 3, DN_HEADS, DN_KEY_W), np.float32)
    for j in range(3):
        spread3[j, :, j] = spread
    return (jnp.asarray(np.stack(mats), F32), jnp.asarray(mask_c, BF16), jnp.asarray(mask_r, BF16),
            jnp.asarray(spread3.reshape(3, 9 * DN_HEADS, DN_KEY_W), BF16), jnp.asarray(np.tile(spread, (1, 2)), BF16))


def _gdn_kernel(qkv_ref, z_ref, misc_ref, at_ref, cw_ref, alog_c_ref, dtb_c_ref, alog_r_ref, dtb_r_ref,
                ng_ref, fm_ref, mc_ref, mr_ref, sr_ref, sc_ref, out_ref, xbuf_ref, state_ref, *, tb, sb):
    C = GDN_CHUNK
    nck = tb // C
    H = DN_HEADS
    step = pl.program_id(1)

    @pl.when(step == 0)
    def _():
        xbuf_ref[0:SUBLANES, :] = jnp.zeros((SUBLANES, DN_QKV_W), F32)
        state_ref[...] = jnp.zeros(state_ref.shape, F32)

    xbuf_ref[SUBLANES:SUBLANES + tb, :] = qkv_ref[...]
    xe = xbuf_ref[...]
    y = xe[SUBLANES:] * cw_ref[DN_CONV - 1:DN_CONV, :]
    for d in range(1, DN_CONV):
        y = y + pltpu.roll(xe, d, 0)[SUBLANES:] * cw_ref[DN_CONV - 1 - d:DN_CONV - d, :]
    xbuf_ref[0:SUBLANES, :] = xe[tb:tb + SUBLANES]
    y = _silu(y)

    def softplus(v):
        return jnp.maximum(v, 0.0) + jnp.log(1.0 + jnp.exp(-jnp.abs(v)))

    misc = misc_ref[...]
    beta_c = _sigmoid(misc[:, MISC_B:MISC_B + H])
    g_c = -jnp.exp(alog_r_ref[...]) * softplus(misc[:, MISC_A:MISC_A + H] + dtb_r_ref[...])
    g_r = -jnp.exp(alog_c_ref[...]) * softplus(at_ref[...] + dtb_c_ref[...])

    rows = [slice(b * sb, (b + 1) * sb) for b in range(tb // sb)]
    mask_c = mc_ref[...]
    mask_r = mr_ref[...]

    def pieces(v, axis, n=3):
        out, rest = [], v
        for _ in range(n - 1):
            out.append(rest.astype(BF16))
            rest = rest - out[-1].astype(F32)
        return jnp.concatenate(out + [rest.astype(BF16)], axis=axis)

    sums_c = [_dot(mask_c, pieces(g_c[r], 1)) for r in rows]
    sums_c = [s[:, 0:H] + s[:, H:2 * H] + s[:, 2 * H:3 * H] for s in sums_c]
    sums_r = [_dot(pieces(g_r[:, r], 0), mask_r) for r in rows]
    sums_r = [s[0:H] + s[H:2 * H] + s[2 * H:3 * H] for s in sums_r]
    gcum_c = [s[0:sb] for s in sums_c]
    glast_c = [s[sb:2 * sb] for s in sums_c]
    gcum_r = [s[:, 0:sb] for s in sums_r]
    glast_r = [s[:, sb:2 * sb] for s in sums_r]
    blk8, ring_f, eye, off_diag, tri_bias = fm_ref[0], [fm_ref[1], fm_ref[2], fm_ref[3]], fm_ref[4], fm_ref[5], fm_ref[6]

    hs = range(H)
    bh = [(b, h) for b in range(len(rows)) for h in hs]
    ch = range(len(bh))
    def expand(*vs):
        p = pieces(jnp.concatenate(list(vs) + [vs[-1]] * (3 - len(vs)), axis=1), 1)
        return [_dot(p, sr_ref[j]) for j in range(len(vs))]

    head_sums = lambda v: _dot_nt(pieces(v, 1, 2), sc_ref[...])
    yq, yk, yv = y[:, :DN_KEY_W], y[:, DN_KEY_W:2 * DN_KEY_W], y[:, 2 * DN_KEY_W:]
    gc_all = jnp.concatenate(gcum_c, axis=0)
    eg_c = jnp.exp(gc_all)
    rq_x, rk_x, beta_x = expand(lax.rsqrt(head_sums(yq * yq) + RMS_EPS) * (DN_K_DIM ** -0.5),
                                lax.rsqrt(head_sums(yk * yk) + RMS_EPS), beta_c)
    eg_x, egl_x = expand(eg_c, jnp.exp(jnp.concatenate(glast_c, axis=0) - gc_all))
    q_n, k_n = yq * rq_x, yk * rk_x
    kb_all = k_n * beta_x
    k_bf, kb_bf, q_bf = k_n.astype(BF16), kb_all.astype(BF16), q_n.astype(BF16)
    rhs_v, rhs_k = (yv * beta_x).astype(BF16), (kb_all * eg_x).astype(BF16)
    qd_all = (q_n * eg_x).astype(BF16)
    ke_all = (k_n * egl_x).astype(BF16)
    head = lambda a, i: a[rows[bh[i][0]], bh[i][1] * DN_K_DIM:(bh[i][1] + 1) * DN_K_DIM]
    gcs = [gcum_c[b][:, h:h + 1] for b, h in bh]
    kk = [_dot_nt(head(kb_bf, i), head(k_bf, i)) for i in ch]
    qk = [_dot_nt(head(q_bf, i), head(k_bf, i)) for i in ch]
    decay = [jnp.exp((gcs[i] - gcum_r[b][h:h + 1, :]) + tri_bias) for i, (b, h) in enumerate(bh)]
    lower = [kk[i] * (decay[i] * off_diag) for i in ch]
    attn = [(qk[i] * decay[i]).astype(BF16) for i in ch]
    pw = [lower[i] * blk8 for i in ch]
    x_inv = [eye - pw[i] for i in ch]
    for _ in range(2):
        pwb = [p.astype(BF16) for p in pw]
        pw = [_dot(p, p) for p in pwb]
        x_inv = [x_inv[i] + _dot(x_inv[i].astype(BF16), pw[i].astype(BF16)) for i in ch]
    for ring in ring_f:
        xb = [x.astype(BF16) for x in x_inv]
        t1 = [_dot(xb[i], (lower[i] * ring).astype(BF16)).astype(BF16) for i in ch]
        x_inv = [x_inv[i] - _dot(t1[i], xb[i]) for i in ch]
    uw = [_dot(x_inv[i].astype(BF16), jnp.concatenate([head(rhs_v, i), head(rhs_k, i)], axis=1)) for i in ch]
    us = [m[:, :DN_V_DIM] for m in uw]
    wkb = [m[:, DN_V_DIM:].astype(BF16) for m in uw]
    qd = [head(qd_all, i) for i in ch]
    ke = [head(ke_all, i) for i in ch]

    st = [state_ref[h] for h in hs]
    outs = [[] for _ in hs]
    for n in range(nck):
        b, r0 = divmod(n * C, sb)
        r = slice(r0, r0 + C)
        ws = [_dot(jnp.concatenate([wkb[b * H + h][r], qd[b * H + h][r]], axis=0), st[h].astype(BF16)) for h in hs]
        v_nb = [(us[b * H + h][r] - ws[h][0:C]).astype(BF16) for h in hs]
        for h in hs:
            outs[h].append(ws[h][C:2 * C] + _dot(attn[b * H + h][r, r], v_nb[h]))
        cd = [jnp.exp(glast_r[b][h:h + 1, r0:r0 + 1]) for h in hs]
        st = [st[h] * cd[h] + _dot_tn(ke[b * H + h][r], v_nb[h]) for h in hs]
    for h in hs:
        state_ref[h] = st[h]
    o = jnp.concatenate([jnp.concatenate(outs[h], axis=0) for h in hs], axis=1)
    r_o = lax.rsqrt(head_sums(o * o) * (1.0 / DN_V_DIM) + RMS_EPS)
    o = o * expand(r_o)[0] * ng_ref[...]
    out_ref[...] = (o * _silu(z_ref[...])).astype(out_ref.dtype)


def _gdn(qkv, z, misc, misct, conv_w, a_log, dt_bias, norm_g, *, B, S):
    T = B * S
    sb = min(GDN_SOLVE, S)
    tb = min(GDN_SOLVES_PER_STEP * sb, S)
    ns = S // tb
    H = DN_HEADS
    kern = functools.partial(_gdn_kernel, tb=tb, sb=sb)
    consts = _gdn_constants(sb)
    const = lambda *s: pl.BlockSpec(s, lambda b, i: (0,) * len(s))
    return pl.pallas_call(
        kern,
        grid=(B, ns),
        in_specs=[
            pl.BlockSpec((tb, DN_QKV_W), lambda b, i: (b * ns + i, 0)),
            pl.BlockSpec((tb, DN_VAL_W), lambda b, i: (b * ns + i, 0)),
            pl.BlockSpec((tb, MISC_W), lambda b, i: (b * ns + i, 0)),
            pl.BlockSpec((H, tb), lambda b, i: (MISC_A // H, b * ns + i)),
            const(DN_CONV, DN_QKV_W), const(H, 1), const(H, 1), const(1, H), const(1, H), const(1, DN_VAL_W),
        ] + [const(*c.shape) for c in consts],
        out_specs=pl.BlockSpec((tb, DN_VAL_W), lambda b, i: (b * ns + i, 0)),
        out_shape=jax.ShapeDtypeStruct((T, DN_VAL_W), BF16),
        scratch_shapes=[
            pltpu.VMEM((tb + SUBLANES, DN_QKV_W), F32),
            pltpu.VMEM((H, DN_K_DIM, DN_V_DIM), F32),
        ],
        compiler_params=pltpu.CompilerParams(
            dimension_semantics=("parallel", "arbitrary"), vmem_limit_bytes=VMEM_LIMIT),
        name="gdn",
    )(qkv, z, misc, misct, conv_w, a_log.reshape(H, 1), dt_bias.reshape(H, 1),
      a_log.reshape(1, H), dt_bias.reshape(1, H), jnp.tile(norm_g.reshape(1, DN_V_DIM), (1, H)), *consts)


def _out_proj_kernel(a_ref, d_ref, x_ref, wa_ref, wd_ref, g_ref, b_ref, h_ref, *, alpha):
    for r in range(0, a_ref.shape[0], ROW_GROUP):
        rows = slice(r, r + ROW_GROUP)
        mix = _dot(a_ref[rows, :], wa_ref[...]) + _dot(d_ref[rows, :], wd_ref[...])
        h_ref[rows, :] = _layer_norm_rows(alpha * x_ref[rows, :] + mix, g_ref[...], b_ref[...])


def _out_proj(a, d, x2, wa, wd, g, b, *, alpha, tm):
    T = x2.shape[0]
    const = lambda *s: pl.BlockSpec(s, lambda i: (0,) * len(s))
    return pl.pallas_call(
        functools.partial(_out_proj_kernel, alpha=alpha),
        grid=(T // tm,),
        in_specs=[
            pl.BlockSpec((tm, A_OUT_W), lambda i: (i, 0)),
            pl.BlockSpec((tm, DN_VAL_W), lambda i: (i, 0)),
            pl.BlockSpec((tm, D_MODEL), lambda i: (i, 0)),
            const(A_OUT_W, D_MODEL), const(DN_VAL_W, D_MODEL), const(1, D_MODEL), const(1, D_MODEL),
        ],
        out_specs=pl.BlockSpec((tm, D_MODEL), lambda i: (i, 0)),
        out_shape=jax.ShapeDtypeStruct((T, D_MODEL), F32),
        compiler_params=pltpu.CompilerParams(dimension_semantics=("parallel",), vmem_limit_bytes=VMEM_LIMIT),
        name="out_proj",
    )(a, d, x2, wa, wd, g, b)


def _ffn_kernel(h_ref, halo_ref, wg_ref, wu_ref, cw_ref, cb_ref, wd_ref, g_ref, b_ref, out_ref,
                acc_ref, *, alpha, tiles_per_seq):
    i = pl.program_id(0)
    f = pl.program_id(1)
    nf = pl.num_programs(1)
    hb = h_ref[...].astype(BF16)
    wg = wg_ref[...]
    gate = _dot(hb, wg)
    up = _dot(hb, wu_ref[...])
    halo = _dot(halo_ref[...].astype(BF16), wg)
    halo = jnp.where(i % tiles_per_seq == 0, 0.0, halo)
    ge = jnp.concatenate([halo, gate], axis=0)
    conv = gate * cw_ref[FFN_CONV - 1:FFN_CONV, :]
    for d in range(1, FFN_CONV):
        conv = conv + pltpu.roll(ge, d, 0)[SUBLANES:] * cw_ref[FFN_CONV - 1 - d:FFN_CONV - d, :]
    act = (_silu(conv + cb_ref[...]) * up).astype(BF16)
    part = _dot(act, wd_ref[...])

    @pl.when(f == 0)
    def _():
        acc_ref[...] = part

    @pl.when(f > 0)
    def _():
        acc_ref[...] += part

    @pl.when(f == nf - 1)
    def _():
        out_ref[...] = _layer_norm_rows(alpha * h_ref[...] + acc_ref[...], g_ref[...], b_ref[...])


def _ffn(h, wg, wu, cw, cb, wd, g, b, *, alpha, S, tm, tf):
    T = h.shape[0]
    nf = D_FF // tf
    hs = tm // SUBLANES
    kern = functools.partial(_ffn_kernel, alpha=alpha, tiles_per_seq=S // tm)
    return pl.pallas_call(
        kern,
        grid=(T // tm, nf),
        in_specs=[
            pl.BlockSpec((tm, D_MODEL), lambda i, f: (i, 0)),
            pl.BlockSpec((SUBLANES, D_MODEL), lambda i, f: (jnp.maximum(i * hs - 1, 0), 0)),
            pl.BlockSpec((D_MODEL, tf), lambda i, f: (0, f)),
            pl.BlockSpec((D_MODEL, tf), lambda i, f: (0, f)),
            pl.BlockSpec((FFN_CONV, tf), lambda i, f: (0, f)),
            pl.BlockSpec((1, tf), lambda i, f: (0, f)),
            pl.BlockSpec((tf, D_MODEL), lambda i, f: (f, 0)),
            pl.BlockSpec((1, D_MODEL), lambda i, f: (0, 0)),
            pl.BlockSpec((1, D_MODEL), lambda i, f: (0, 0)),
        ],
        out_specs=pl.BlockSpec((tm, D_MODEL), lambda i, f: (i, 0)),
        out_shape=jax.ShapeDtypeStruct((T, D_MODEL), F32),
        scratch_shapes=[pltpu.VMEM((tm, D_MODEL), F32)],
        compiler_params=pltpu.CompilerParams(
            dimension_semantics=("parallel", "arbitrary"), vmem_limit_bytes=VMEM_LIMIT),
        name="ffn",
    )(h, h, wg, wu, cw, cb, wd, g, b)


def _regroup_w_in(w):
    offs = [0]
    for s in IN_SIZES:
        offs.append(offs[-1] + s)
    cq, ckv, kidx, widx, qkv, z, b, a = (w[:, offs[i]:offs[i + 1]] for i in range(8))
    pad = jnp.zeros((w.shape[0], MISC_W - (IDX_DIM + IDX_HEADS + 2 * DN_HEADS)), w.dtype)
    return jnp.concatenate([cq, ckv, kidx, widx, b, a, pad, qkv, z], axis=1).astype(BF16)


def _layer(x2, p, *, B, S, alpha):
    row = lambda v: v.reshape(1, -1)
    wc = _regroup_w_in(p["w_in"])
    wuk_bd_t = jnp.einsum("hdc,hg->hcgd", p["w_uk"], jnp.eye(A_HEADS, dtype=F32)).reshape(
        A_HEADS * A_KV_RANK, A_HEADS * A_QK_DIM)
    qlat, qidx, ckv, ckvt, kidx, misc, misct, qkv, z = _in_proj(
        x2, wc, row(p["q_norm_g"]), p["w_uq"].T.astype(BF16), wuk_bd_t.astype(BF16), p["w_qidx"].T.astype(BF16),
        row(p["kv_norm_g"]),
        row(p["kidx_ln_g"]), row(p["kidx_ln_b"]), tm=min(512, S))
    a_out = _dsa(qidx, qlat, misct, kidx, ckv, ckvt, p["w_uv"].transpose(0, 2, 1).astype(BF16),
                 row(p["attn_out_g"]), B=B, S=S)
    d_out = _gdn(qkv, z, misc, misct, p["dn_conv_w"], p["dn_a_log"], p["dn_dt_bias"], p["dn_norm_g"],
                 B=B, S=S)
    w_out = p["w_out"].astype(BF16)
    h = _out_proj(a_out, d_out, x2, w_out[:A_OUT_W], w_out[A_OUT_W:], row(p["ln1_g"]), row(p["ln1_b"]),
                  alpha=alpha, tm=min(512, S))
    w_ffn = p["ffn_w_in"].astype(BF16)
    return _ffn(h, w_ffn[:, :D_FF], w_ffn[:, D_FF:], p["ffn_conv_w"], row(p["ffn_conv_b"]),
                p["ffn_w_down"].astype(BF16), row(p["ln2_g"]), row(p["ln2_b"]),
                alpha=alpha, S=S, tm=min(512, S), tf=D_FF // 2)


_PARAM_NAMES = ("w_in", "q_norm_g", "w_uq", "w_qidx", "kv_norm_g", "w_uk", "w_uv", "kidx_ln_g", "kidx_ln_b",
                "attn_out_g", "dn_conv_w", "dn_a_log", "dn_dt_bias", "dn_norm_g", "w_out", "ln1_g", "ln1_b",
                "ffn_w_in", "ffn_conv_w", "ffn_conv_b", "ffn_w_down", "ln2_g", "ln2_b")


def kernel(x, w_in, q_norm_g, w_uq, w_qidx, kv_norm_g, w_uk, w_uv, kidx_ln_g, kidx_ln_b, attn_out_g, dn_conv_w, dn_a_log, dn_dt_bias, dn_norm_g, w_out, ln1_g, ln1_b, ffn_w_in, ffn_conv_w, ffn_conv_b, ffn_w_down, ln2_g, ln2_b):
    params = (w_in, q_norm_g, w_uq, w_qidx, kv_norm_g, w_uk, w_uv, kidx_ln_g, kidx_ln_b, attn_out_g, dn_conv_w,
              dn_a_log, dn_dt_bias, dn_norm_g, w_out, ln1_g, ln1_b, ffn_w_in, ffn_conv_w, ffn_conv_b, ffn_w_down,
              ln2_g, ln2_b)
    B, S, D = x.shape
    depth = w_in.shape[0]
    alpha = (2 * depth) ** 0.25
    x2 = x.reshape(B * S, D)
    for l in range(depth):
        x2 = _layer(x2, {n: v[l] for n, v in zip(_PARAM_NAMES, params)}, B=B, S=S, alpha=alpha)
    return x2.reshape(B, S, D)
```
